```python
import jax, jax.numpy as jnp
from jax import lax
import numpy as np

D_MODEL = 1024
BATCH = 16
SEQ = 2048
DEPTH = 2

GRID_W = 64
CTX_LEN = 256
HEAD_DIM = 64
ROPE_THETA = 10000.0
NORM_EPS = 1e-6
Q_BLOCK = 128

A_HEADS = 8
A_KV_HEADS = 2
A_Q_W = A_HEADS * HEAD_DIM
A_KV_W = A_KV_HEADS * HEAD_DIM
B_HEADS = 8
B_Q_LORA = 384
B_KV_LORA = 256
B_NOPE = 64
B_ROPE = 32
B_V = 64
C_HEADS = 8
C_W = C_HEADS * HEAD_DIM
C_DECAY_LORA = 64
C_AAA_LORA = 64
C_GATE_LORA = 128
C_GN_EPS = 64e-5
D_HEADS = 8
D_W = D_HEADS * HEAD_DIM
NA_ROWS = 8
NA_COLS = 16

N_BRANCH = 4
BRANCH_W = 512
D_FF = 4 * D_MODEL

A_IN = A_Q_W + 2 * A_KV_W
B_IN = B_Q_LORA + B_KV_LORA + B_ROPE
C_IN = 3 * C_W + 2 * C_DECAY_LORA + 2 * C_AAA_LORA + C_GATE_LORA
D_IN = 3 * D_W
GATE_IN = N_BRANCH * D_MODEL
N_IN = A_IN + B_IN + C_IN + D_IN + GATE_IN

kernel_name = 'hybrid_prefix_flow_block'


def _split(p, widths):
    cuts = [int(v) for v in np.cumsum(widths)[:-1]]
    return jnp.split(p, cuts, axis=-1)


def rms_norm(x, g):
    xf = x.astype(jnp.float32)
    y = xf * lax.rsqrt(jnp.mean(xf * xf, axis=-1, keepdims=True) + NORM_EPS)
    return (y * g.astype(jnp.float32)).astype(x.dtype)


def modulate(h, shift, scale):
    return h * (1 + scale) + shift


def _rope_1d(x, pos):
    half = x.shape[-1] // 2
    inv = ROPE_THETA ** (-jnp.arange(half, dtype=jnp.float32) / half)
    ang = pos.astype(jnp.float32)[:, None] * inv[None, :]
    cos = jnp.cos(ang)[None, :, None, :]
    sin = jnp.sin(ang)[None, :, None, :]
    xf = x.astype(jnp.float32)
    x1, x2 = xf[..., :half], xf[..., half:]
    return jnp.concatenate([x1 * cos - x2 * sin, x1 * sin + x2 * cos], -1).astype(x.dtype)


def rope_2d(x, rows, cols):
    h = x.shape[-1] // 2
    return jnp.concatenate([_rope_1d(x[..., :h], rows), _rope_1d(x[..., h:], cols)], -1)


def block_attention(q, k, v):
    b, tq, hk, g, d = q.shape
    nb = tq // Q_BLOCK
    kf = k.astype(jnp.float32)
    vf = v.astype(jnp.float32)
    qb = jnp.moveaxis(q.reshape(b, nb, Q_BLOCK, hk, g, d), 1, 0)
    scale = d ** -0.5

    def one_block(qi):
        s = jnp.einsum('bqkgd,bskd->bkgqs', qi.astype(jnp.float32), kf) * scale
        p = jax.nn.softmax(s, axis=-1)
        return jnp.einsum('bkgqs,bskd->bqkgd', p, vf)

    o = lax.map(one_block, qb)
    return jnp.moveaxis(o, 0, 1).reshape(b, tq, hk * g * v.shape[-1]).astype(q.dtype)


def _gqa_heads(p, q_gain, k_gain, rows, cols):
    b, t, _ = p.shape
    q, k, v = _split(p, (A_Q_W, A_KV_W, A_KV_W))
    q = rms_norm(q.reshape(b, t, A_HEADS, HEAD_DIM), q_gain)
    k = rms_norm(k.reshape(b, t, A_KV_HEADS, HEAD_DIM), k_gain)
    v = v.reshape(b, t, A_KV_HEADS, HEAD_DIM)
    if rows is not None:
        q = rope_2d(q, rows, cols)
        k = rope_2d(k, rows, cols)
    return q.reshape(b, t, A_KV_HEADS, A_HEADS // A_KV_HEADS, HEAD_DIM), k, v


def gqa_mixer(p_lat, p_ctx, q_gain, k_gain, rows, cols, with_ctx):
    ql, kl, vl = _gqa_heads(p_lat, q_gain, k_gain, rows, cols)
    qc, kc, vc = _gqa_heads(p_ctx, q_gain, k_gain, None, None)
    o_lat = block_attention(ql, jnp.concatenate([kc, kl], 1), jnp.concatenate([vc, vl], 1))
    o_ctx = block_attention(qc, kc, vc) if with_ctx else None
    return o_lat, o_ctx


def _mla_heads(p, q_gain, kv_gain, w_q_up, w_kv_up, rows, cols):
    b, t, _ = p.shape
    cq, ckv, kr = _split(p, (B_Q_LORA, B_KV_LORA, B_ROPE))
    q = (rms_norm(cq, q_gain) @ w_q_up).reshape(b, t, B_HEADS, B_NOPE + B_ROPE)
    kv = (rms_norm(ckv, kv_gain) @ w_kv_up).reshape(b, t, B_HEADS, B_NOPE + B_V)
    q_nope, q_rope = q[..., :B_NOPE], q[..., B_NOPE:]
    k_nope, v = kv[..., :B_NOPE], kv[..., B_NOPE:]
    k_rope = kr.reshape(b, t, 1, B_ROPE)
    if rows is not None:
        q_rope = rope_2d(q_rope, rows, cols)
        k_rope = rope_2d(k_rope, rows, cols)
    k = jnp.concatenate([k_nope, jnp.broadcast_to(k_rope, (b, t, B_HEADS, B_ROPE))], -1)
    q = jnp.concatenate([q_nope, q_rope], -1)
    return q[:, :, :, None, :], k, v


def mla_mixer(p_lat, p_ctx, q_gain, kv_gain, w_q_up, w_kv_up, rows, cols, with_ctx):
    ql, kl, vl = _mla_heads(p_lat, q_gain, kv_gain, w_q_up, w_kv_up, rows, cols)
    qc, kc, vc = _mla_heads(p_ctx, q_gain, kv_gain, w_q_up, w_kv_up, None, None)
    o_lat = block_attention(ql, jnp.concatenate([kc, kl], 1), jnp.concatenate([vc, vl], 1))
    o_ctx = block_attention(qc, kc, vc) if with_ctx else None
    return o_lat, o_ctx


def _rwkv_prepare(z, mu, w0, w_decay, a0, w_aaa, w_gate, k_k, k_a):
    f = lambda t: t.astype(jnp.float32)
    b, t, _ = z.shape
    zf = f(z)
    zp = jnp.pad(zf, ((0, 0), (1, 1), (0, 0)))
    zf = zf + (0.5 * (zp[:, :-2] + zp[:, 2:]) - zf) * f(mu)
    r, k, v, w_lo_f, w_lo_b, a_lo_f, a_lo_b, g_lo = _split(
        zf, (C_W, C_W, C_W, C_DECAY_LORA, C_DECAY_LORA, C_AAA_LORA, C_AAA_LORA, C_GATE_LORA))
    heads = lambda u: u.reshape(b, t, C_HEADS, HEAD_DIM)
    kk = heads(k * f(k_k))
    kk = kk / jnp.maximum(jnp.sqrt(jnp.sum(kk * kk, -1, keepdims=True)), 1e-12)
    decays, aas, ks = [], [], []
    for d, (w_lo, a_lo) in enumerate(((w_lo_f, a_lo_f), (w_lo_b, a_lo_b))):
        logw = -jax.nn.softplus(-(f(w0[d]) + jnp.tanh(w_lo) @ f(w_decay[d]))) - 0.5
        decays.append(heads(jnp.exp(-jnp.exp(logw))))
        a = jax.nn.sigmoid(f(a0[d]) + a_lo @ f(w_aaa[d]))
        aas.append(heads(a))
        ks.append(heads(k * (1 + (a - 1) * f(k_a))))
    g = jax.nn.sigmoid(g_lo) @ f(w_gate)
    return heads(r), heads(v), kk, ks, decays, aas, g


def _wkv_scan(r, w, k, v, kk, a, s0, reverse):
    def step(S, inp):
        r_t, w_t, k_t, v_t, kk_t, a_t = inp
        sa = jnp.einsum('bhvk,bhk->bhv', S, kk_t)
        S = (S * w_t[:, :, None, :] - sa[..., None] * (kk_t * a_t)[:, :, None, :]
             + v_t[..., None] * k_t[:, :, None, :])
        return S, jnp.einsum('bhvk,bhk->bhv', S, r_t)
    xs = tuple(jnp.moveaxis(u, 1, 0) for u in (r, w, k, v, kk, a))
    S, ys = lax.scan(step, s0, xs, reverse=reverse)
    return jnp.moveaxis(ys, 0, 1), S


def _rwkv_out(y, r, v, ks, g, r_k, gn_w, gn_b, dtype):
    b, t = y.shape[:2]
    mean = jnp.mean(y, -1, keepdims=True)
    var = jnp.mean(jnp.square(y - mean), -1, keepdims=True)
    yn = ((y - mean) * lax.rsqrt(var + C_GN_EPS)).reshape(b, t, C_W)
    yn = yn * gn_w.astype(jnp.float32) + gn_b.astype(jnp.float32)
    bonus = jnp.sum(r * (ks[0] + ks[1]) * r_k.astype(jnp.float32), -1, keepdims=True) * v
    return ((yn + bonus.reshape(b, t, C_W)) * g).astype(dtype)


def rwkv_mixer(z_lat, z_ctx, mu, w0, w_decay, a0, w_aaa, w_gate, k_k, k_a, r_k, gn_w, gn_b, with_ctx):
    rl, vl, kkl, ksl, wl, al, gl = _rwkv_prepare(z_lat, mu, w0, w_decay, a0, w_aaa, w_gate, k_k, k_a)
    rc, vc, kkc, ksc, wc, ac, gc = _rwkv_prepare(z_ctx, mu, w0, w_decay, a0, w_aaa, w_gate, k_k, k_a)
    s0 = jnp.zeros((z_lat.shape[0], C_HEADS, HEAD_DIM, HEAD_DIM), jnp.float32)
    y_lat, y_ctx = [], []
    for d in range(2):
        rev = d == 1
        yc, s_ctx = _wkv_scan(rc, wc[d], ksc[d], vc, kkc, ac[d], s0, rev)
        yl, _ = _wkv_scan(rl, wl[d], ksl[d], vl, kkl, al[d], s_ctx, rev)
        y_lat.append(yl)
        y_ctx.append(yc)
    o_lat = _rwkv_out(y_lat[0] + y_lat[1], rl, vl, ksl, gl, r_k, gn_w, gn_b, z_lat.dtype)
    o_ctx = _rwkv_out(y_ctx[0] + y_ctx[1], rc, vc, ksc, gc, r_k, gn_w, gn_b, z_ctx.dtype) if with_ctx else None
    return o_lat, o_ctx


def _heads(u, h):
    b, n, _ = u.shape
    return u.reshape(b, n, h, -1)


def nat_mixer(p_lat, p_ctx, rel_bias, with_ctx):
    ql, kl, vl = [_heads(u, D_HEADS) for u in _split(p_lat, (D_W, D_W, D_W))]
    qc, kc, vc = [_heads(u, D_HEADS) for u in _split(p_ctx, (D_W, D_W, D_W))]
    b, s, h, d = ql.shape
    n_rows = s // GRID_W
    kr = min(NA_ROWS, n_rows)
    n_nb = kr * NA_COLS
    scale = d ** -0.5
    kcf, vcf = kc.astype(jnp.float32), vc.astype(jnp.float32)
    qg = jnp.moveaxis(ql.reshape(b, n_rows, GRID_W, h, d), 1, 0)
    kg = kl.astype(jnp.float32).reshape(b, n_rows, GRID_W, h, d)
    vg = vl.astype(jnp.float32).reshape(b, n_rows, GRID_W, h, d)
    col = jnp.arange(GRID_W)
    col_idx = jnp.clip(col - NA_COLS // 2, 0, GRID_W - NA_COLS)[:, None] + jnp.arange(NA_COLS)[None, :]
    dcol = col_idx - col[:, None] + (NA_COLS - 1)

    def one_row(args):
        r, q_r = args
        q_r = q_r.astype(jnp.float32)
        r0 = jnp.clip(r - kr // 2, 0, n_rows - kr)
        k_nb = lax.dynamic_slice_in_dim(kg, r0, kr, axis=1)[:, :, col_idx]
        v_nb = lax.dynamic_slice_in_dim(vg, r0, kr, axis=1)[:, :, col_idx]
        drow = r0 + jnp.arange(kr) - r + (NA_ROWS - 1)
        bias = rel_bias[:, drow[None, :, None], dcol[:, None, :]].astype(jnp.float32)
        s_nb = (jnp.einsum('bqhd,brqjhd->bhqrj', q_r, k_nb) * scale + bias[None]).reshape(b, h, GRID_W, n_nb)
        s_cx = jnp.einsum('bqhd,bchd->bhqc', q_r, kcf) * scale
        pr = jax.nn.softmax(jnp.concatenate([s_nb, s_cx], -1), axis=-1)
        p_nb = pr[..., :n_nb].reshape(b, h, GRID_W, kr, NA_COLS)
        return (jnp.einsum('bhqrj,brqjhd->bqhd', p_nb, v_nb)
                + jnp.einsum('bhqc,bchd->bqhd', pr[..., n_nb:], vcf))

    o = lax.map(one_row, (jnp.arange(n_rows), qg))
    o_lat = jnp.moveaxis(o, 0, 1).reshape(b, s, h * d).astype(p_lat.dtype)
    o_ctx = block_attention(qc[:, :, :, None, :], kc, vc) if with_ctx else None
    return o_lat, o_ctx


def _merge(outs, gate_logits, w_branch, w_out):
    b, t, _ = gate_logits.shape
    gates = jax.nn.sigmoid(gate_logits).reshape(b, t, N_BRANCH, D_MODEL)
    y = gates[:, :, 0] * (outs[0] @ w_branch[0])
    for i in range(1, N_BRANCH):
        y = y + gates[:, :, i] * (outs[i] @ w_branch[i])
    return y @ w_out


def _mlp(h, w1, w2):
    return jnp.square(jax.nn.relu(h @ w1)) @ w2


def _layer(x, xc, rows, cols, mod_lat, mod_ctx, lp, last):
    sh1, sc1, gt1, sh2, sc2, gt2 = jnp.split(mod_lat, 6, axis=-1)
    csh1, csc1, cgt1, csh2, csc2, cgt2 = jnp.split(mod_ctx, 6, axis=-1)
    with_ctx = not last
    p_l = modulate(rms_norm(x, lp['g_norm1']), sh1, sc1) @ lp['w_in']
    p_c = modulate(rms_norm(xc, lp['g_norm1']), csh1, csc1) @ lp['w_in']
    a_l, b_l, c_l, d_l, g_l = _split(p_l, (A_IN, B_IN, C_IN, D_IN, GATE_IN))
    a_c, b_c, c_c, d_c, g_c = _split(p_c, (A_IN, B_IN, C_IN, D_IN, GATE_IN))
    oa = gqa_mixer(a_l, a_c, lp['a_q_gain'], lp['a_k_gain'], rows, cols, with_ctx)
    ob = mla_mixer(b_l, b_c, lp['b_q_gain'], lp['b_kv_gain'], lp['b_w_q_up'], lp['b_w_kv_up'], rows, cols, with_ctx)
    oc = rwkv_mixer(c_l, c_c, lp['c_mu'], lp['c_w0'], lp['c_w_decay'], lp['c_a0'], lp['c_w_aaa'], lp['c_w_gate'],
                    lp['c_k_k'], lp['c_k_a'], lp['c_r_k'], lp['c_gn_w'], lp['c_gn_b'], with_ctx)
    od = nat_mixer(d_l, d_c, lp['d_rel_bias'], with_ctx)
    x = x + gt1 * _merge((oa[0], ob[0], oc[0], od[0]), g_l, lp['w_branch'], lp['w_out'])
    x = x + gt2 * _mlp(modulate(rms_norm(x, lp['g_norm2']), sh2, sc2), lp['w_mlp1'], lp['w_mlp2'])
    if last:
        return x, None
    xc = xc + cgt1 * _merge((oa[1], ob[1], oc[1], od[1]), g_c, lp['w_branch'], lp['w_out'])
    xc = xc + cgt2 * _mlp(modulate(rms_norm(xc, lp['g_norm2']), csh2, csc2), lp['w_mlp1'], lp['w_mlp2'])
    return x, xc


def setup_inputs(seed: int = 0) -> dict:
    key = jax.random.key(seed)
    keys = iter(jax.random.split(key, 48))
    L, D = DEPTH, D_MODEL

    def nrm(shape, std):
        return std * jax.random.normal(next(keys), shape, jnp.float32)

    def gain(shape):
        return 1.0 + 0.05 * jax.random.normal(next(keys), shape, jnp.float32)

    return {
        'x': nrm((BATCH, SEQ, D), 1.0),
        'c': nrm((BATCH, D), 1.0),
        'ctx': nrm((BATCH, CTX_LEN, D), 1.0),
        'c_ctx': nrm((D,), 1.0),
        'w_ada': nrm((L, D, 6 * D), 0.5 * D ** -0.5),
        'b_ada': nrm((L, 6 * D), 0.02),
        'g_norm1': gain((L, D)),
        'g_norm2': gain((L, D)),
        'w_in': nrm((L, D, N_IN), D ** -0.5),
        'a_q_gain': gain((L, HEAD_DIM)),
        'a_k_gain': gain((L, HEAD_DIM)),
        'b_q_gain': gain((L, B_Q_LORA)),
        'b_kv_gain': gain((L, B_KV_LORA)),
        'b_w_q_up': nrm((L, B_Q_LORA, B_HEADS * (B_NOPE + B_ROPE)), B_Q_LORA ** -0.5),
        'b_w_kv_up': nrm((L, B_KV_LORA, B_HEADS * (B_NOPE + B_V)), B_KV_LORA ** -0.5),
        'c_mu': jax.random.uniform(next(keys), (L, C_IN), jnp.float32),
        'c_w0': jax.random.uniform(next(keys), (L, 2, C_W), jnp.float32, minval=-6.0, maxval=-1.0),
        'c_w_decay': nrm((L, 2, C_DECAY_LORA, C_W), 0.1),
        'c_a0': nrm((L, 2, C_W), 0.5),
        'c_w_aaa': nrm((L, 2, C_AAA_LORA, C_W), 0.5 * C_AAA_LORA ** -0.5),
        'c_w_gate': nrm((L, C_GATE_LORA, C_W), C_GATE_LORA ** -0.5),
        'c_k_k': 0.85 + nrm((L, C_W), 0.05),
        'c_k_a': gain((L, C_W)),
        'c_r_k': nrm((L, C_HEADS, HEAD_DIM), 0.1),
        'c_gn_w': gain((L, C_W)),
        'c_gn_b': nrm((L, C_W), 0.02),
        'd_rel_bias': nrm((L, D_HEADS, 2 * NA_ROWS - 1, 2 * NA_COLS - 1), 0.2),
        'w_branch': nrm((L, N_BRANCH, BRANCH_W, D), BRANCH_W ** -0.5),
        'w_out': nrm((L, D, D), D ** -0.5),
        'w_mlp1': nrm((L, D, D_FF), D ** -0.5),
        'w_mlp2': nrm((L, D_FF, D), D_FF ** -0.5),
        'g_final': gain((D,)),
    }


def reference(x, c, ctx, c_ctx, w_ada, b_ada, g_norm1, g_norm2, w_in, a_q_gain, a_k_gain, b_q_gain, b_kv_gain,
              b_w_q_up, b_w_kv_up, c_mu, c_w0, c_w_decay, c_a0, c_w_aaa, c_w_gate, c_k_k, c_k_a, c_r_k,
              c_gn_w, c_gn_b, d_rel_bias, w_branch, w_out, w_mlp1, w_mlp2, g_final):
    s = x.shape[1]
    t = jnp.arange(s)
    rows = t // GRID_W
    cols = t % GRID_W
    xc = ctx
    for l in range(DEPTH):
        mod_lat = (jax.nn.silu(c) @ w_ada[l] + b_ada[l])[:, None, :]
        mod_ctx = (jax.nn.silu(c_ctx) @ w_ada[l] + b_ada[l])[None, None, :]
        lp = dict(g_norm1=g_norm1[l], g_norm2=g_norm2[l], w_in=w_in[l],
                  a_q_gain=a_q_gain[l], a_k_gain=a_k_gain[l],
                  b_q_gain=b_q_gain[l], b_kv_gain=b_kv_gain[l], b_w_q_up=b_w_q_up[l], b_w_kv_up=b_w_kv_up[l],
                  c_mu=c_mu[l], c_w0=c_w0[l], c_w_decay=c_w_decay[l], c_a0=c_a0[l], c_w_aaa=c_w_aaa[l],
                  c_w_gate=c_w_gate[l], c_k_k=c_k_k[l], c_k_a=c_k_a[l], c_r_k=c_r_k[l],
                  c_gn_w=c_gn_w[l], c_gn_b=c_gn_b[l], d_rel_bias=d_rel_bias[l],
                  w_branch=w_branch[l], w_out=w_out[l], w_mlp1=w_mlp1[l], w_mlp2=w_mlp2[l])
        x, xc = _layer(x, xc, rows, cols, mod_lat, mod_ctx, lp, l == DEPTH - 1)
    return rms_norm(x, g_final)
```

```python
import functools

import numpy as np
import jax
import jax.numpy as jnp
from jax import lax
from jax.experimental import pallas as pl
from jax.experimental.pallas import tpu as pltpu

F32 = jnp.float32
BF16 = jnp.bfloat16

D_MODEL = 1024
SEQ = 2048
DEPTH = 2
GRID_W = 64
N_ROWS = SEQ // GRID_W
CTX_LEN = 256
T_ALL = CTX_LEN + SEQ
HEAD_DIM = 64
ROPE_THETA = 10000.0
NORM_EPS = 1e-6
N_HEADS = 8
A_KV_HEADS = 2
B_Q_LORA = 384
B_KV_LORA = 256
B_NOPE = 64
B_ROPE = 32
C_W = 512
C_GN_EPS = 64e-5
NA_ROWS = 8
NA_COLS = 16
D_FF = 4 * D_MODEL
A_IN = 768
B_IN = 672
C_IN = 1920
D_IN = 1536
GATE_IN = 4096

V7X_LANES = 128
V7X_VMEM_LIMIT = 56 * 1024 * 1024

Q_TILE = 256
N_QT = T_ALL // Q_TILE
NAT_QROWS = Q_TILE // GRID_W
NAT_KROWS = 12
NAT_KWIN = NAT_KROWS * GRID_W
CHUNK = 64
N_CHUNK = T_ALL // CHUNK
N_CTX_CHUNK = CTX_LEN // CHUNK
HG = 4
HGW = HG * HEAD_DIM
NEG_BIG = -1e30


def _cparams(sem, vmem=V7X_VMEM_LIMIT):
    return pltpu.CompilerParams(dimension_semantics=sem, vmem_limit_bytes=vmem)


def _split_bf16(a):
    hi = a.astype(BF16)
    lo = (a - hi.astype(F32)).astype(BF16)
    return hi, lo


def _dot(a, b):
    return jnp.dot(a, b, preferred_element_type=F32)


def _dot_nt(a, b):
    return lax.dot_general(a, b, (((1,), (1,)), ((), ())), preferred_element_type=F32)


def _dot3(a, b):
    ah, al = _split_bf16(a)
    bh, bl = _split_bf16(b)
    return _dot(ah, bh) + _dot(ah, bl) + _dot(al, bh)


def _dot3_exact_rhs(a, b_bf16):
    a0 = a.astype(BF16)
    r1 = a - a0.astype(F32)
    a1 = r1.astype(BF16)
    a2 = (r1 - a1.astype(F32)).astype(BF16)
    return _dot(a0, b_bf16) + _dot(a1, b_bf16) + _dot(a2, b_bf16)


def _sigmoid(x):
    return 1.0 / (1.0 + jnp.exp(-x))


def _lane(shape):
    return lax.broadcasted_iota(jnp.int32, shape, len(shape) - 1)


def _rope(x, cos, sin, half):
    n = x.shape[-1]
    lo = (_lane(x.shape) % (2 * half)) < half
    partner = jnp.where(lo, pltpu.roll(x, n - half, 1), pltpu.roll(x, half, 1))
    return x * cos + partner * sin


def _mod_vec(mb_ref, mc_ref, k, is_ctx):
    lat = mb_ref[0, :, k * D_MODEL:(k + 1) * D_MODEL]
    ctx = mc_ref[0, :, k * D_MODEL:(k + 1) * D_MODEL]
    return jnp.where(is_ctx, ctx, lat)


def _is_ctx_rows(tm, tile_axis):
    row = pl.program_id(tile_axis) * tm + lax.broadcasted_iota(jnp.int32, (tm, 1), 0)
    return row < CTX_LEN


def _norm_mod(x, g, shift, scale):
    y = x * lax.rsqrt(jnp.mean(x * x, axis=-1, keepdims=True) + NORM_EPS) * g
    return y * (1.0 + scale) + shift


def _ada_kernel(c_ref, w_ref, b_ref, o_ref):
    c = c_ref[...]
    s = c * _sigmoid(c)
    o_ref[0] = _dot3(s, w_ref[0]) + b_ref[0]


def _ada(cc, w_ada, b_ada):
    n_l, _, n_out = w_ada.shape
    tn = 1536
    rows = cc.shape[0]
    return pl.pallas_call(
        _ada_kernel,
        grid=(n_l, n_out // tn),
        in_specs=[pl.BlockSpec((rows, D_MODEL), lambda l, j: (0, 0)),
                  pl.BlockSpec((1, D_MODEL, tn), lambda l, j: (l, 0, j)),
                  pl.BlockSpec((1, 1, tn), lambda l, j: (l, 0, j))],
        out_specs=pl.BlockSpec((1, rows, tn), lambda l, j: (l, 0, j)),
        out_shape=jax.ShapeDtypeStruct((n_l, rows, n_out), F32),
        compiler_params=_cparams(("arbitrary", "arbitrary")),
        name="ada",
    )(cc, w_ada, b_ada.reshape(n_l, 1, n_out))


def _mod_specs(n_b, grid_rank):
    if grid_rank == 2:
        return (pl.BlockSpec((1, 1, 6 * D_MODEL), lambda b, j: (b, 0, 0)),
                pl.BlockSpec((1, 1, 6 * D_MODEL), lambda b, j: (n_b, 0, 0)))
    return (pl.BlockSpec((1, 1, 6 * D_MODEL), lambda b, j, f: (b, 0, 0)),
            pl.BlockSpec((1, 1, 6 * D_MODEL), lambda b, j, f: (n_b, 0, 0)))


def _norm_kernel(x_ref, g_ref, mb_ref, mc_ref, h_ref, *, tm):
    is_ctx = _is_ctx_rows(tm, 1)
    h = _norm_mod(x_ref[0], g_ref[...], _mod_vec(mb_ref, mc_ref, 0, is_ctx), _mod_vec(mb_ref, mc_ref, 1, is_ctx))
    h_ref[0] = h.astype(h_ref.dtype)


def _norm1(x, g, modl):
    n_b = x.shape[0]
    tm = 768
    mb, mc = _mod_specs(n_b, 2)
    return pl.pallas_call(
        functools.partial(_norm_kernel, tm=tm),
        grid=(n_b, T_ALL // tm),
        in_specs=[pl.BlockSpec((1, tm, D_MODEL), lambda b, j: (b, j, 0)),
                  pl.BlockSpec((1, D_MODEL), lambda b, j: (0, 0)), mb, mc],
        out_specs=pl.BlockSpec((1, tm, D_MODEL), lambda b, j: (b, j, 0)),
        out_shape=jax.ShapeDtypeStruct(x.shape, BF16),
        compiler_params=_cparams(("parallel", "parallel")),
        name="norm1",
    )(x, g.reshape(1, D_MODEL), modl, modl)


def _mm_kernel(a_ref, w_ref, o_ref):
    o_ref[...] = _dot(a_ref[...], w_ref[...]).astype(o_ref.dtype)


def _mm(a, w, out_dtype, tm, tn, name):
    m, k = a.shape
    n = w.shape[1]
    return pl.pallas_call(
        _mm_kernel,
        grid=(m // tm, n // tn),
        in_specs=[pl.BlockSpec((tm, k), lambda i, j: (i, 0)),
                  pl.BlockSpec((k, tn), lambda i, j: (0, j))],
        out_specs=pl.BlockSpec((tm, tn), lambda i, j: (i, j)),
        out_shape=jax.ShapeDtypeStruct((m, n), out_dtype),
        compiler_params=_cparams(("parallel", "arbitrary")),
        name=name,
    )(a, w)


def _softmax_pv_pair(q_even, q_odd, k_of, v_of, extra=None):
    acc = None
    for hh, q in enumerate((q_even, q_odd)):
        s = _dot_nt(q, k_of(hh))
        m = jnp.max(s, axis=-1, keepdims=True)
        if extra is not None:
            bias, k2_of, v2_of = extra
            s = s + bias(hh)
            s2 = _dot_nt(q, k2_of(hh))
            m = jnp.maximum(jnp.max(s, axis=-1, keepdims=True), jnp.max(s2, axis=-1, keepdims=True))
            p2 = jnp.exp(s2 - m)
        p = jnp.exp(s - m)
        l = jnp.sum(p, axis=-1, keepdims=True)
        o = _dot(p.astype(BF16), v_of(hh))
        if extra is not None:
            l = l + jnp.sum(p2, axis=-1, keepdims=True)
            o = o + _dot(p2.astype(BF16), v2_of(hh))
        o = o / l
        acc = o if acc is None else acc + o
    return acc


def _pad_heads(blk):
    lo = _lane(blk.shape) < HEAD_DIM
    return jnp.where(lo, blk, 0.0), jnp.where(lo, pltpu.roll(blk, HEAD_DIM, 1), 0.0)


def _head_rms(x, gain):
    ms = jnp.sum(x * x, axis=-1, keepdims=True) * (1.0 / HEAD_DIM)
    return x * lax.rsqrt(ms + NORM_EPS) * gain


def _gqa_kernel(pq_ref, pall_ref, cq_ref, sq_ref, call_ref, sall_ref, qg_ref, kg_ref, o_ref, k_scr, v_scr):
    j = pl.program_id(1)

    @pl.when(j == 0)
    def _prep():
        kblk = pall_ref[0, :, 512:640].astype(F32)
        for g, kh in enumerate(_pad_heads(kblk)):
            kh = _rope(_head_rms(kh, kg_ref[...]), call_ref[...], sall_ref[...], 16)
            k_scr[:, 128 * g:128 * (g + 1)] = kh.astype(BF16)
        vblk = pall_ref[0, :, 640:768].astype(F32)
        vrot = pltpu.roll(vblk, HEAD_DIM, 1)
        lo = _lane(vblk.shape) < HEAD_DIM
        v_scr[:, 0:128] = jnp.where(lo, vblk, 0.0).astype(BF16)
        v_scr[:, 128:256] = jnp.where(lo, 0.0, vrot).astype(BF16)
        v_scr[:, 256:384] = jnp.where(lo, vrot, 0.0).astype(BF16)
        v_scr[:, 384:512] = jnp.where(lo, 0.0, vblk).astype(BF16)

    def attend(n_keys):
        for jp in range(N_HEADS // 2):
            blk = pq_ref[0, :, 128 * jp:128 * (jp + 1)].astype(F32)
            qs = [(_rope(_head_rms(qh, qg_ref[...]), cq_ref[...], sq_ref[...], 16) * (HEAD_DIM ** -0.5)).astype(BF16)
                  for qh in _pad_heads(blk)]
            g = (2 * jp) // (N_HEADS // A_KV_HEADS)
            o = _softmax_pv_pair(
                qs[0], qs[1],
                lambda hh: k_scr[0:n_keys, 128 * g:128 * (g + 1)],
                lambda hh: v_scr[0:n_keys, 128 * (2 * g + hh):128 * (2 * g + hh + 1)])
            o_ref[0, :, 128 * jp:128 * (jp + 1)] = o.astype(o_ref.dtype)

    @pl.when(j == 0)
    def _ctx():
        attend(CTX_LEN)

    @pl.when(j > 0)
    def _lat():
        attend(T_ALL)


def _gqa(p_a, cos, sin, q_gain, k_gain):
    n_b = p_a.shape[0]
    pad = lambda g: jnp.concatenate([g, jnp.zeros((HEAD_DIM,), F32)]).reshape(1, 128)
    tile = lambda b, j: (b, j, 0)
    whole = lambda b, j: (b, 0, 0)
    return pl.pallas_call(
        _gqa_kernel,
        grid=(n_b, N_QT),
        in_specs=[pl.BlockSpec((1, Q_TILE, A_IN), tile),
                  pl.BlockSpec((1, T_ALL, A_IN), whole),
                  pl.BlockSpec((Q_TILE, 128), lambda b, j: (j, 0)),
                  pl.BlockSpec((Q_TILE, 128), lambda b, j: (j, 0)),
                  pl.BlockSpec((T_ALL, 128), lambda b, j: (0, 0)),
                  pl.BlockSpec((T_ALL, 128), lambda b, j: (0, 0)),
                  pl.BlockSpec((1, 128), lambda b, j: (0, 0)),
                  pl.BlockSpec((1, 128), lambda b, j: (0, 0))],
        out_specs=pl.BlockSpec((1, Q_TILE, 512), tile),
        out_shape=jax.ShapeDtypeStruct((n_b, T_ALL, 512), BF16),
        scratch_shapes=[pltpu.VMEM((T_ALL, 256), BF16), pltpu.VMEM((T_ALL, 512), BF16)],
        compiler_params=_cparams(("parallel", "arbitrary")),
        name="gqa",
    )(p_a, p_a, cos, sin, cos, sin, pad(q_gain), pad(k_gain))


def _mla_kernel(pq_ref, pall_ref, cq_ref, sq_ref, call_ref, sall_ref, qg_ref, kvg_ref, wq_ref, wkv_ref,
                o_ref, k_scr, v_scr):
    j = pl.program_id(1)
    kw = N_HEADS * 128

    @pl.when(j == 0)
    def _prep():
        def body(i, carry):
            r0 = pl.multiple_of(i * Q_TILE, Q_TILE)
            rows = pl.ds(r0, Q_TILE)
            ckv = pall_ref[0, rows, B_Q_LORA:B_Q_LORA + B_KV_LORA].astype(F32)
            n = ckv * lax.rsqrt(jnp.mean(ckv * ckv, axis=-1, keepdims=True) + NORM_EPS) * kvg_ref[...]
            kv = _dot(n.astype(BF16), wkv_ref[...])
            kr = _rope(pall_ref[0, rows, 640:768].astype(F32), call_ref[rows, :], sall_ref[rows, :], 8)
            for h in range(N_HEADS):
                k_scr[rows, 128 * h:128 * (h + 1)] = (kv[:, 128 * h:128 * (h + 1)] + kr).astype(BF16)
            v_scr[rows, :] = kv[:, kw:].astype(BF16)
            return carry
        lax.fori_loop(0, N_QT, body, 0)

    def attend(n_keys):
        cq = pq_ref[0, :, 0:B_Q_LORA].astype(F32)
        n = cq * lax.rsqrt(jnp.mean(cq * cq, axis=-1, keepdims=True) + NORM_EPS) * qg_ref[...]
        q = _dot(n.astype(BF16), wq_ref[...])
        scale = (B_NOPE + B_ROPE) ** -0.5
        for jp in range(N_HEADS // 2):
            qs = [(_rope(q[:, 128 * h:128 * (h + 1)], cq_ref[...], sq_ref[...], 8) * scale).astype(BF16)
                  for h in (2 * jp, 2 * jp + 1)]
            o = _softmax_pv_pair(
                qs[0], qs[1],
                lambda hh: k_scr[0:n_keys, 128 * (2 * jp + hh):128 * (2 * jp + hh + 1)],
                lambda hh: v_scr[0:n_keys, 128 * (2 * jp + hh):128 * (2 * jp + hh + 1)])
            o_ref[0, :, 128 * jp:128 * (jp + 1)] = o.astype(o_ref.dtype)

    @pl.when(j == 0)
    def _ctx():
        attend(CTX_LEN)

    @pl.when(j > 0)
    def _lat():
        attend(T_ALL)


def _mla(p_b, cos, sin, q_gain, kv_gain, wq, wkv):
    n_b = p_b.shape[0]
    tile = lambda b, j: (b, j, 0)
    whole = lambda b, j: (b, 0, 0)
    const = lambda b, j: (0, 0)
    w = p_b.shape[-1]
    return pl.pallas_call(
        _mla_kernel,
        grid=(n_b, N_QT),
        in_specs=[pl.BlockSpec((1, Q_TILE, w), tile),
                  pl.BlockSpec((1, T_ALL, w), whole),
                  pl.BlockSpec((Q_TILE, 128), lambda b, j: (j, 0)),
                  pl.BlockSpec((Q_TILE, 128), lambda b, j: (j, 0)),
                  pl.BlockSpec((T_ALL, 128), const),
                  pl.BlockSpec((T_ALL, 128), const),
                  pl.BlockSpec((1, B_Q_LORA), const),
                  pl.BlockSpec((1, B_KV_LORA), const),
                  pl.BlockSpec(wq.shape, const),
                  pl.BlockSpec(wkv.shape, const)],
        out_specs=pl.BlockSpec((1, Q_TILE, 512), tile),
        out_shape=jax.ShapeDtypeStruct((n_b, T_ALL, 512), BF16),
        scratch_shapes=[pltpu.VMEM((T_ALL, N_HEADS * 128), BF16), pltpu.VMEM((T_ALL, N_HEADS * 128), BF16)],
        compiler_params=_cparams(("parallel", "arbitrary")),
        name="mla",
    )(p_b, p_b, cos, sin, cos, sin, q_gain.reshape(1, -1), kv_gain.reshape(1, -1), wq, wkv)


def _nat_kernel(pq_ref, pall_ref, bias_ref, o_ref, k_scr, v_scr):
    j = pl.program_id(1)

    @pl.when(j == 0)
    def _prep():
        for jb in range(N_HEADS // 2):
            kblk = pall_ref[0, :, 512 + 128 * jb:512 + 128 * (jb + 1)].astype(F32)
            for hh, kh in enumerate(_pad_heads(kblk)):
                h = 2 * jb + hh
                k_scr[:, 128 * h:128 * (h + 1)] = kh.astype(BF16)
            vblk = pall_ref[0, :, 1024 + 128 * jb:1024 + 128 * (jb + 1)]
            lo = _lane(vblk.shape) < HEAD_DIM
            zero = jnp.zeros_like(vblk)
            v_scr[:, 128 * (2 * jb):128 * (2 * jb + 1)] = jnp.where(lo, vblk, zero)
            v_scr[:, 128 * (2 * jb + 1):128 * (2 * jb + 2)] = jnp.where(lo, zero, vblk)

    def q_pair(jp):
        blk = pq_ref[0, :, 128 * jp:128 * (jp + 1)].astype(F32)
        return [(qh * (HEAD_DIM ** -0.5)).astype(BF16) for qh in _pad_heads(blk)]

    hs = lambda jp, hh: slice(128 * (2 * jp + hh), 128 * (2 * jp + hh + 1))

    @pl.when(j == 0)
    def _ctx():
        for jp in range(N_HEADS // 2):
            qs = q_pair(jp)
            o = _softmax_pv_pair(qs[0], qs[1],
                                 lambda hh: k_scr[0:CTX_LEN, hs(jp, hh)],
                                 lambda hh: v_scr[0:CTX_LEN, hs(jp, hh)])
            o_ref[0, :, 128 * jp:128 * (jp + 1)] = o.astype(o_ref.dtype)

    @pl.when(j > 0)
    def _lat():
        first_row = jnp.clip(NAT_QROWS * (j - 1) - NA_ROWS // 2, 0, N_ROWS - NAT_KROWS)
        start = pl.multiple_of(CTX_LEN + first_row * GRID_W, Q_TILE)
        win = pl.ds(start, NAT_KWIN)
        for jp in range(N_HEADS // 2):
            qs = q_pair(jp)
            o = _softmax_pv_pair(
                qs[0], qs[1],
                lambda hh: k_scr[win, hs(jp, hh)],
                lambda hh: v_scr[win, hs(jp, hh)],
                extra=(lambda hh: bias_ref[0, 2 * jp + hh],
                       lambda hh: k_scr[0:CTX_LEN, hs(jp, hh)],
                       lambda hh: v_scr[0:CTX_LEN, hs(jp, hh)]))
            o_ref[0, :, 128 * jp:128 * (jp + 1)] = o.astype(o_ref.dtype)


def _nat_bias_table(rel_bias):
    tabs = []
    for qb in (0, 1, N_ROWS // NAT_QROWS - 1):
        ql = np.arange(Q_TILE)
        r = NAT_QROWS * qb + ql // GRID_W
        c = ql % GRID_W
        first_row = int(np.clip(NAT_QROWS * qb - NA_ROWS // 2, 0, N_ROWS - NAT_KROWS))
        kl = np.arange(NAT_KWIN)
        kr = first_row + kl // GRID_W
        kc = kl % GRID_W
        r0 = np.clip(r - NA_ROWS // 2, 0, N_ROWS - NA_ROWS)
        c0 = np.clip(c - NA_COLS // 2, 0, GRID_W - NA_COLS)
        valid = ((kr[None, :] >= r0[:, None]) & (kr[None, :] < r0[:, None] + NA_ROWS)
                 & (kc[None, :] >= c0[:, None]) & (kc[None, :] < c0[:, None] + NA_COLS))
        drow = np.clip(kr[None, :] - r[:, None] + NA_ROWS - 1, 0, 2 * NA_ROWS - 2)
        dcol = np.clip(kc[None, :] - c[:, None] + NA_COLS - 1, 0, 2 * NA_COLS - 2)
        tab = rel_bias[:, drow, dcol].astype(F32)
        tabs.append(jnp.where(valid[None], tab, NEG_BIG))
    return jnp.stack(tabs)


def _nat(p_d, bias_tab):
    n_b = p_d.shape[0]
    tile = lambda b, j: (b, j, 0)
    n_lat_tiles = N_QT - 1

    def bias_idx(b, j):
        qb = j - 1
        return (jnp.where(qb <= 0, 0, jnp.where(qb == n_lat_tiles - 1, 2, 1)), 0, 0, 0)

    return pl.pallas_call(
        _nat_kernel,
        grid=(n_b, N_QT),
        in_specs=[pl.BlockSpec((1, Q_TILE, D_IN), tile),
                  pl.BlockSpec((1, T_ALL, D_IN), lambda b, j: (b, 0, 0)),
                  pl.BlockSpec((1, N_HEADS, Q_TILE, NAT_KWIN), bias_idx)],
        out_specs=pl.BlockSpec((1, Q_TILE, 512), tile),
        out_shape=jax.ShapeDtypeStruct((n_b, T_ALL, 512), BF16),
        scratch_shapes=[pltpu.VMEM((T_ALL, N_HEADS * 128), BF16), pltpu.VMEM((T_ALL, N_HEADS * 128), BF16)],
        compiler_params=_cparams(("parallel", "arbitrary")),
        name="nat",
    )(p_d, p_d, bias_tab)


def _seg_sum(x, ones_bd):
    return _dot3_exact_rhs(x, ones_bd)


def _rwkv_prep_kernel(z_ref, zp_ref, zn_ref, mu_ref, w0_ref, a0_ref, kk_ref, ka_ref, rk_ref,
                      wdh_ref, wdl_ref, wah_ref, wal_ref, wgh_ref, wgl_ref, ones_ref,
                      r_out, v_out, kk_out, kd_out, lw_out, bd_out, bonus_out, g_out):
    j = pl.program_id(1)
    z = z_ref[0]
    tm = z.shape[0]
    row = lax.broadcasted_iota(jnp.int32, (tm, 1), 0)
    prev_row = jnp.where(j <= 1, 0.0, zp_ref[0, 7:8, :])
    next_row = jnp.where((j == 0) | (j == N_QT - 1), 0.0, zn_ref[0, 0:1, :])
    z_prev = jnp.where(row == 0, prev_row, pltpu.roll(z, 1, 0))
    z_next = jnp.where(row == tm - 1, next_row, pltpu.roll(z, tm - 1, 0))
    zs = z + (0.5 * (z_prev + z_next) - z) * mu_ref[...]

    r = zs[:, 0:512]
    k = zs[:, 512:1024]
    v = zs[:, 1024:1536]
    w_lo = zs[:, 1536:1664]
    a_lo = zs[:, 1664:1792]
    g_lo = zs[:, 1792:1920]
    ones_bd = ones_ref[...]

    def lora(x, wh_ref, wl_ref):
        xh, xl = _split_bf16(x)
        return _dot(xh, wh_ref[...]) + _dot(xh, wl_ref[...]) + _dot(xl, wh_ref[...])

    kkr = k * kk_ref[...]
    nrm = jnp.maximum(jnp.sqrt(_seg_sum(kkr * kkr, ones_bd)), 1e-12)
    kk = kkr / nrm
    dec = lora(jnp.tanh(w_lo), wdh_ref, wdl_ref)
    aaa = lora(a_lo, wah_ref, wal_ref)
    g = lora(_sigmoid(g_lo), wgh_ref, wgl_ref)
    r_out[0] = r
    v_out[0] = v
    kk_out[0] = kk
    g_out[0] = g
    ksum = None
    for d in range(2):
        u = -(w0_ref[d:d + 1, :] + dec[:, 512 * d:512 * (d + 1)])
        softplus = jnp.maximum(u, 0.0) + jnp.log(1.0 + jnp.exp(-jnp.abs(u)))
        logw = -softplus - 0.5
        lw_out[d, 0] = -jnp.exp(logw)
        a = _sigmoid(a0_ref[d:d + 1, :] + aaa[:, 512 * d:512 * (d + 1)])
        bd_out[d, 0] = kk * a
        kd = k * (1.0 + (a - 1.0) * ka_ref[...])
        kd_out[d, 0] = kd
        ksum = kd if ksum is None else ksum + kd
    bonus_out[0] = _seg_sum(r * ksum * rk_ref[...], ones_bd) * v


def _rwkv_prep(p_c, lp):
    n_b = p_c.shape[0]
    tile = lambda b, j: (b, j, 0)
    const = lambda b, j: (0, 0)
    blocks8 = T_ALL // 8
    tpb = Q_TILE // 8
    prev = lambda b, j: (b, jnp.maximum(j * tpb - 1, 0), 0)
    nxt = lambda b, j: (b, jnp.minimum((j + 1) * tpb, blocks8 - 1), 0)
    o3 = jax.ShapeDtypeStruct((n_b, T_ALL, C_W), F32)
    o4 = jax.ShapeDtypeStruct((2, n_b, T_ALL, C_W), F32)
    s3 = pl.BlockSpec((1, Q_TILE, C_W), tile)
    s4 = pl.BlockSpec((2, 1, Q_TILE, C_W), lambda b, j: (0, b, j, 0))
    small = [lp['c_mu'].reshape(1, C_IN), lp['c_w0'], lp['c_a0'], lp['c_k_k'].reshape(1, C_W),
             lp['c_k_a'].reshape(1, C_W), lp['c_r_k'].reshape(1, C_W)]
    bd2 = lambda w: jnp.concatenate(
        [jnp.concatenate([w[0], jnp.zeros_like(w[0])], 1), jnp.concatenate([jnp.zeros_like(w[1]), w[1]], 1)], 0)
    mats = []
    for w in (bd2(lp['c_w_decay']), bd2(lp['c_w_aaa']), lp['c_w_gate']):
        mats.extend(_split_bf16(w))
    ones_bd = jnp.asarray(np.kron(np.eye(N_HEADS), np.ones((HEAD_DIM, HEAD_DIM))), BF16)
    ins = small + mats + [ones_bd]
    return pl.pallas_call(
        _rwkv_prep_kernel,
        grid=(n_b, N_QT),
        in_specs=[pl.BlockSpec((1, Q_TILE, C_IN), tile),
                  pl.BlockSpec((1, 8, C_IN), prev),
                  pl.BlockSpec((1, 8, C_IN), nxt)] + [pl.BlockSpec(a.shape, const) for a in ins],
        out_specs=[s3, s3, s3, s4, s4, s4, s3, s3],
        out_shape=[o3, o3, o3, o4, o4, o4, o3, o3],
        compiler_params=_cparams(("parallel", "parallel")),
        name="rwkv_prep",
    )(p_c, p_c, p_c, *ins)


def _rwkv_scan_kernel(r_ref, v_ref, kk_ref, kd_ref, lw_ref, bd_ref, y_ref, s_scr):
    d = pl.program_id(2)
    fwd = d == 0
    n = HG * CHUNK
    ri = lax.broadcasted_iota(jnp.int32, (n, n), 0)
    ci = lax.broadcasted_iota(jnp.int32, (n, n), 1)
    same = (ri // CHUNK) == (ci // CHUNK)
    flip = lambda t: jnp.where(fwd, t, CHUNK - 1 - t)
    tr, tc = flip(ri % CHUNK), flip(ci % CHUNK)
    strict = same & (tr > tc)
    incl = same & (tr >= tc)
    eye = (ri == ci).astype(F32)
    lower_left = [same & ((tr // (2 * h)) == (tc // (2 * h))) & (((tr // h) % 2) == 1) & (((tc // h) % 2) == 0)
                  for h in (1, 2, 4, 8, 16, 32)]
    t_r = flip(lax.broadcasted_iota(jnp.int32, (CHUNK, CHUNK), 0))
    t_c = flip(lax.broadcasted_iota(jnp.int32, (CHUNK, CHUNK), 1))
    cum_mat = jnp.where(t_r >= t_c, 1.0, 0.0).astype(BF16)
    head_of_lane = lax.broadcasted_iota(jnp.int32, (CHUNK, HGW), 1) // HEAD_DIM

    def stack(x):
        return jnp.concatenate([jnp.where(head_of_lane == p, x, 0.0) for p in range(HG)], axis=0)

    s_scr[...] = jnp.zeros_like(s_scr)

    def body(i, carry):
        c = jnp.where(fwd, i, jnp.where(i < N_CTX_CHUNK, N_CTX_CHUNK - 1 - i, N_CHUNK + N_CTX_CHUNK - 1 - i))
        rows = pl.ds(pl.multiple_of(c * CHUNK, CHUNK), CHUNK)
        lw = lw_ref[0, 0, rows, :]
        cum = _dot3_exact_rhs_lhs(cum_mat, lw)
        total = jnp.sum(lw, axis=0, keepdims=True)
        e_pos = jnp.exp(cum)
        e_neg = jnp.exp(-cum)
        e_prev = jnp.exp(cum - lw)
        e_rest = jnp.exp(total - cum)
        v = v_ref[0, rows, :]
        bd = bd_ref[0, 0, rows, :]
        kd = kd_ref[0, 0, rows, :]
        al4 = stack(kk_ref[0, rows, :] * e_prev)
        be4 = stack(bd * e_neg)
        ka4 = stack(kd * e_neg)
        rh4 = stack(r_ref[0, rows, :] * e_pos)
        bee4 = stack(bd * e_rest)
        kae4 = stack(kd * e_rest)
        v4 = stack(v).astype(BF16)
        al4b, be4b, ka4b, rh4b = (x.astype(BF16) for x in (al4, be4, ka4, rh4))

        l_ab = jnp.where(strict, _dot_nt(al4b, be4b), 0.0)
        l_ak = jnp.where(strict, _dot_nt(al4b, ka4b), 0.0)
        m_rb = jnp.where(incl, _dot_nt(rh4b, be4b), 0.0).astype(BF16)
        m_rk = jnp.where(incl, _dot_nt(rh4b, ka4b), 0.0).astype(BF16)

        x = eye - jnp.where(lower_left[0], l_ab, 0.0)
        for half_mask in lower_left[1:]:
            xb = x.astype(BF16)
            x = x - _dot(xb, _dot(jnp.where(half_mask, l_ab, 0.0).astype(BF16), xb).astype(BF16))
        tb = x.astype(BF16)

        w4 = _dot(tb, al4b)
        u04 = _dot(tb, _dot(l_ak.astype(BF16), v4).astype(BF16))
        w4b, u04b = w4.astype(BF16), u04.astype(BF16)
        q4 = rh4 - _dot(m_rb, w4b)
        y04 = _dot(m_rk, v4) - _dot(m_rb, u04b)
        g4 = eye * jnp.exp(total) - _dot(bee4.T.astype(BF16), w4b)
        h4 = _dot(kae4.T.astype(BF16), v4) - _dot(bee4.T.astype(BF16), u04b)

        s = s_scr[...]
        y4 = _dot(q4.astype(BF16), s.astype(BF16)) + y04
        y = y4[0:CHUNK]
        for p in range(1, HG):
            y = y + y4[p * CHUNK:(p + 1) * CHUNK]
        y_ref[0, 0, rows, :] = y
        s_scr[...] = _dot3(g4, s) + h4
        return carry

    lax.fori_loop(0, N_CHUNK, body, 0)


def _dot3_exact_rhs_lhs(m_bf16, a):
    a0 = a.astype(BF16)
    r1 = a - a0.astype(F32)
    a1 = r1.astype(BF16)
    a2 = (r1 - a1.astype(F32)).astype(BF16)
    return _dot(m_bf16, a0) + _dot(m_bf16, a1) + _dot(m_bf16, a2)


def _rwkv_scan(r, v, kk, kd, lw, bd):
    n_b = r.shape[0]
    n_g = C_W // HGW
    s3 = pl.BlockSpec((1, T_ALL, HGW), lambda b, g, d: (b, 0, g))
    s4 = pl.BlockSpec((1, 1, T_ALL, HGW), lambda b, g, d: (d, b, 0, g))
    return pl.pallas_call(
        _rwkv_scan_kernel,
        grid=(n_b, n_g, 2),
        in_specs=[s3, s3, s3, s4, s4, s4],
        out_specs=s4,
        out_shape=jax.ShapeDtypeStruct((2, n_b, T_ALL, C_W), F32),
        scratch_shapes=[pltpu.VMEM((HGW, HGW), F32)],
        compiler_params=_cparams(("parallel", "parallel", "arbitrary")),
        name="rwkv_scan",
    )(r, v, kk, kd, lw, bd)


def _rwkv_out_kernel(y_ref, bonus_ref, g_ref, gw_ref, gb_ref, ones_ref, o_ref):
    y = y_ref[0, 0] + y_ref[1, 0]
    ones_bd = ones_ref[...]
    mean = _seg_sum(y, ones_bd) * (1.0 / HEAD_DIM)
    yc = y - mean
    var = _seg_sum(yc * yc, ones_bd) * (1.0 / HEAD_DIM)
    yn = yc * lax.rsqrt(var + C_GN_EPS) * gw_ref[...] + gb_ref[...]
    o_ref[0] = ((yn + bonus_ref[0]) * g_ref[0]).astype(o_ref.dtype)


def _rwkv_out(y, bonus, g, gn_w, gn_b):
    n_b = bonus.shape[0]
    tm = 768
    tile = lambda b, j: (b, j, 0)
    const = lambda b, j: (0, 0)
    ones_bd = jnp.asarray(np.kron(np.eye(N_HEADS), np.ones((HEAD_DIM, HEAD_DIM))), BF16)
    return pl.pallas_call(
        _rwkv_out_kernel,
        grid=(n_b, T_ALL // tm),
        in_specs=[pl.BlockSpec((2, 1, tm, C_W), lambda b, j: (0, b, j, 0)),
                  pl.BlockSpec((1, tm, C_W), tile), pl.BlockSpec((1, tm, C_W), tile),
                  pl.BlockSpec((1, C_W), const), pl.BlockSpec((1, C_W), const),
                  pl.BlockSpec((C_W, C_W), const)],
        out_specs=pl.BlockSpec((1, tm, C_W), tile),
        out_shape=jax.ShapeDtypeStruct((n_b, T_ALL, C_W), BF16),
        compiler_params=_cparams(("parallel", "parallel")),
        name="rwkv_out",
    )(y, bonus, g, gn_w.reshape(1, C_W), gn_b.reshape(1, C_W), ones_bd)


def _merge_kernel(x_ref, oa_ref, ob_ref, oc_ref, od_ref, gate_ref, wb_ref, wo_ref, mb_ref, mc_ref, out_ref, *, tm):
    is_ctx = _is_ctx_rows(tm, 1)
    y = None
    for i, o_ref in enumerate((oa_ref, ob_ref, oc_ref, od_ref)):
        z = _dot(o_ref[0], wb_ref[i])
        sg = _sigmoid(gate_ref[0, :, i * D_MODEL:(i + 1) * D_MODEL].astype(F32))
        y = sg * z if y is None else y + sg * z
    z = _dot(y.astype(BF16), wo_ref[...])
    out_ref[0] = x_ref[0] + _mod_vec(mb_ref, mc_ref, 2, is_ctx) * z


def _merge(x, outs, gates, wb, wo, modl):
    n_b = x.shape[0]
    tm = 768
    tile = lambda b, j: (b, j, 0)
    mb, mc = _mod_specs(n_b, 2)
    o_spec = pl.BlockSpec((1, tm, 512), tile)
    return pl.pallas_call(
        functools.partial(_merge_kernel, tm=tm),
        grid=(n_b, T_ALL // tm),
        in_specs=[pl.BlockSpec((1, tm, D_MODEL), tile), o_spec, o_spec, o_spec, o_spec,
                  pl.BlockSpec((1, tm, GATE_IN), tile),
                  pl.BlockSpec(wb.shape, lambda b, j: (0, 0, 0)),
                  pl.BlockSpec(wo.shape, lambda b, j: (0, 0)), mb, mc],
        out_specs=pl.BlockSpec((1, tm, D_MODEL), tile),
        out_shape=jax.ShapeDtypeStruct(x.shape, F32),
        compiler_params=_cparams(("parallel", "parallel")),
        name="merge",
    )(x, *outs, gates, wb, wo, modl, modl)


def _mlp_kernel(x_ref, g_ref, mb_ref, mc_ref, w1_ref, w2_ref, out_ref, h_scr, acc_scr, *, tm, n_f):
    f = pl.program_id(2)
    is_ctx = _is_ctx_rows(tm, 1)

    @pl.when(f == 0)
    def _init():
        h = _norm_mod(x_ref[0], g_ref[...], _mod_vec(mb_ref, mc_ref, 3, is_ctx), _mod_vec(mb_ref, mc_ref, 4, is_ctx))
        h_scr[...] = h.astype(BF16)
        acc_scr[...] = jnp.zeros_like(acc_scr)

    a = jnp.square(jnp.maximum(_dot(h_scr[...], w1_ref[...]), 0.0))
    acc_scr[...] += _dot(a.astype(BF16), w2_ref[...])

    @pl.when(f == n_f - 1)
    def _fin():
        out_ref[0] = x_ref[0] + _mod_vec(mb_ref, mc_ref, 5, is_ctx) * acc_scr[...]


def _mlp(x, g, w1, w2, modl):
    n_b = x.shape[0]
    tm, tf = 768, 512
    n_f = D_FF // tf
    tile = lambda b, j, f: (b, j, 0)
    mb, mc = _mod_specs(n_b, 3)
    return pl.pallas_call(
        functools.partial(_mlp_kernel, tm=tm, n_f=n_f),
        grid=(n_b, T_ALL // tm, n_f),
        in_specs=[pl.BlockSpec((1, tm, D_MODEL), tile),
                  pl.BlockSpec((1, D_MODEL), lambda b, j, f: (0, 0)), mb, mc,
                  pl.BlockSpec((D_MODEL, tf), lambda b, j, f: (0, f)),
                  pl.BlockSpec((tf, D_MODEL), lambda b, j, f: (f, 0))],
        out_specs=pl.BlockSpec((1, tm, D_MODEL), tile),
        out_shape=jax.ShapeDtypeStruct(x.shape, F32),
        scratch_shapes=[pltpu.VMEM((tm, D_MODEL), BF16), pltpu.VMEM((tm, D_MODEL), F32)],
        compiler_params=_cparams(("parallel", "parallel", "arbitrary")),
        name="mlp",
    )(x, g.reshape(1, D_MODEL), modl, modl, w1, w2)


def _final_kernel(x_ref, g_ref, o_ref):
    x = x_ref[0]
    o_ref[0] = x * lax.rsqrt(jnp.mean(x * x, axis=-1, keepdims=True) + NORM_EPS) * g_ref[...]


def _final_norm(x, g):
    n_b = x.shape[0]
    tm = Q_TILE
    return pl.pallas_call(
        _final_kernel,
        grid=(n_b, SEQ // tm),
        in_specs=[pl.BlockSpec((1, tm, D_MODEL), lambda b, j: (b, j + CTX_LEN // tm, 0)),
                  pl.BlockSpec((1, D_MODEL), lambda b, j: (0, 0))],
        out_specs=pl.BlockSpec((1, tm, D_MODEL), lambda b, j: (b, j, 0)),
        out_shape=jax.ShapeDtypeStruct((n_b, SEQ, D_MODEL), F32),
        compiler_params=_cparams(("parallel", "parallel")),
        name="final_norm",
    )(x, g.reshape(1, D_MODEL))


def _rope_tables(half, lane0):
    t = np.arange(SEQ)
    inv = ROPE_THETA ** (-np.arange(half, dtype=np.float64) / half)
    cos = np.ones((T_ALL, V7X_LANES), np.float64)
    sin = np.zeros((T_ALL, V7X_LANES), np.float64)
    for part, pos in enumerate((t // GRID_W, t % GRID_W)):
        ang = pos[:, None].astype(np.float64) * inv[None, :]
        ang = ang.astype(np.float32).astype(np.float64)
        base = lane0 + 2 * half * part
        cos[CTX_LEN:, base:base + half] = np.cos(ang)
        cos[CTX_LEN:, base + half:base + 2 * half] = np.cos(ang)
        sin[CTX_LEN:, base:base + half] = -np.sin(ang)
        sin[CTX_LEN:, base + half:base + 2 * half] = np.sin(ang)
    return jnp.asarray(cos, F32), jnp.asarray(sin, F32)


def _layer_weights(l, w_in, b_w_q_up, b_w_kv_up, w_branch, w_out, w_mlp1, w_mlp2):
    wi = w_in[l]
    o_b = A_IN
    o_c = o_b + B_IN
    o_d = o_c + C_IN
    o_g = o_d + D_IN
    w_a = wi[:, :o_b]
    wb_raw = wi[:, o_b:o_c]
    z = lambda n: jnp.zeros((D_MODEL, n), F32)
    w_b = jnp.concatenate([wb_raw[:, :B_Q_LORA + B_KV_LORA], z(64), wb_raw[:, B_Q_LORA + B_KV_LORA:], z(32)], 1)
    wq = b_w_q_up[l].reshape(B_Q_LORA, N_HEADS, B_NOPE + B_ROPE)
    wq = jnp.concatenate([wq, jnp.zeros((B_Q_LORA, N_HEADS, 128 - B_NOPE - B_ROPE), F32)], -1)
    wkv = b_w_kv_up[l].reshape(B_KV_LORA, N_HEADS, 2 * HEAD_DIM)
    zk = jnp.zeros((B_KV_LORA, N_HEADS, HEAD_DIM), F32)
    wk = jnp.concatenate([wkv[:, :, :B_NOPE], zk], -1)
    even = (jnp.arange(N_HEADS) % 2 == 0)[None, :, None]
    wv = jnp.concatenate([jnp.where(even, wkv[:, :, B_NOPE:], 0.0), jnp.where(even, 0.0, wkv[:, :, B_NOPE:])], -1)
    return dict(
        w_a=w_a.astype(BF16), w_b=w_b.astype(BF16), w_c=wi[:, o_c:o_d].astype(BF16),
        w_d=wi[:, o_d:o_g].astype(BF16), w_g=wi[:, o_g:].astype(BF16),
        wq=wq.reshape(B_Q_LORA, N_HEADS * 128).astype(BF16),
        wkv=jnp.concatenate([wk.reshape(B_KV_LORA, -1), wv.reshape(B_KV_LORA, -1)], 1).astype(BF16),
        w_branch=w_branch[l].astype(BF16), w_out=w_out[l].astype(BF16),
        w_mlp1=w_mlp1[l].astype(BF16), w_mlp2=w_mlp2[l].astype(BF16))


def kernel(x, c, ctx, c_ctx, w_ada, b_ada, g_norm1, g_norm2, w_in, a_q_gain, a_k_gain, b_q_gain, b_kv_gain,
           b_w_q_up, b_w_kv_up, c_mu, c_w0, c_w_decay, c_a0, c_w_aaa, c_w_gate, c_k_k, c_k_a, c_r_k,
           c_gn_w, c_gn_b, d_rel_bias, w_branch, w_out, w_mlp1, w_mlp2, g_final):
    n_b = x.shape[0]
    assert x.shape[1:] == (SEQ, D_MODEL) and ctx.shape[1:] == (CTX_LEN, D_MODEL)
    m = n_b * T_ALL
    mod_rows = ((n_b + 1 + 7) // 8) * 8
    cc = jnp.concatenate([c, c_ctx[None, :], jnp.zeros((mod_rows - n_b - 1, D_MODEL), F32)], 0)
    mod_all = _ada(cc, w_ada, b_ada)
    cos_a, sin_a = _rope_tables(16, 0)
    cos_b, sin_b = _rope_tables(8, B_NOPE)
    xs = jnp.concatenate([ctx, x], axis=1)
    for l in range(DEPTH):
        lw = _layer_weights(l, w_in, b_w_q_up, b_w_kv_up, w_branch, w_out, w_mlp1, w_mlp2)
        lp = dict(c_mu=c_mu[l], c_w0=c_w0[l], c_w_decay=c_w_decay[l], c_a0=c_a0[l], c_w_aaa=c_w_aaa[l],
                  c_w_gate=c_w_gate[l], c_k_k=c_k_k[l], c_k_a=c_k_a[l], c_r_k=c_r_k[l])
        modl = mod_all[l].reshape(mod_rows, 1, 6 * D_MODEL)
        h = _norm1(xs, g_norm1[l], modl).reshape(m, D_MODEL)
        p_a = _mm(h, lw['w_a'], BF16, 768, A_IN, "w_in_a").reshape(n_b, T_ALL, A_IN)
        p_b = _mm(h, lw['w_b'], BF16, 768, 768, "w_in_b").reshape(n_b, T_ALL, 768)
        p_c = _mm(h, lw['w_c'], F32, 768, 640, "w_in_c").reshape(n_b, T_ALL, C_IN)
        p_d = _mm(h, lw['w_d'], BF16, 768, 768, "w_in_d").reshape(n_b, T_ALL, D_IN)
        p_g = _mm(h, lw['w_g'], BF16, 768, 1024, "w_in_g").reshape(n_b, T_ALL, GATE_IN)
        o_a = _gqa(p_a, cos_a, sin_a, a_q_gain[l], a_k_gain[l])
        o_b = _mla(p_b, cos_b, sin_b, b_q_gain[l], b_kv_gain[l], lw['wq'], lw['wkv'])
        r, v, kk, kd, lwd, bd, bonus, g = _rwkv_prep(p_c, lp)
        y = _rwkv_scan(r, v, kk, kd, lwd, bd)
        o_c = _rwkv_out(y, bonus, g, c_gn_w[l], c_gn_b[l])
        o_d = _nat(p_d, _nat_bias_table(d_rel_bias[l]))
        xs = _merge(xs, (o_a, o_b, o_c, o_d), p_g, lw['w_branch'], lw['w_out'], modl)
        xs = _mlp(xs, g_norm2[l], lw['w_mlp1'], lw['w_mlp2'], modl)
    return _final_norm(xs, g_final)
```

```python
import functools

import numpy as np
import jax
import jax.numpy as jnp
from jax import lax
from jax.experimental import pallas as pl
from jax.experimental.pallas import tpu as pltpu

F32 = jnp.float32
BF16 = jnp.bfloat16

D_MODEL = 1024
SEQ = 2048
DEPTH = 2
GRID_W = 64
N_ROWS = SEQ // GRID_W
CTX_LEN = 256
T_ALL = CTX_LEN + SEQ
HEAD_DIM = 64
ROPE_THETA = 10000.0
NORM_EPS = 1e-6
N_HEADS = 8
A_KV_HEADS = 2
B_Q_LORA = 384
B_KV_LORA = 256
B_NOPE = 64
B_ROPE = 32
C_W = 512
C_GN_EPS = 64e-5
NA_ROWS = 8
NA_COLS = 16
D_FF = 4 * D_MODEL
A_IN = 768
B_IN = 672
C_IN = 1920
D_IN = 1536
GATE_IN = 4096

V7X_LANES = 128
V7X_VMEM_LIMIT = 56 * 1024 * 1024

Q_TILE = 256
N_QT = T_ALL // Q_TILE
NAT_QROWS = Q_TILE // GRID_W
NAT_KROWS = 12
NAT_KWIN = NAT_KROWS * GRID_W
CHUNK = 64
N_CHUNK = T_ALL // CHUNK
N_CTX_CHUNK = CTX_LEN // CHUNK
HG = 4
HGW = HG * HEAD_DIM
NEG_BIG = -1e30


def _cparams(sem, vmem=V7X_VMEM_LIMIT):
    return pltpu.CompilerParams(dimension_semantics=sem, vmem_limit_bytes=vmem)


def _split_bf16(a):
    hi = a.astype(BF16)
    lo = (a - hi.astype(F32)).astype(BF16)
    return hi, lo


def _dot(a, b):
    return jnp.dot(a, b, preferred_element_type=F32)


def _dot_nt(a, b):
    return lax.dot_general(a, b, (((1,), (1,)), ((), ())), preferred_element_type=F32)


def _dot3(a, b):
    ah, al = _split_bf16(a)
    bh, bl = _split_bf16(b)
    return _dot(ah, bh) + _dot(ah, bl) + _dot(al, bh)


def _dot3_exact_rhs(a, b_bf16):
    a0 = a.astype(BF16)
    r1 = a - a0.astype(F32)
    a1 = r1.astype(BF16)
    a2 = (r1 - a1.astype(F32)).astype(BF16)
    return _dot(a0, b_bf16) + _dot(a1, b_bf16) + _dot(a2, b_bf16)


def _sigmoid(x):
    return 1.0 / (1.0 + jnp.exp(-x))


def _lane(shape):
    return lax.broadcasted_iota(jnp.int32, shape, len(shape) - 1)


def _rope(x, cos, sin, half):
    n = x.shape[-1]
    lo = (_lane(x.shape) % (2 * half)) < half
    partner = jnp.where(lo, pltpu.roll(x, n - half, 1), pltpu.roll(x, half, 1))
    return x * cos + partner * sin


def _mod_vec(mb_ref, mc_ref, k, is_ctx):
    lat = mb_ref[0, :, k * D_MODEL:(k + 1) * D_MODEL]
    ctx = mc_ref[0, :, k * D_MODEL:(k + 1) * D_MODEL]
    return jnp.where(is_ctx, ctx, lat)


def _is_ctx_rows(tm, tile_axis):
    row = pl.program_id(tile_axis) * tm + lax.broadcasted_iota(jnp.int32, (tm, 1), 0)
    return row < CTX_LEN


def _norm_mod(x, g, shift, scale):
    y = x * lax.rsqrt(jnp.mean(x * x, axis=-1, keepdims=True) + NORM_EPS) * g
    return y * (1.0 + scale) + shift


def _ada_kernel(c_ref, w_ref, b_ref, o_ref):
    c = c_ref[...]
    s = c * _sigmoid(c)
    o_ref[0] = _dot3(s, w_ref[0]) + b_ref[0]


def _ada(cc, w_ada, b_ada):
    n_l, _, n_out = w_ada.shape
    tn = 1536
    rows = cc.shape[0]
    return pl.pallas_call(
        _ada_kernel,
        grid=(n_l, n_out // tn),
        in_specs=[pl.BlockSpec((rows, D_MODEL), lambda l, j: (0, 0)),
                  pl.BlockSpec((1, D_MODEL, tn), lambda l, j: (l, 0, j)),
                  pl.BlockSpec((1, 1, tn), lambda l, j: (l, 0, j))],
        out_specs=pl.BlockSpec((1, rows, tn), lambda l, j: (l, 0, j)),
        out_shape=jax.ShapeDtypeStruct((n_l, rows, n_out), F32),
        compiler_params=_cparams(("arbitrary", "arbitrary")),
        name="ada",
    )(cc, w_ada, b_ada.reshape(n_l, 1, n_out))


def _mod_specs(n_b, grid_rank):
    if grid_rank == 2:
        return (pl.BlockSpec((1, 1, 6 * D_MODEL), lambda b, j: (b, 0, 0)),
                pl.BlockSpec((1, 1, 6 * D_MODEL), lambda b, j: (n_b, 0, 0)))
    return (pl.BlockSpec((1, 1, 6 * D_MODEL), lambda b, j, f: (b, 0, 0)),
            pl.BlockSpec((1, 1, 6 * D_MODEL), lambda b, j, f: (n_b, 0, 0)))


def _norm_kernel(x_ref, g_ref, mb_ref, mc_ref, h_ref, *, tm):
    is_ctx = _is_ctx_rows(tm, 1)
    h = _norm_mod(x_ref[0], g_ref[...], _mod_vec(mb_ref, mc_ref, 0, is_ctx), _mod_vec(mb_ref, mc_ref, 1, is_ctx))
    h_ref[0] = h.astype(h_ref.dtype)


def _norm1(x, g, modl):
    n_b = x.shape[0]
    tm = 768
    mb, mc = _mod_specs(n_b, 2)
    return pl.pallas_call(
        functools.partial(_norm_kernel, tm=tm),
        grid=(n_b, T_ALL // tm),
        in_specs=[pl.BlockSpec((1, tm, D_MODEL), lambda b, j: (b, j, 0)),
                  pl.BlockSpec((1, D_MODEL), lambda b, j: (0, 0)), mb, mc],
        out_specs=pl.BlockSpec((1, tm, D_MODEL), lambda b, j: (b, j, 0)),
        out_shape=jax.ShapeDtypeStruct(x.shape, BF16),
        compiler_params=_cparams(("parallel", "parallel")),
        name="norm1",
    )(x, g.reshape(1, D_MODEL), modl, modl)


def _mm_kernel(a_ref, w_ref, o_ref):
    o_ref[...] = _dot(a_ref[...], w_ref[...]).astype(o_ref.dtype)


def _mm(a, w, out_dtype, tm, tn, name):
    m, k = a.shape
    n = w.shape[1]
    return pl.pallas_call(
        _mm_kernel,
        grid=(m // tm, n // tn),
        in_specs=[pl.BlockSpec((tm, k), lambda i, j: (i, 0)),
                  pl.BlockSpec((k, tn), lambda i, j: (0, j))],
        out_specs=pl.BlockSpec((tm, tn), lambda i, j: (i, j)),
        out_shape=jax.ShapeDtypeStruct((m, n), out_dtype),
        compiler_params=_cparams(("parallel", "arbitrary")),
        name=name,
    )(a, w)


def _softmax_pv_pair(q_even, q_odd, k_of, v_of, extra=None):
    acc = None
    for hh, q in enumerate((q_even, q_odd)):
        s = _dot_nt(q, k_of(hh))
        m = jnp.max(s, axis=-1, keepdims=True)
        if extra is not None:
            bias, k2_of, v2_of = extra
            s = s + bias(hh)
            s2 = _dot_nt(q, k2_of(hh))
            m = jnp.maximum(jnp.max(s, axis=-1, keepdims=True), jnp.max(s2, axis=-1, keepdims=True))
            p2 = jnp.exp(s2 - m)
        p = jnp.exp(s - m)
        l = jnp.sum(p, axis=-1, keepdims=True)
        o = _dot(p.astype(BF16), v_of(hh))
        if extra is not None:
            l = l + jnp.sum(p2, axis=-1, keepdims=True)
            o = o + _dot(p2.astype(BF16), v2_of(hh))
        o = o / l
        acc = o if acc is None else acc + o
    return acc


def _pad_heads(blk):
    lo = _lane(blk.shape) < HEAD_DIM
    return jnp.where(lo, blk, 0.0), jnp.where(lo, pltpu.roll(blk, HEAD_DIM, 1), 0.0)


def _head_rms(x, gain):
    ms = jnp.sum(x * x, axis=-1, keepdims=True) * (1.0 / HEAD_DIM)
    return x * lax.rsqrt(ms + NORM_EPS) * gain


def _gqa_kernel(pq_ref, pall_ref, cq_ref, sq_ref, call_ref, sall_ref, qg_ref, kg_ref, o_ref, k_scr, v_scr):
    j = pl.program_id(1)

    @pl.when(j == 0)
    def _prep():
        kblk = pall_ref[0, :, 512:640].astype(F32)
        for g, kh in enumerate(_pad_heads(kblk)):
            kh = _rope(_head_rms(kh, kg_ref[...]), call_ref[...], sall_ref[...], 16)
            k_scr[:, 128 * g:128 * (g + 1)] = kh.astype(BF16)
        vblk = pall_ref[0, :, 640:768].astype(F32)
        vrot = pltpu.roll(vblk, HEAD_DIM, 1)
        lo = _lane(vblk.shape) < HEAD_DIM
        v_scr[:, 0:128] = jnp.where(lo, vblk, 0.0).astype(BF16)
        v_scr[:, 128:256] = jnp.where(lo, 0.0, vrot).astype(BF16)
        v_scr[:, 256:384] = jnp.where(lo, vrot, 0.0).astype(BF16)
        v_scr[:, 384:512] = jnp.where(lo, 0.0, vblk).astype(BF16)

    def attend(n_keys):
        for jp in range(N_HEADS // 2):
            blk = pq_ref[0, :, 128 * jp:128 * (jp + 1)].astype(F32)
            qs = [(_rope(_head_rms(qh, qg_ref[...]), cq_ref[...], sq_ref[...], 16) * (HEAD_DIM ** -0.5)).astype(BF16)
                  for qh in _pad_heads(blk)]
            g = (2 * jp) // (N_HEADS // A_KV_HEADS)
            o = _softmax_pv_pair(
                qs[0], qs[1],
                lambda hh: k_scr[0:n_keys, 128 * g:128 * (g + 1)],
                lambda hh: v_scr[0:n_keys, 128 * (2 * g + hh):128 * (2 * g + hh + 1)])
            o_ref[0, :, 128 * jp:128 * (jp + 1)] = o.astype(o_ref.dtype)

    @pl.when(j == 0)
    def _ctx():
        attend(CTX_LEN)

    @pl.when(j > 0)
    def _lat():
        attend(T_ALL)


def _gqa(p_a, cos, sin, q_gain, k_gain):
    n_b = p_a.shape[0]
    pad = lambda g: jnp.concatenate([g, jnp.zeros((HEAD_DIM,), F32)]).reshape(1, 128)
    tile = lambda b, j: (b, j, 0)
    whole = lambda b, j: (b, 0, 0)
    return pl.pallas_call(
        _gqa_kernel,
        grid=(n_b, N_QT),
        in_specs=[pl.BlockSpec((1, Q_TILE, A_IN), tile),
                  pl.BlockSpec((1, T_ALL, A_IN), whole),
                  pl.BlockSpec((Q_TILE, 128), lambda b, j: (j, 0)),
                  pl.BlockSpec((Q_TILE, 128), lambda b, j: (j, 0)),
                  pl.BlockSpec((T_ALL, 128), lambda b, j: (0, 0)),
                  pl.BlockSpec((T_ALL, 128), lambda b, j: (0, 0)),
                  pl.BlockSpec((1, 128), lambda b, j: (0, 0)),
                  pl.BlockSpec((1, 128), lambda b, j: (0, 0))],
        out_specs=pl.BlockSpec((1, Q_TILE, 512), tile),
        out_shape=jax.ShapeDtypeStruct((n_b, T_ALL, 512), BF16),
        scratch_shapes=[pltpu.VMEM((T_ALL, 256), BF16), pltpu.VMEM((T_ALL, 512), BF16)],
        compiler_params=_cparams(("parallel", "arbitrary")),
        name="gqa",
    )(p_a, p_a, cos, sin, cos, sin, pad(q_gain), pad(k_gain))


def _mla_kernel(pq_ref, pall_ref, cq_ref, sq_ref, call_ref, sall_ref, qg_ref, kvg_ref, wq_ref, wkv_ref,
                o_ref, k_scr, v_scr):
    j = pl.program_id(1)
    kw = N_HEADS * 128

    @pl.when(j == 0)
    def _prep():
        def body(i, carry):
            r0 = pl.multiple_of(i * Q_TILE, Q_TILE)
            rows = pl.ds(r0, Q_TILE)
            ckv = pall_ref[0, rows, B_Q_LORA:B_Q_LORA + B_KV_LORA].astype(F32)
            n = ckv * lax.rsqrt(jnp.mean(ckv * ckv, axis=-1, keepdims=True) + NORM_EPS) * kvg_ref[...]
            kv = _dot(n.astype(BF16), wkv_ref[...])
            kr = _rope(pall_ref[0, rows, 640:768].astype(F32), call_ref[rows, :], sall_ref[rows, :], 8)
            for h in range(N_HEADS):
                k_scr[rows, 128 * h:128 * (h + 1)] = (kv[:, 128 * h:128 * (h + 1)] + kr).astype(BF16)
            v_scr[rows, :] = kv[:, kw:].astype(BF16)
            return carry
        lax.fori_loop(0, N_QT, body, 0)

    def attend(n_keys):
        cq = pq_ref[0, :, 0:B_Q_LORA].astype(F32)
        n = cq * lax.rsqrt(jnp.mean(cq * cq, axis=-1, keepdims=True) + NORM_EPS) * qg_ref[...]
        q = _dot(n.astype(BF16), wq_ref[...])
        scale = (B_NOPE + B_ROPE) ** -0.5
        for jp in range(N_HEADS // 2):
            qs = [(_rope(q[:, 128 * h:128 * (h + 1)], cq_ref[...], sq_ref[...], 8) * scale).astype(BF16)
                  for h in (2 * jp, 2 * jp + 1)]
            o = _softmax_pv_pair(
                qs[0], qs[1],
                lambda hh: k_scr[0:n_keys, 128 * (2 * jp + hh):128 * (2 * jp + hh + 1)],
                lambda hh: v_scr[0:n_keys, 128 * (2 * jp + hh):128 * (2 * jp + hh + 1)])
            o_ref[0, :, 128 * jp:128 * (jp + 1)] = o.astype(o_ref.dtype)

    @pl.when(j == 0)
    def _ctx():
        attend(CTX_LEN)

    @pl.when(j > 0)
    def _lat():
        attend(T_ALL)


def _mla(p_b, cos, sin, q_gain, kv_gain, wq, wkv):
    n_b = p_b.shape[0]
    tile = lambda b, j: (b, j, 0)
    whole = lambda b, j: (b, 0, 0)
    const = lambda b, j: (0, 0)
    w = p_b.shape[-1]
    return pl.pallas_call(
        _mla_kernel,
        grid=(n_b, N_QT),
        in_specs=[pl.BlockSpec((1, Q_TILE, w), tile),
                  pl.BlockSpec((1, T_ALL, w), whole),
                  pl.BlockSpec((Q_TILE, 128), lambda b, j: (j, 0)),
                  pl.BlockSpec((Q_TILE, 128), lambda b, j: (j, 0)),
                  pl.BlockSpec((T_ALL, 128), const),
                  pl.BlockSpec((T_ALL, 128), const),
                  pl.BlockSpec((1, B_Q_LORA), const),
                  pl.BlockSpec((1, B_KV_LORA), const),
                  pl.BlockSpec(wq.shape, const),
                  pl.BlockSpec(wkv.shape, const)],
        out_specs=pl.BlockSpec((1, Q_TILE, 512), tile),
        out_shape=jax.ShapeDtypeStruct((n_b, T_ALL, 512), BF16),
        scratch_shapes=[pltpu.VMEM((T_ALL, N_HEADS * 128), BF16), pltpu.VMEM((T_ALL, N_HEADS * 128), BF16)],
        compiler_params=_cparams(("parallel", "arbitrary")),
        name="mla",
    )(p_b, p_b, cos, sin, cos, sin, q_gain.reshape(1, -1), kv_gain.reshape(1, -1), wq, wkv)


def _nat_kernel(pq_ref, pall_ref, bias_ref, o_ref, k_scr, v_scr):
    j = pl.program_id(1)

    @pl.when(j == 0)
    def _prep():
        for jb in range(N_HEADS // 2):
            kblk = pall_ref[0, :, 512 + 128 * jb:512 + 128 * (jb + 1)].astype(F32)
            for hh, kh in enumerate(_pad_heads(kblk)):
                h = 2 * jb + hh
                k_scr[:, 128 * h:128 * (h + 1)] = kh.astype(BF16)
            vblk = pall_ref[0, :, 1024 + 128 * jb:1024 + 128 * (jb + 1)]
            lo = _lane(vblk.shape) < HEAD_DIM
            zero = jnp.zeros_like(vblk)
            v_scr[:, 128 * (2 * jb):128 * (2 * jb + 1)] = jnp.where(lo, vblk, zero)
            v_scr[:, 128 * (2 * jb + 1):128 * (2 * jb + 2)] = jnp.where(lo, zero, vblk)

    def q_pair(jp):
        blk = pq_ref[0, :, 128 * jp:128 * (jp + 1)].astype(F32)
        return [(qh * (HEAD_DIM ** -0.5)).astype(BF16) for qh in _pad_heads(blk)]

    hs = lambda jp, hh: slice(128 * (2 * jp + hh), 128 * (2 * jp + hh + 1))

    @pl.when(j == 0)
    def _ctx():
        for jp in range(N_HEADS // 2):
            qs = q_pair(jp)
            o = _softmax_pv_pair(qs[0], qs[1],
                                 lambda hh: k_scr[0:CTX_LEN, hs(jp, hh)],
                                 lambda hh: v_scr[0:CTX_LEN, hs(jp, hh)])
            o_ref[0, :, 128 * jp:128 * (jp + 1)] = o.astype(o_ref.dtype)

    @pl.when(j > 0)
    def _lat():
        first_row = jnp.clip(NAT_QROWS * (j - 1) - NA_ROWS // 2, 0, N_ROWS - NAT_KROWS)
        start = pl.multiple_of(CTX_LEN + first_row * GRID_W, Q_TILE)
        win = pl.ds(start, NAT_KWIN)
        for jp in range(N_HEADS // 2):
            qs = q_pair(jp)
            o = _softmax_pv_pair(
                qs[0], qs[1],
                lambda hh: k_scr[win, hs(jp, hh)],
                lambda hh: v_scr[win, hs(jp, hh)],
                extra=(lambda hh: bias_ref[0, 2 * jp + hh],
                       lambda hh: k_scr[0:CTX_LEN, hs(jp, hh)],
                       lambda hh: v_scr[0:CTX_LEN, hs(jp, hh)]))
            o_ref[0, :, 128 * jp:128 * (jp + 1)] = o.astype(o_ref.dtype)


def _nat_bias_table(rel_bias):
    n_dr, n_dc = 2 * NA_ROWS - 1, 2 * NA_COLS - 1
    cols = np.arange(GRID_W)
    pick_col = (cols[None, None, :] - cols[None, :, None] + NA_COLS - 1 == np.arange(n_dc)[:, None, None])
    pick_row = np.zeros((3, NAT_QROWS, NAT_KROWS, n_dr), np.float32)
    valid = np.zeros((3, Q_TILE, NAT_KWIN), bool)
    for t, qb in enumerate((0, 1, N_ROWS // NAT_QROWS - 1)):
        first_row = int(np.clip(NAT_QROWS * qb - NA_ROWS // 2, 0, N_ROWS - NAT_KROWS))
        rq = NAT_QROWS * qb + np.arange(NAT_QROWS)
        rk = first_row + np.arange(NAT_KROWS)
        pick_row[t] = (rk[None, :, None] - rq[:, None, None] + NA_ROWS - 1 == np.arange(n_dr)[None, None, :])
        ql, kl = np.arange(Q_TILE), np.arange(NAT_KWIN)
        r, c = NAT_QROWS * qb + ql // GRID_W, ql % GRID_W
        kr, kc = first_row + kl // GRID_W, kl % GRID_W
        r0 = np.clip(r - NA_ROWS // 2, 0, N_ROWS - NA_ROWS)
        c0 = np.clip(c - NA_COLS // 2, 0, GRID_W - NA_COLS)
        valid[t] = ((kr[None, :] >= r0[:, None]) & (kr[None, :] < r0[:, None] + NA_ROWS)
                    & (kc[None, :] >= c0[:, None]) & (kc[None, :] < c0[:, None] + NA_COLS))
    by_col = jnp.einsum('hdc,cab->hdab', rel_bias.astype(F32), jnp.asarray(pick_col, F32), precision=lax.Precision.HIGHEST)
    tab = jnp.einsum('tqkd,hdab->thqakb', jnp.asarray(pick_row), by_col, precision=lax.Precision.HIGHEST)
    tab = tab.reshape(3, N_HEADS, Q_TILE, NAT_KWIN)
    return jnp.where(jnp.asarray(valid)[:, None], tab, NEG_BIG)


def _nat(p_d, bias_tab):
    n_b = p_d.shape[0]
    tile = lambda b, j: (b, j, 0)
    n_lat_tiles = N_QT - 1

    def bias_idx(b, j):
        qb = j - 1
        return (jnp.where(qb <= 0, 0, jnp.where(qb == n_lat_tiles - 1, 2, 1)), 0, 0, 0)

    return pl.pallas_call(
        _nat_kernel,
        grid=(n_b, N_QT),
        in_specs=[pl.BlockSpec((1, Q_TILE, D_IN), tile),
                  pl.BlockSpec((1, T_ALL, D_IN), lambda b, j: (b, 0, 0)),
                  pl.BlockSpec((1, N_HEADS, Q_TILE, NAT_KWIN), bias_idx)],
        out_specs=pl.BlockSpec((1, Q_TILE, 512), tile),
        out_shape=jax.ShapeDtypeStruct((n_b, T_ALL, 512), BF16),
        scratch_shapes=[pltpu.VMEM((T_ALL, N_HEADS * 128), BF16), pltpu.VMEM((T_ALL, N_HEADS * 128), BF16)],
        compiler_params=_cparams(("parallel", "arbitrary")),
        name="nat",
    )(p_d, p_d, bias_tab)


def _seg_sum(x, ones_bd):
    return _dot3_exact_rhs(x, ones_bd)


def _rwkv_prep_kernel(z_ref, zp_ref, zn_ref, mu_ref, w0_ref, a0_ref, kk_ref, ka_ref, rk_ref,
                      wdh_ref, wdl_ref, wah_ref, wal_ref, wgh_ref, wgl_ref, ones_ref,
                      r_out, v_out, kk_out, kd_out, lw_out, bd_out, bonus_out, g_out):
    j = pl.program_id(1)
    z = z_ref[0]
    tm = z.shape[0]
    row = lax.broadcasted_iota(jnp.int32, (tm, 1), 0)
    prev_row = jnp.where(j <= 1, 0.0, zp_ref[0, 7:8, :])
    next_row = jnp.where((j == 0) | (j == N_QT - 1), 0.0, zn_ref[0, 0:1, :])
    z_prev = jnp.where(row == 0, prev_row, pltpu.roll(z, 1, 0))
    z_next = jnp.where(row == tm - 1, next_row, pltpu.roll(z, tm - 1, 0))
    zs = z + (0.5 * (z_prev + z_next) - z) * mu_ref[...]

    r = zs[:, 0:512]
    k = zs[:, 512:1024]
    v = zs[:, 1024:1536]
    w_lo = zs[:, 1536:1664]
    a_lo = zs[:, 1664:1792]
    g_lo = zs[:, 1792:1920]
    ones_bd = ones_ref[...]

    def lora(x, wh_ref, wl_ref):
        xh, xl = _split_bf16(x)
        return _dot(xh, wh_ref[...]) + _dot(xh, wl_ref[...]) + _dot(xl, wh_ref[...])

    kkr = k * kk_ref[...]
    nrm = jnp.maximum(jnp.sqrt(_seg_sum(kkr * kkr, ones_bd)), 1e-12)
    kk = kkr / nrm
    dec = lora(jnp.tanh(w_lo), wdh_ref, wdl_ref)
    aaa = lora(a_lo, wah_ref, wal_ref)
    g = lora(_sigmoid(g_lo), wgh_ref, wgl_ref)
    r_out[0] = r
    v_out[0] = v
    kk_out[0] = kk
    g_out[0] = g
    ksum = None
    for d in range(2):
        u = -(w0_ref[d:d + 1, :] + dec[:, 512 * d:512 * (d + 1)])
        softplus = jnp.maximum(u, 0.0) + jnp.log(1.0 + jnp.exp(-jnp.abs(u)))
        logw = -softplus - 0.5
        lw_out[d, 0] = -jnp.exp(logw)
        a = _sigmoid(a0_ref[d:d + 1, :] + aaa[:, 512 * d:512 * (d + 1)])
        bd_out[d, 0] = kk * a
        kd = k * (1.0 + (a - 1.0) * ka_ref[...])
        kd_out[d, 0] = kd
        ksum = kd if ksum is None else ksum + kd
    bonus_out[0] = _seg_sum(r * ksum * rk_ref[...], ones_bd) * v


def _rwkv_prep(p_c, lp):
    n_b = p_c.shape[0]
    tile = lambda b, j: (b, j, 0)
    const = lambda b, j: (0, 0)
    blocks8 = T_ALL // 8
    tpb = Q_TILE // 8
    prev = lambda b, j: (b, jnp.maximum(j * tpb - 1, 0), 0)
    nxt = lambda b, j: (b, jnp.minimum((j + 1) * tpb, blocks8 - 1), 0)
    o3 = jax.ShapeDtypeStruct((n_b, T_ALL, C_W), F32)
    o4 = jax.ShapeDtypeStruct((2, n_b, T_ALL, C_W), F32)
    s3 = pl.BlockSpec((1, Q_TILE, C_W), tile)
    s4 = pl.BlockSpec((2, 1, Q_TILE, C_W), lambda b, j: (0, b, j, 0))
    small = [lp['c_mu'].reshape(1, C_IN), lp['c_w0'], lp['c_a0'], lp['c_k_k'].reshape(1, C_W),
             lp['c_k_a'].reshape(1, C_W), lp['c_r_k'].reshape(1, C_W)]
    bd2 = lambda w: jnp.concatenate(
        [jnp.concatenate([w[0], jnp.zeros_like(w[0])], 1), jnp.concatenate([jnp.zeros_like(w[1]), w[1]], 1)], 0)
    mats = []
    for w in (bd2(lp['c_w_decay']), bd2(lp['c_w_aaa']), lp['c_w_gate']):
        mats.extend(_split_bf16(w))
    ones_bd = jnp.asarray(np.kron(np.eye(N_HEADS), np.ones((HEAD_DIM, HEAD_DIM))), BF16)
    ins = small + mats + [ones_bd]
    return pl.pallas_call(
        _rwkv_prep_kernel,
        grid=(n_b, N_QT),
        in_specs=[pl.BlockSpec((1, Q_TILE, C_IN), tile),
                  pl.BlockSpec((1, 8, C_IN), prev),
                  pl.BlockSpec((1, 8, C_IN), nxt)] + [pl.BlockSpec(a.shape, const) for a in ins],
        out_specs=[s3, s3, s3, s4, s4, s4, s3, s3],
        out_shape=[o3, o3, o3, o4, o4, o4, o3, o3],
        compiler_params=_cparams(("parallel", "parallel")),
        name="rwkv_prep",
    )(p_c, p_c, p_c, *ins)


def _dot3_exact_rhs_lhs(m_bf16, a):
    a0 = a.astype(BF16)
    r1 = a - a0.astype(F32)
    a1 = r1.astype(BF16)
    a2 = (r1 - a1.astype(F32)).astype(BF16)
    return _dot(m_bf16, a0) + _dot(m_bf16, a1) + _dot(m_bf16, a2)


def _scan_masks(fwd):
    n = HG * CHUNK
    ri = lax.broadcasted_iota(jnp.int32, (n, n), 0)
    ci = lax.broadcasted_iota(jnp.int32, (n, n), 1)
    same = (ri // CHUNK) == (ci // CHUNK)
    pos = (lambda t: t % CHUNK) if fwd else (lambda t: CHUNK - 1 - t % CHUNK)
    tr, tc = pos(ri), pos(ci)
    strict = same & (tr > tc)
    incl = same & (tr >= tc)
    lower_left = [same & ((tr // (2 * h)) == (tc // (2 * h))) & (((tr // h) % 2) == 1) & (((tc // h) % 2) == 0)
                  for h in (1, 2, 4, 8, 16, 32)]
    t_r = pos(lax.broadcasted_iota(jnp.int32, (CHUNK, CHUNK), 0))
    t_c = pos(lax.broadcasted_iota(jnp.int32, (CHUNK, CHUNK), 1))
    cum_mat = jnp.where(t_r >= t_c, 1.0, 0.0).astype(BF16)
    return strict, incl, lower_left, cum_mat


def _rwkv_scan_kernel(rf_ref, vf_ref, kkf_ref, kdf_ref, lwf_ref, bdf_ref,
                      rb_ref, vb_ref, kkb_ref, kdb_ref, lwb_ref, bdb_ref, yf_ref, yb_ref, s_scr):
    j = pl.program_id(1)
    n = HG * CHUNK
    eye = (lax.broadcasted_iota(jnp.int32, (n, n), 0) == lax.broadcasted_iota(jnp.int32, (n, n), 1)).astype(F32)
    head_of_lane = lax.broadcasted_iota(jnp.int32, (CHUNK, HGW), 1) // HEAD_DIM
    masks = (_scan_masks(True), _scan_masks(False))
    dir_refs = ((rf_ref, vf_ref, kkf_ref, kdf_ref, lwf_ref, bdf_ref, yf_ref),
                (rb_ref, vb_ref, kkb_ref, kdb_ref, lwb_ref, bdb_ref, yb_ref))
    chunks_per_tile = Q_TILE // CHUNK

    def stack(x):
        return jnp.concatenate([jnp.where(head_of_lane == p, x, 0.0) for p in range(HG)], axis=0)

    @pl.when(j == 0)
    def _init():
        s_scr[...] = jnp.zeros_like(s_scr)

    def chain(idx, strict, incl, lower_left, kk, bd, kd, r, v, e_pos, e_neg, e_prev, e_rest, p_end):
        al4 = stack(kk * e_prev)
        be4 = stack(bd * e_neg)
        ka4 = stack(kd * e_neg)
        rh4 = stack(r * e_pos)
        bee4 = stack(bd * e_rest)
        kae4 = stack(kd * e_rest)
        v4 = stack(v).astype(BF16)
        al4b, be4b, ka4b, rh4b = (x.astype(BF16) for x in (al4, be4, ka4, rh4))

        l_ab = jnp.where(strict, _dot_nt(al4b, be4b), 0.0)
        l_ak = jnp.where(strict, _dot_nt(al4b, ka4b), 0.0)
        m_rb = jnp.where(incl, _dot_nt(rh4b, be4b), 0.0).astype(BF16)
        m_rk = jnp.where(incl, _dot_nt(rh4b, ka4b), 0.0).astype(BF16)

        x = eye - jnp.where(lower_left[0], l_ab, 0.0)
        for half_mask in lower_left[1:]:
            xb = x.astype(BF16)
            x = x - _dot(xb, _dot(jnp.where(half_mask, l_ab, 0.0).astype(BF16), xb).astype(BF16))
        tb = x.astype(BF16)

        w4 = _dot(tb, al4b)
        u04 = _dot(tb, _dot(l_ak.astype(BF16), v4).astype(BF16))
        w4b, u04b = w4.astype(BF16), u04.astype(BF16)
        q4 = rh4 - _dot(m_rb, w4b)
        y04 = _dot(m_rk, v4) - _dot(m_rb, u04b)
        bee4t = bee4.T.astype(BF16)
        g4 = eye * p_end - _dot(bee4t, w4b)
        h4 = _dot(kae4.T.astype(BF16), v4) - _dot(bee4t, u04b)

        s = s_scr[idx]
        y4 = _dot(q4.astype(BF16), s.astype(BF16)) + y04
        y = y4[0:CHUNK]
        for p in range(1, HG):
            y = y + y4[p * CHUNK:(p + 1) * CHUNK]
        s_scr[idx] = _dot3(g4, s) + h4
        return y

    def body(i, carry):
        for d in range(2):
            r_ref, v_ref, kk_ref, kd_ref, lw_ref, bd_ref, y_ref = dir_refs[d]
            strict, incl, lower_left, cum_mat = masks[d]
            c = i if d == 0 else chunks_per_tile - 1 - i
            rows = pl.ds(pl.multiple_of(c * CHUNK, CHUNK), CHUNK)
            lw = lw_ref[0, 0, rows, :]
            cum = _dot3_exact_rhs_lhs(cum_mat, lw)
            total = jnp.sum(lw, axis=0, keepdims=True)
            e_pos = jnp.exp(cum)
            e_neg = jnp.exp(-cum)
            e_prev = jnp.exp(cum - lw)
            e_rest = jnp.exp(total - cum)
            p_end = jnp.exp(total)
            for g in range(C_W // HGW):
                ln = slice(g * HGW, (g + 1) * HGW)
                y = chain(2 * d + g, strict, incl, lower_left,
                          kk_ref[0, rows, ln], bd_ref[0, 0, rows, ln], kd_ref[0, 0, rows, ln], r_ref[0, rows, ln],
                          v_ref[0, rows, ln], e_pos[:, ln], e_neg[:, ln], e_prev[:, ln], e_rest[:, ln], p_end[:, ln])
                y_ref[0, rows, ln] = y
        return carry

    lax.fori_loop(0, chunks_per_tile, body, 0)


def _rwkv_scan(r, v, kk, kd, lw, bd):
    n_b = r.shape[0]
    bwd_tile = lambda j: jnp.where(j == 0, 0, N_QT - j)
    f3 = pl.BlockSpec((1, Q_TILE, C_W), lambda b, j: (b, j, 0))
    f4 = pl.BlockSpec((1, 1, Q_TILE, C_W), lambda b, j: (0, b, j, 0))
    b3 = pl.BlockSpec((1, Q_TILE, C_W), lambda b, j: (b, bwd_tile(j), 0))
    b4 = pl.BlockSpec((1, 1, Q_TILE, C_W), lambda b, j: (1, b, bwd_tile(j), 0))
    y_shape = jax.ShapeDtypeStruct((n_b, T_ALL, C_W), F32)
    return pl.pallas_call(
        _rwkv_scan_kernel,
        grid=(n_b, N_QT),
        in_specs=[f3, f3, f3, f4, f4, f4, b3, b3, b3, b4, b4, b4],
        out_specs=[f3, b3],
        out_shape=[y_shape, y_shape],
        scratch_shapes=[pltpu.VMEM((2 * (C_W // HGW), HGW, HGW), F32)],
        compiler_params=_cparams(("parallel", "arbitrary")),
        name="rwkv_scan",
    )(r, v, kk, kd, lw, bd, r, v, kk, kd, lw, bd)


def _rwkv_out_kernel(yf_ref, yb_ref, bonus_ref, g_ref, gw_ref, gb_ref, ones_ref, o_ref):
    y = yf_ref[0] + yb_ref[0]
    ones_bd = ones_ref[...]
    mean = _seg_sum(y, ones_bd) * (1.0 / HEAD_DIM)
    yc = y - mean
    var = _seg_sum(yc * yc, ones_bd) * (1.0 / HEAD_DIM)
    yn = yc * lax.rsqrt(var + C_GN_EPS) * gw_ref[...] + gb_ref[...]
    o_ref[0] = ((yn + bonus_ref[0]) * g_ref[0]).astype(o_ref.dtype)


def _rwkv_out(y_f, y_b, bonus, g, gn_w, gn_b):
    n_b = bonus.shape[0]
    tm = 768
    tile = lambda b, j: (b, j, 0)
    const = lambda b, j: (0, 0)
    ones_bd = jnp.asarray(np.kron(np.eye(N_HEADS), np.ones((HEAD_DIM, HEAD_DIM))), BF16)
    return pl.pallas_call(
        _rwkv_out_kernel,
        grid=(n_b, T_ALL // tm),
        in_specs=[pl.BlockSpec((1, tm, C_W), tile), pl.BlockSpec((1, tm, C_W), tile),
                  pl.BlockSpec((1, tm, C_W), tile), pl.BlockSpec((1, tm, C_W), tile),
                  pl.BlockSpec((1, C_W), const), pl.BlockSpec((1, C_W), const),
                  pl.BlockSpec((C_W, C_W), const)],
        out_specs=pl.BlockSpec((1, tm, C_W), tile),
        out_shape=jax.ShapeDtypeStruct((n_b, T_ALL, C_W), BF16),
        compiler_params=_cparams(("parallel", "parallel")),
        name="rwkv_out",
    )(y_f, y_b, bonus, g, gn_w.reshape(1, C_W), gn_b.reshape(1, C_W), ones_bd)


def _merge_kernel(x_ref, oa_ref, ob_ref, oc_ref, od_ref, gate_ref, wb_ref, wo_ref, mb_ref, mc_ref, out_ref, *, tm):
    is_ctx = _is_ctx_rows(tm, 1)
    y = None
    for i, o_ref in enumerate((oa_ref, ob_ref, oc_ref, od_ref)):
        z = _dot(o_ref[0], wb_ref[i])
        sg = _sigmoid(gate_ref[0, :, i * D_MODEL:(i + 1) * D_MODEL].astype(F32))
        y = sg * z if y is None else y + sg * z
    z = _dot(y.astype(BF16), wo_ref[...])
    out_ref[0] = x_ref[0] + _mod_vec(mb_ref, mc_ref, 2, is_ctx) * z


def _merge(x, outs, gates, wb, wo, modl):
    n_b = x.shape[0]
    tm = 768
    tile = lambda b, j: (b, j, 0)
    mb, mc = _mod_specs(n_b, 2)
    o_spec = pl.BlockSpec((1, tm, 512), tile)
    return pl.pallas_call(
        functools.partial(_merge_kernel, tm=tm),
        grid=(n_b, T_ALL // tm),
        in_specs=[pl.BlockSpec((1, tm, D_MODEL), tile), o_spec, o_spec, o_spec, o_spec,
                  pl.BlockSpec((1, tm, GATE_IN), tile),
                  pl.BlockSpec(wb.shape, lambda b, j: (0, 0, 0)),
                  pl.BlockSpec(wo.shape, lambda b, j: (0, 0)), mb, mc],
        out_specs=pl.BlockSpec((1, tm, D_MODEL), tile),
        out_shape=jax.ShapeDtypeStruct(x.shape, F32),
        compiler_params=_cparams(("parallel", "parallel")),
        name="merge",
    )(x, *outs, gates, wb, wo, modl, modl)


def _mlp_kernel(x_ref, g_ref, mb_ref, mc_ref, w1_ref, w2_ref, out_ref, h_scr, acc_scr, *, tm, n_f):
    f = pl.program_id(2)
    is_ctx = _is_ctx_rows(tm, 1)

    @pl.when(f == 0)
    def _init():
        h = _norm_mod(x_ref[0], g_ref[...], _mod_vec(mb_ref, mc_ref, 3, is_ctx), _mod_vec(mb_ref, mc_ref, 4, is_ctx))
        h_scr[...] = h.astype(BF16)
        acc_scr[...] = jnp.zeros_like(acc_scr)

    a = jnp.square(jnp.maximum(_dot(h_scr[...], w1_ref[...]), 0.0))
    acc_scr[...] += _dot(a.astype(BF16), w2_ref[...])

    @pl.when(f == n_f - 1)
    def _fin():
        out_ref[0] = x_ref[0] + _mod_vec(mb_ref, mc_ref, 5, is_ctx) * acc_scr[...]


def _mlp(x, g, w1, w2, modl):
    n_b = x.shape[0]
    tm, tf = 768, 512
    n_f = D_FF // tf
    tile = lambda b, j, f: (b, j, 0)
    mb, mc = _mod_specs(n_b, 3)
    return pl.pallas_call(
        functools.partial(_mlp_kernel, tm=tm, n_f=n_f),
        grid=(n_b, T_ALL // tm, n_f),
        in_specs=[pl.BlockSpec((1, tm, D_MODEL), tile),
                  pl.BlockSpec((1, D_MODEL), lambda b, j, f: (0, 0)), mb, mc,
                  pl.BlockSpec((D_MODEL, tf), lambda b, j, f: (0, f)),
                  pl.BlockSpec((tf, D_MODEL), lambda b, j, f: (f, 0))],
        out_specs=pl.BlockSpec((1, tm, D_MODEL), tile),
        out_shape=jax.ShapeDtypeStruct(x.shape, F32),
        scratch_shapes=[pltpu.VMEM((tm, D_MODEL), BF16), pltpu.VMEM((tm, D_MODEL), F32)],
        compiler_params=_cparams(("parallel", "parallel", "arbitrary")),
        name="mlp",
    )(x, g.reshape(1, D_MODEL), modl, modl, w1, w2)


def _final_kernel(x_ref, g_ref, o_ref):
    x = x_ref[0]
    o_ref[0] = x * lax.rsqrt(jnp.mean(x * x, axis=-1, keepdims=True) + NORM_EPS) * g_ref[...]


def _final_norm(x, g):
    n_b = x.shape[0]
    tm = Q_TILE
    return pl.pallas_call(
        _final_kernel,
        grid=(n_b, SEQ // tm),
        in_specs=[pl.BlockSpec((1, tm, D_MODEL), lambda b, j: (b, j + CTX_LEN // tm, 0)),
                  pl.BlockSpec((1, D_MODEL), lambda b, j: (0, 0))],
        out_specs=pl.BlockSpec((1, tm, D_MODEL), lambda b, j: (b, j, 0)),
        out_shape=jax.ShapeDtypeStruct((n_b, SEQ, D_MODEL), F32),
        compiler_params=_cparams(("parallel", "parallel")),
        name="final_norm",
    )(x, g.reshape(1, D_MODEL))


def _rope_tables(half, lane0):
    t = np.arange(SEQ)
    inv = ROPE_THETA ** (-np.arange(half, dtype=np.float64) / half)
    cos = np.ones((T_ALL, V7X_LANES), np.float64)
    sin = np.zeros((T_ALL, V7X_LANES), np.float64)
    for part, pos in enumerate((t // GRID_W, t % GRID_W)):
        ang = pos[:, None].astype(np.float64) * inv[None, :]
        ang = ang.astype(np.float32).astype(np.float64)
        base = lane0 + 2 * half * part
        cos[CTX_LEN:, base:base + half] = np.cos(ang)
        cos[CTX_LEN:, base + half:base + 2 * half] = np.cos(ang)
        sin[CTX_LEN:, base:base + half] = -np.sin(ang)
        sin[CTX_LEN:, base + half:base + 2 * half] = np.sin(ang)
    return jnp.asarray(cos, F32), jnp.asarray(sin, F32)


def _layer_weights(l, w_in, b_w_q_up, b_w_kv_up, w_branch, w_out, w_mlp1, w_mlp2):
    wi = w_in[l]
    o_b = A_IN
    o_c = o_b + B_IN
    o_d = o_c + C_IN
    o_g = o_d + D_IN
    w_a = wi[:, :o_b]
    wb_raw = wi[:, o_b:o_c]
    z = lambda n: jnp.zeros((D_MODEL, n), F32)
    w_b = jnp.concatenate([wb_raw[:, :B_Q_LORA + B_KV_LORA], z(64), wb_raw[:, B_Q_LORA + B_KV_LORA:], z(32)], 1)
    wq = b_w_q_up[l].reshape(B_Q_LORA, N_HEADS, B_NOPE + B_ROPE)
    wq = jnp.concatenate([wq, jnp.zeros((B_Q_LORA, N_HEADS, 128 - B_NOPE - B_ROPE), F32)], -1)
    wkv = b_w_kv_up[l].reshape(B_KV_LORA, N_HEADS, 2 * HEAD_DIM)
    zk = jnp.zeros((B_KV_LORA, N_HEADS, HEAD_DIM), F32)
    wk = jnp.concatenate([wkv[:, :, :B_NOPE], zk], -1)
    even = (jnp.arange(N_HEADS) % 2 == 0)[None, :, None]
    wv = jnp.concatenate([jnp.where(even, wkv[:, :, B_NOPE:], 0.0), jnp.where(even, 0.0, wkv[:, :, B_NOPE:])], -1)
    return dict(
        w_a=w_a.astype(BF16), w_b=w_b.astype(BF16), w_c=wi[:, o_c:o_d].astype(BF16),
        w_d=wi[:, o_d:o_g].astype(BF16), w_g=wi[:, o_g:].astype(BF16),
        wq=wq.reshape(B_Q_LORA, N_HEADS * 128).astype(BF16),
        wkv=jnp.concatenate([wk.reshape(B_KV_LORA, -1), wv.reshape(B_KV_LORA, -1)], 1).astype(BF16),
        w_branch=w_branch[l].astype(BF16), w_out=w_out[l].astype(BF16),
        w_mlp1=w_mlp1[l].astype(BF16), w_mlp2=w_mlp2[l].astype(BF16))


def kernel(x, c, ctx, c_ctx, w_ada, b_ada, g_norm1, g_norm2, w_in, a_q_gain, a_k_gain, b_q_gain, b_kv_gain,
           b_w_q_up, b_w_kv_up, c_mu, c_w0, c_w_decay, c_a0, c_w_aaa, c_w_gate, c_k_k, c_k_a, c_r_k,
           c_gn_w, c_gn_b, d_rel_bias, w_branch, w_out, w_mlp1, w_mlp2, g_final):
    n_b = x.shape[0]
    assert x.shape[1:] == (SEQ, D_MODEL) and ctx.shape[1:] == (CTX_LEN, D_MODEL)
    m = n_b * T_ALL
    mod_rows = ((n_b + 1 + 7) // 8) * 8
    cc = jnp.concatenate([c, c_ctx[None, :], jnp.zeros((mod_rows - n_b - 1, D_MODEL), F32)], 0)
    mod_all = _ada(cc, w_ada, b_ada)
    cos_a, sin_a = _rope_tables(16, 0)
    cos_b, sin_b = _rope_tables(8, B_NOPE)
    xs = jnp.concatenate([ctx, x], axis=1)
    for l in range(DEPTH):
        lw = _layer_weights(l, w_in, b_w_q_up, b_w_kv_up, w_branch, w_out, w_mlp1, w_mlp2)
        lp = dict(c_mu=c_mu[l], c_w0=c_w0[l], c_w_decay=c_w_decay[l], c_a0=c_a0[l], c_w_aaa=c_w_aaa[l],
                  c_w_gate=c_w_gate[l], c_k_k=c_k_k[l], c_k_a=c_k_a[l], c_r_k=c_r_k[l])
        modl = mod_all[l].reshape(mod_rows, 1, 6 * D_MODEL)
        h = _norm1(xs, g_norm1[l], modl).reshape(m, D_MODEL)
        p_a = _mm(h, lw['w_a'], BF16, 768, A_IN, "w_in_a").reshape(n_b, T_ALL, A_IN)
        p_b = _mm(h, lw['w_b'], BF16, 768, 768, "w_in_b").reshape(n_b, T_ALL, 768)
        p_c = _mm(h, lw['w_c'], F32, 768, 640, "w_in_c").reshape(n_b, T_ALL, C_IN)
        p_d = _mm(h, lw['w_d'], BF16, 768, 768, "w_in_d").reshape(n_b, T_ALL, D_IN)
        p_g = _mm(h, lw['w_g'], BF16, 768, 1024, "w_in_g").reshape(n_b, T_ALL, GATE_IN)
        o_a = _gqa(p_a, cos_a, sin_a, a_q_gain[l], a_k_gain[l])
        o_b = _mla(p_b, cos_b, sin_b, b_q_gain[l], b_kv_gain[l], lw['wq'], lw['wkv'])
        r, v, kk, kd, lwd, bd, bonus, g = _rwkv_prep(p_c, lp)
        y_f, y_b = _rwkv_scan(r, v, kk, kd, lwd, bd)
        o_c = _rwkv_out(y_f, y_b, bonus, g, c_gn_w[l], c_gn_b[l])
        o_d = _nat(p_d, _nat_bias_table(d_rel_bias[l]))
        xs = _merge(xs, (o_a, o_b, o_c, o_d), p_g, lw['w_branch'], lw['w_out'], modl)
        xs = _mlp(xs, g_norm2[l], lw['w_mlp1'], lw['w_mlp2'], modl)
    return _final_norm(xs, g_final)
```

```python
import functools

import numpy as np
import jax
import jax.numpy as jnp
from jax import lax
from jax.experimental import pallas as pl
from jax.experimental.pallas import tpu as pltpu

F32 = jnp.float32
BF16 = jnp.bfloat16

D_MODEL = 1024
SEQ = 2048
DEPTH = 2
GRID_W = 64
N_ROWS = SEQ // GRID_W
CTX_LEN = 256
T_ALL = CTX_LEN + SEQ
HEAD_DIM = 64
ROPE_THETA = 10000.0
NORM_EPS = 1e-6
N_HEADS = 8
A_KV_HEADS = 2
B_Q_LORA = 384
B_KV_LORA = 256
B_NOPE = 64
B_ROPE = 32
C_W = 512
C_GN_EPS = 64e-5
NA_ROWS = 8
NA_COLS = 16
D_FF = 4 * D_MODEL
A_IN = 768
B_IN = 672
C_IN = 1920
D_IN = 1536
GATE_IN = 4096

V7X_LANES = 128
V7X_VMEM_LIMIT = 56 * 1024 * 1024

Q_TILE = 256
N_QT = T_ALL // Q_TILE
NAT_QROWS = Q_TILE // GRID_W
NAT_KROWS = 12
NAT_KWIN = NAT_KROWS * GRID_W
CHUNK = 64
N_CHUNK = T_ALL // CHUNK
N_CTX_CHUNK = CTX_LEN // CHUNK
HG = 4
HGW = HG * HEAD_DIM
NEG_BIG = -1e30


def _cparams(sem, vmem=V7X_VMEM_LIMIT):
    return pltpu.CompilerParams(dimension_semantics=sem, vmem_limit_bytes=vmem)


def _split_bf16(a):
    hi = a.astype(BF16)
    lo = (a - hi.astype(F32)).astype(BF16)
    return hi, lo


def _dot(a, b):
    return jnp.dot(a, b, preferred_element_type=F32)


def _dot_nt(a, b):
    return lax.dot_general(a, b, (((1,), (1,)), ((), ())), preferred_element_type=F32)


def _dot3(a, b):
    ah, al = _split_bf16(a)
    bh, bl = _split_bf16(b)
    return _dot(ah, bh) + _dot(ah, bl) + _dot(al, bh)


def _dot3_exact_rhs(a, b_bf16):
    a0 = a.astype(BF16)
    r1 = a - a0.astype(F32)
    a1 = r1.astype(BF16)
    a2 = (r1 - a1.astype(F32)).astype(BF16)
    return _dot(a0, b_bf16) + _dot(a1, b_bf16) + _dot(a2, b_bf16)


def _sigmoid(x):
    return 1.0 / (1.0 + jnp.exp(-x))


def _lane(shape):
    return lax.broadcasted_iota(jnp.int32, shape, len(shape) - 1)


def _rope(x, cos, sin, half):
    n = x.shape[-1]
    lo = (_lane(x.shape) % (2 * half)) < half
    partner = jnp.where(lo, pltpu.roll(x, n - half, 1), pltpu.roll(x, half, 1))
    return x * cos + partner * sin


def _mod_vec(mb_ref, mc_ref, k, is_ctx):
    lat = mb_ref[0, :, k * D_MODEL:(k + 1) * D_MODEL]
    ctx = mc_ref[0, :, k * D_MODEL:(k + 1) * D_MODEL]
    return jnp.where(is_ctx, ctx, lat)


def _is_ctx_rows(tm, tile_axis):
    row = pl.program_id(tile_axis) * tm + lax.broadcasted_iota(jnp.int32, (tm, 1), 0)
    return row < CTX_LEN


def _norm_mod(x, g, shift, scale):
    y = x * lax.rsqrt(jnp.mean(x * x, axis=-1, keepdims=True) + NORM_EPS) * g
    return y * (1.0 + scale) + shift


def _ada_kernel(c_ref, w_ref, b_ref, o_ref):
    c = c_ref[...]
    s = c * _sigmoid(c)
    o_ref[0] = _dot3(s, w_ref[0]) + b_ref[0]


def _ada(cc, w_ada, b_ada):
    n_l, _, n_out = w_ada.shape
    tn = 1536
    rows = cc.shape[0]
    return pl.pallas_call(
        _ada_kernel,
        grid=(n_l, n_out // tn),
        in_specs=[pl.BlockSpec((rows, D_MODEL), lambda l, j: (0, 0)),
                  pl.BlockSpec((1, D_MODEL, tn), lambda l, j: (l, 0, j)),
                  pl.BlockSpec((1, 1, tn), lambda l, j: (l, 0, j))],
        out_specs=pl.BlockSpec((1, rows, tn), lambda l, j: (l, 0, j)),
        out_shape=jax.ShapeDtypeStruct((n_l, rows, n_out), F32),
        compiler_params=_cparams(("arbitrary", "arbitrary")),
        name="ada",
    )(cc, w_ada, b_ada.reshape(n_l, 1, n_out))


def _mod_specs(n_b, grid_rank):
    if grid_rank == 2:
        return (pl.BlockSpec((1, 1, 6 * D_MODEL), lambda b, j: (b, 0, 0)),
                pl.BlockSpec((1, 1, 6 * D_MODEL), lambda b, j: (n_b, 0, 0)))
    return (pl.BlockSpec((1, 1, 6 * D_MODEL), lambda b, j, f: (b, 0, 0)),
            pl.BlockSpec((1, 1, 6 * D_MODEL), lambda b, j, f: (n_b, 0, 0)))


def _norm_kernel(x_ref, g_ref, mb_ref, mc_ref, h_ref, *, tm):
    is_ctx = _is_ctx_rows(tm, 1)
    h = _norm_mod(x_ref[0], g_ref[...], _mod_vec(mb_ref, mc_ref, 0, is_ctx), _mod_vec(mb_ref, mc_ref, 1, is_ctx))
    h_ref[0] = h.astype(h_ref.dtype)


def _norm1(x, g, modl):
    n_b = x.shape[0]
    tm = 768
    mb, mc = _mod_specs(n_b, 2)
    return pl.pallas_call(
        functools.partial(_norm_kernel, tm=tm),
        grid=(n_b, T_ALL // tm),
        in_specs=[pl.BlockSpec((1, tm, D_MODEL), lambda b, j: (b, j, 0)),
                  pl.BlockSpec((1, D_MODEL), lambda b, j: (0, 0)), mb, mc],
        out_specs=pl.BlockSpec((1, tm, D_MODEL), lambda b, j: (b, j, 0)),
        out_shape=jax.ShapeDtypeStruct(x.shape, BF16),
        compiler_params=_cparams(("parallel", "parallel")),
        name="norm1",
    )(x, g.reshape(1, D_MODEL), modl, modl)


def _mm_kernel(a_ref, w_ref, o_ref):
    o_ref[...] = _dot(a_ref[...], w_ref[...]).astype(o_ref.dtype)


def _mm(a, w, out_dtype, tm, tn, name):
    m, k = a.shape
    n = w.shape[1]
    return pl.pallas_call(
        _mm_kernel,
        grid=(m // tm, n // tn),
        in_specs=[pl.BlockSpec((tm, k), lambda i, j: (i, 0)),
                  pl.BlockSpec((k, tn), lambda i, j: (0, j))],
        out_specs=pl.BlockSpec((tm, tn), lambda i, j: (i, j)),
        out_shape=jax.ShapeDtypeStruct((m, n), out_dtype),
        compiler_params=_cparams(("parallel", "arbitrary")),
        name=name,
    )(a, w)


def _softmax_pv_pair(q_even, q_odd, k_of, v_of, extra=None):
    acc = None
    for hh, q in enumerate((q_even, q_odd)):
        s = _dot_nt(q, k_of(hh))
        m = jnp.max(s, axis=-1, keepdims=True)
        if extra is not None:
            bias, k2_of, v2_of = extra
            s = s + bias(hh)
            s2 = _dot_nt(q, k2_of(hh))
            m = jnp.maximum(jnp.max(s, axis=-1, keepdims=True), jnp.max(s2, axis=-1, keepdims=True))
            p2 = jnp.exp(s2 - m)
        p = jnp.exp(s - m)
        l = jnp.sum(p, axis=-1, keepdims=True)
        o = _dot(p.astype(BF16), v_of(hh))
        if extra is not None:
            l = l + jnp.sum(p2, axis=-1, keepdims=True)
            o = o + _dot(p2.astype(BF16), v2_of(hh))
        o = o / l
        acc = o if acc is None else acc + o
    return acc


def _pad_heads(blk):
    lo = _lane(blk.shape) < HEAD_DIM
    return jnp.where(lo, blk, 0.0), jnp.where(lo, pltpu.roll(blk, HEAD_DIM, 1), 0.0)


def _head_rms(x, gain):
    ms = jnp.sum(x * x, axis=-1, keepdims=True) * (1.0 / HEAD_DIM)
    return x * lax.rsqrt(ms + NORM_EPS) * gain


def _gqa_kernel(pq_ref, pall_ref, cq_ref, sq_ref, call_ref, sall_ref, qg_ref, kg_ref, o_ref, k_scr, v_scr):
    j = pl.program_id(1)

    @pl.when(j == 0)
    def _prep():
        kblk = pall_ref[0, :, 512:640].astype(F32)
        for g, kh in enumerate(_pad_heads(kblk)):
            kh = _rope(_head_rms(kh, kg_ref[...]), call_ref[...], sall_ref[...], 16)
            k_scr[:, 128 * g:128 * (g + 1)] = kh.astype(BF16)
        vblk = pall_ref[0, :, 640:768].astype(F32)
        vrot = pltpu.roll(vblk, HEAD_DIM, 1)
        lo = _lane(vblk.shape) < HEAD_DIM
        v_scr[:, 0:128] = jnp.where(lo, vblk, 0.0).astype(BF16)
        v_scr[:, 128:256] = jnp.where(lo, 0.0, vrot).astype(BF16)
        v_scr[:, 256:384] = jnp.where(lo, vrot, 0.0).astype(BF16)
        v_scr[:, 384:512] = jnp.where(lo, 0.0, vblk).astype(BF16)

    def attend(n_keys):
        for jp in range(N_HEADS // 2):
            blk = pq_ref[0, :, 128 * jp:128 * (jp + 1)].astype(F32)
            qs = [(_rope(_head_rms(qh, qg_ref[...]), cq_ref[...], sq_ref[...], 16) * (HEAD_DIM ** -0.5)).astype(BF16)
                  for qh in _pad_heads(blk)]
            g = (2 * jp) // (N_HEADS // A_KV_HEADS)
            o = _softmax_pv_pair(
                qs[0], qs[1],
                lambda hh: k_scr[0:n_keys, 128 * g:128 * (g + 1)],
                lambda hh: v_scr[0:n_keys, 128 * (2 * g + hh):128 * (2 * g + hh + 1)])
            o_ref[0, :, 128 * jp:128 * (jp + 1)] = o.astype(o_ref.dtype)

    @pl.when(j == 0)
    def _ctx():
        attend(CTX_LEN)

    @pl.when(j > 0)
    def _lat():
        attend(T_ALL)


def _gqa(p_a, cos, sin, q_gain, k_gain):
    n_b = p_a.shape[0]
    pad = lambda g: jnp.concatenate([g, jnp.zeros((HEAD_DIM,), F32)]).reshape(1, 128)
    tile = lambda b, j: (b, j, 0)
    whole = lambda b, j: (b, 0, 0)
    return pl.pallas_call(
        _gqa_kernel,
        grid=(n_b, N_QT),
        in_specs=[pl.BlockSpec((1, Q_TILE, A_IN), tile),
                  pl.BlockSpec((1, T_ALL, A_IN), whole),
                  pl.BlockSpec((Q_TILE, 128), lambda b, j: (j, 0)),
                  pl.BlockSpec((Q_TILE, 128), lambda b, j: (j, 0)),
                  pl.BlockSpec((T_ALL, 128), lambda b, j: (0, 0)),
                  pl.BlockSpec((T_ALL, 128), lambda b, j: (0, 0)),
                  pl.BlockSpec((1, 128), lambda b, j: (0, 0)),
                  pl.BlockSpec((1, 128), lambda b, j: (0, 0))],
        out_specs=pl.BlockSpec((1, Q_TILE, 512), tile),
        out_shape=jax.ShapeDtypeStruct((n_b, T_ALL, 512), BF16),
        scratch_shapes=[pltpu.VMEM((T_ALL, 256), BF16), pltpu.VMEM((T_ALL, 512), BF16)],
        compiler_params=_cparams(("parallel", "arbitrary")),
        name="gqa",
    )(p_a, p_a, cos, sin, cos, sin, pad(q_gain), pad(k_gain))


def _mla_kernel(pq_ref, pall_ref, cq_ref, sq_ref, call_ref, sall_ref, qg_ref, kvg_ref, wq_ref, wkv_ref,
                o_ref, k_scr, v_scr):
    j = pl.program_id(1)
    kw = N_HEADS * 128

    @pl.when(j == 0)
    def _prep():
        def body(i, carry):
            r0 = pl.multiple_of(i * Q_TILE, Q_TILE)
            rows = pl.ds(r0, Q_TILE)
            ckv = pall_ref[0, rows, B_Q_LORA:B_Q_LORA + B_KV_LORA].astype(F32)
            n = ckv * lax.rsqrt(jnp.mean(ckv * ckv, axis=-1, keepdims=True) + NORM_EPS) * kvg_ref[...]
            kv = _dot(n.astype(BF16), wkv_ref[...])
            kr = _rope(pall_ref[0, rows, 640:768].astype(F32), call_ref[rows, :], sall_ref[rows, :], 8)
            for h in range(N_HEADS):
                k_scr[rows, 128 * h:128 * (h + 1)] = (kv[:, 128 * h:128 * (h + 1)] + kr).astype(BF16)
            v_scr[rows, :] = kv[:, kw:].astype(BF16)
            return carry
        lax.fori_loop(0, N_QT, body, 0)

    def attend(n_keys):
        cq = pq_ref[0, :, 0:B_Q_LORA].astype(F32)
        n = cq * lax.rsqrt(jnp.mean(cq * cq, axis=-1, keepdims=True) + NORM_EPS) * qg_ref[...]
        q = _dot(n.astype(BF16), wq_ref[...])
        scale = (B_NOPE + B_ROPE) ** -0.5
        for jp in range(N_HEADS // 2):
            qs = [(_rope(q[:, 128 * h:128 * (h + 1)], cq_ref[...], sq_ref[...], 8) * scale).astype(BF16)
                  for h in (2 * jp, 2 * jp + 1)]
            o = _softmax_pv_pair(
                qs[0], qs[1],
                lambda hh: k_scr[0:n_keys, 128 * (2 * jp + hh):128 * (2 * jp + hh + 1)],
                lambda hh: v_scr[0:n_keys, 128 * (2 * jp + hh):128 * (2 * jp + hh + 1)])
            o_ref[0, :, 128 * jp:128 * (jp + 1)] = o.astype(o_ref.dtype)

    @pl.when(j == 0)
    def _ctx():
        attend(CTX_LEN)

    @pl.when(j > 0)
    def _lat():
        attend(T_ALL)


def _mla(p_b, cos, sin, q_gain, kv_gain, wq, wkv):
    n_b = p_b.shape[0]
    tile = lambda b, j: (b, j, 0)
    whole = lambda b, j: (b, 0, 0)
    const = lambda b, j: (0, 0)
    w = p_b.shape[-1]
    return pl.pallas_call(
        _mla_kernel,
        grid=(n_b, N_QT),
        in_specs=[pl.BlockSpec((1, Q_TILE, w), tile),
                  pl.BlockSpec((1, T_ALL, w), whole),
                  pl.BlockSpec((Q_TILE, 128), lambda b, j: (j, 0)),
                  pl.BlockSpec((Q_TILE, 128), lambda b, j: (j, 0)),
                  pl.BlockSpec((T_ALL, 128), const),
                  pl.BlockSpec((T_ALL, 128), const),
                  pl.BlockSpec((1, B_Q_LORA), const),
                  pl.BlockSpec((1, B_KV_LORA), const),
                  pl.BlockSpec(wq.shape, const),
                  pl.BlockSpec(wkv.shape, const)],
        out_specs=pl.BlockSpec((1, Q_TILE, 512), tile),
        out_shape=jax.ShapeDtypeStruct((n_b, T_ALL, 512), BF16),
        scratch_shapes=[pltpu.VMEM((T_ALL, N_HEADS * 128), BF16), pltpu.VMEM((T_ALL, N_HEADS * 128), BF16)],
        compiler_params=_cparams(("parallel", "arbitrary")),
        name="mla",
    )(p_b, p_b, cos, sin, cos, sin, q_gain.reshape(1, -1), kv_gain.reshape(1, -1), wq, wkv)


def _nat_kernel(pq_ref, pall_ref, bias_ref, o_ref, k_scr, v_scr):
    j = pl.program_id(1)

    @pl.when(j == 0)
    def _prep():
        for jb in range(N_HEADS // 2):
            kblk = pall_ref[0, :, 512 + 128 * jb:512 + 128 * (jb + 1)].astype(F32)
            for hh, kh in enumerate(_pad_heads(kblk)):
                h = 2 * jb + hh
                k_scr[:, 128 * h:128 * (h + 1)] = kh.astype(BF16)
            vblk = pall_ref[0, :, 1024 + 128 * jb:1024 + 128 * (jb + 1)]
            lo = _lane(vblk.shape) < HEAD_DIM
            zero = jnp.zeros_like(vblk)
            v_scr[:, 128 * (2 * jb):128 * (2 * jb + 1)] = jnp.where(lo, vblk, zero)
            v_scr[:, 128 * (2 * jb + 1):128 * (2 * jb + 2)] = jnp.where(lo, zero, vblk)

    def q_pair(jp):
        blk = pq_ref[0, :, 128 * jp:128 * (jp + 1)].astype(F32)
        return [(qh * (HEAD_DIM ** -0.5)).astype(BF16) for qh in _pad_heads(blk)]

    hs = lambda jp, hh: slice(128 * (2 * jp + hh), 128 * (2 * jp + hh + 1))

    @pl.when(j == 0)
    def _ctx():
        for jp in range(N_HEADS // 2):
            qs = q_pair(jp)
            o = _softmax_pv_pair(qs[0], qs[1],
                                 lambda hh: k_scr[0:CTX_LEN, hs(jp, hh)],
                                 lambda hh: v_scr[0:CTX_LEN, hs(jp, hh)])
            o_ref[0, :, 128 * jp:128 * (jp + 1)] = o.astype(o_ref.dtype)

    @pl.when(j > 0)
    def _lat():
        first_row = jnp.clip(NAT_QROWS * (j - 1) - NA_ROWS // 2, 0, N_ROWS - NAT_KROWS)
        start = pl.multiple_of(CTX_LEN + first_row * GRID_W, Q_TILE)
        win = pl.ds(start, NAT_KWIN)
        for jp in range(N_HEADS // 2):
            qs = q_pair(jp)
            o = _softmax_pv_pair(
                qs[0], qs[1],
                lambda hh: k_scr[win, hs(jp, hh)],
                lambda hh: v_scr[win, hs(jp, hh)],
                extra=(lambda hh: bias_ref[0, 2 * jp + hh],
                       lambda hh: k_scr[0:CTX_LEN, hs(jp, hh)],
                       lambda hh: v_scr[0:CTX_LEN, hs(jp, hh)]))
            o_ref[0, :, 128 * jp:128 * (jp + 1)] = o.astype(o_ref.dtype)


def _nat_bias_table(rel_bias):
    n_dr, n_dc = 2 * NA_ROWS - 1, 2 * NA_COLS - 1
    cols = np.arange(GRID_W)
    pick_col = (cols[None, None, :] - cols[None, :, None] + NA_COLS - 1 == np.arange(n_dc)[:, None, None])
    pick_row = np.zeros((3, NAT_QROWS, NAT_KROWS, n_dr), np.float32)
    valid = np.zeros((3, Q_TILE, NAT_KWIN), bool)
    for t, qb in enumerate((0, 1, N_ROWS // NAT_QROWS - 1)):
        first_row = int(np.clip(NAT_QROWS * qb - NA_ROWS // 2, 0, N_ROWS - NAT_KROWS))
        rq = NAT_QROWS * qb + np.arange(NAT_QROWS)
        rk = first_row + np.arange(NAT_KROWS)
        pick_row[t] = (rk[None, :, None] - rq[:, None, None] + NA_ROWS - 1 == np.arange(n_dr)[None, None, :])
        ql, kl = np.arange(Q_TILE), np.arange(NAT_KWIN)
        r, c = NAT_QROWS * qb + ql // GRID_W, ql % GRID_W
        kr, kc = first_row + kl // GRID_W, kl % GRID_W
        r0 = np.clip(r - NA_ROWS // 2, 0, N_ROWS - NA_ROWS)
        c0 = np.clip(c - NA_COLS // 2, 0, GRID_W - NA_COLS)
        valid[t] = ((kr[None, :] >= r0[:, None]) & (kr[None, :] < r0[:, None] + NA_ROWS)
                    & (kc[None, :] >= c0[:, None]) & (kc[None, :] < c0[:, None] + NA_COLS))
    by_col = jnp.einsum('hdc,cab->hdab', rel_bias.astype(F32), jnp.asarray(pick_col, F32), precision=lax.Precision.HIGHEST)
    tab = jnp.einsum('tqkd,hdab->thqakb', jnp.asarray(pick_row), by_col, precision=lax.Precision.HIGHEST)
    tab = tab.reshape(3, N_HEADS, Q_TILE, NAT_KWIN)
    return jnp.where(jnp.asarray(valid)[:, None], tab, NEG_BIG)


def _nat(p_d, bias_tab):
    n_b = p_d.shape[0]
    tile = lambda b, j: (b, j, 0)
    n_lat_tiles = N_QT - 1

    def bias_idx(b, j):
        qb = j - 1
        return (jnp.where(qb <= 0, 0, jnp.where(qb == n_lat_tiles - 1, 2, 1)), 0, 0, 0)

    return pl.pallas_call(
        _nat_kernel,
        grid=(n_b, N_QT),
        in_specs=[pl.BlockSpec((1, Q_TILE, D_IN), tile),
                  pl.BlockSpec((1, T_ALL, D_IN), lambda b, j: (b, 0, 0)),
                  pl.BlockSpec((1, N_HEADS, Q_TILE, NAT_KWIN), bias_idx)],
        out_specs=pl.BlockSpec((1, Q_TILE, 512), tile),
        out_shape=jax.ShapeDtypeStruct((n_b, T_ALL, 512), BF16),
        scratch_shapes=[pltpu.VMEM((T_ALL, N_HEADS * 128), BF16), pltpu.VMEM((T_ALL, N_HEADS * 128), BF16)],
        compiler_params=_cparams(("parallel", "arbitrary")),
        name="nat",
    )(p_d, p_d, bias_tab)


def _seg_sum(x, ones_bd):
    return _dot3_exact_rhs(x, ones_bd)


def _rwkv_prep_kernel(z_ref, zp_ref, zn_ref, mu_ref, w0_ref, a0_ref, kk_ref, ka_ref, rk_ref,
                      wdh_ref, wdl_ref, wah_ref, wal_ref, wgh_ref, wgl_ref, ones_ref,
                      r_out, v_out, kk_out, kd_out, lw_out, bd_out, bonus_out, g_out):
    j = pl.program_id(1)
    z = z_ref[0]
    tm = z.shape[0]
    row = lax.broadcasted_iota(jnp.int32, (tm, 1), 0)
    prev_row = jnp.where(j <= 1, 0.0, zp_ref[0, 7:8, :])
    next_row = jnp.where((j == 0) | (j == N_QT - 1), 0.0, zn_ref[0, 0:1, :])
    z_prev = jnp.where(row == 0, prev_row, pltpu.roll(z, 1, 0))
    z_next = jnp.where(row == tm - 1, next_row, pltpu.roll(z, tm - 1, 0))
    zs = z + (0.5 * (z_prev + z_next) - z) * mu_ref[...]

    r = zs[:, 0:512]
    k = zs[:, 512:1024]
    v = zs[:, 1024:1536]
    w_lo = zs[:, 1536:1664]
    a_lo = zs[:, 1664:1792]
    g_lo = zs[:, 1792:1920]
    ones_bd = ones_ref[...]

    def lora(x, wh_ref, wl_ref):
        xh, xl = _split_bf16(x)
        return _dot(xh, wh_ref[...]) + _dot(xh, wl_ref[...]) + _dot(xl, wh_ref[...])

    kkr = k * kk_ref[...]
    nrm = jnp.maximum(jnp.sqrt(_seg_sum(kkr * kkr, ones_bd)), 1e-12)
    kk = kkr / nrm
    dec = lora(jnp.tanh(w_lo), wdh_ref, wdl_ref)
    aaa = lora(a_lo, wah_ref, wal_ref)
    g = lora(_sigmoid(g_lo), wgh_ref, wgl_ref)
    r_out[0] = r
    v_out[0] = v
    kk_out[0] = kk
    g_out[0] = g
    ksum = None
    for d in range(2):
        u = -(w0_ref[d:d + 1, :] + dec[:, 512 * d:512 * (d + 1)])
        softplus = jnp.maximum(u, 0.0) + jnp.log(1.0 + jnp.exp(-jnp.abs(u)))
        logw = -softplus - 0.5
        lw_out[d, 0] = -jnp.exp(logw)
        a = _sigmoid(a0_ref[d:d + 1, :] + aaa[:, 512 * d:512 * (d + 1)])
        bd_out[d, 0] = kk * a
        kd = k * (1.0 + (a - 1.0) * ka_ref[...])
        kd_out[d, 0] = kd
        ksum = kd if ksum is None else ksum + kd
    bonus_out[0] = _seg_sum(r * ksum * rk_ref[...], ones_bd) * v


def _rwkv_prep(p_c, lp):
    n_b = p_c.shape[0]
    tile = lambda b, j: (b, j, 0)
    const = lambda b, j: (0, 0)
    blocks8 = T_ALL // 8
    tpb = Q_TILE // 8
    prev = lambda b, j: (b, jnp.maximum(j * tpb - 1, 0), 0)
    nxt = lambda b, j: (b, jnp.minimum((j + 1) * tpb, blocks8 - 1), 0)
    o3 = jax.ShapeDtypeStruct((n_b, T_ALL, C_W), F32)
    o4 = jax.ShapeDtypeStruct((2, n_b, T_ALL, C_W), F32)
    s3 = pl.BlockSpec((1, Q_TILE, C_W), tile)
    s4 = pl.BlockSpec((2, 1, Q_TILE, C_W), lambda b, j: (0, b, j, 0))
    small = [lp['c_mu'].reshape(1, C_IN), lp['c_w0'], lp['c_a0'], lp['c_k_k'].reshape(1, C_W),
             lp['c_k_a'].reshape(1, C_W), lp['c_r_k'].reshape(1, C_W)]
    bd2 = lambda w: jnp.concatenate(
        [jnp.concatenate([w[0], jnp.zeros_like(w[0])], 1), jnp.concatenate([jnp.zeros_like(w[1]), w[1]], 1)], 0)
    mats = []
    for w in (bd2(lp['c_w_decay']), bd2(lp['c_w_aaa']), lp['c_w_gate']):
        mats.extend(_split_bf16(w))
    ones_bd = jnp.asarray(np.kron(np.eye(N_HEADS), np.ones((HEAD_DIM, HEAD_DIM))), BF16)
    ins = small + mats + [ones_bd]
    return pl.pallas_call(
        _rwkv_prep_kernel,
        grid=(n_b, N_QT),
        in_specs=[pl.BlockSpec((1, Q_TILE, C_IN), tile),
                  pl.BlockSpec((1, 8, C_IN), prev),
                  pl.BlockSpec((1, 8, C_IN), nxt)] + [pl.BlockSpec(a.shape, const) for a in ins],
        out_specs=[s3, s3, s3, s4, s4, s4, s3, s3],
        out_shape=[o3, o3, o3, o4, o4, o4, o3, o3],
        compiler_params=_cparams(("parallel", "parallel")),
        name="rwkv_prep",
    )(p_c, p_c, p_c, *ins)


def _dot3_exact_rhs_lhs(m_bf16, a):
    a0 = a.astype(BF16)
    r1 = a - a0.astype(F32)
    a1 = r1.astype(BF16)
    a2 = (r1 - a1.astype(F32)).astype(BF16)
    return _dot(m_bf16, a0) + _dot(m_bf16, a1) + _dot(m_bf16, a2)


def _scan_masks(fwd):
    n = HG * CHUNK
    ri = lax.broadcasted_iota(jnp.int32, (n, n), 0)
    ci = lax.broadcasted_iota(jnp.int32, (n, n), 1)
    same = (ri // CHUNK) == (ci // CHUNK)
    pos = (lambda t: t % CHUNK) if fwd else (lambda t: CHUNK - 1 - t % CHUNK)
    tr, tc = pos(ri), pos(ci)
    strict = same & (tr > tc)
    incl = same & (tr >= tc)
    lower_left = [same & ((tr // (2 * h)) == (tc // (2 * h))) & (((tr // h) % 2) == 1) & (((tc // h) % 2) == 0)
                  for h in (1, 2, 4, 8, 16, 32)]
    t_r = pos(lax.broadcasted_iota(jnp.int32, (CHUNK, CHUNK), 0))
    t_c = pos(lax.broadcasted_iota(jnp.int32, (CHUNK, CHUNK), 1))
    cum_mat = jnp.where(t_r >= t_c, 1.0, 0.0).astype(BF16)
    return strict, incl, lower_left, cum_mat


def _rwkv_scan_kernel(rf_ref, vf_ref, kkf_ref, kdf_ref, lwf_ref, bdf_ref,
                      rb_ref, vb_ref, kkb_ref, kdb_ref, lwb_ref, bdb_ref, yf_ref, yb_ref, s_scr):
    j = pl.program_id(1)
    n = HG * CHUNK
    eye = (lax.broadcasted_iota(jnp.int32, (n, n), 0) == lax.broadcasted_iota(jnp.int32, (n, n), 1)).astype(F32)
    head_of_lane = lax.broadcasted_iota(jnp.int32, (CHUNK, HGW), 1) // HEAD_DIM
    masks = (_scan_masks(True), _scan_masks(False))
    dir_refs = ((rf_ref, vf_ref, kkf_ref, kdf_ref, lwf_ref, bdf_ref, yf_ref),
                (rb_ref, vb_ref, kkb_ref, kdb_ref, lwb_ref, bdb_ref, yb_ref))
    chunks_per_tile = Q_TILE // CHUNK

    def stack(x):
        return jnp.concatenate([jnp.where(head_of_lane == p, x, 0.0) for p in range(HG)], axis=0)

    @pl.when(j == 0)
    def _init():
        s_scr[...] = jnp.zeros_like(s_scr)

    def body(i, carry):
        ch = []
        for d in range(2):
            r_ref, v_ref, kk_ref, kd_ref, lw_ref, bd_ref, y_ref = dir_refs[d]
            c = i if d == 0 else chunks_per_tile - 1 - i
            rows = pl.ds(pl.multiple_of(c * CHUNK, CHUNK), CHUNK)
            lw = lw_ref[0, 0, rows, :]
            cum = _dot3_exact_rhs_lhs(masks[d][3], lw)
            total = jnp.sum(lw, axis=0, keepdims=True)
            e_pos, e_neg, e_prev, e_rest = jnp.exp(cum), jnp.exp(-cum), jnp.exp(cum - lw), jnp.exp(total - cum)
            p_end = jnp.exp(total)
            for g in range(C_W // HGW):
                ln = slice(g * HGW, (g + 1) * HGW)
                kk, bd, kd = kk_ref[0, rows, ln], bd_ref[0, 0, rows, ln], kd_ref[0, 0, rows, ln]
                rh4 = stack(r_ref[0, rows, ln] * e_pos[:, ln])
                ch.append(dict(
                    idx=2 * d + g, d=d, rows=rows, ln=ln, y_ref=y_ref, p_end=p_end[:, ln], rh4=rh4,
                    al4b=stack(kk * e_prev[:, ln]).astype(BF16), be4b=stack(bd * e_neg[:, ln]).astype(BF16),
                    ka4b=stack(kd * e_neg[:, ln]).astype(BF16), rh4b=rh4.astype(BF16),
                    bee4t=stack(bd * e_rest[:, ln]).T.astype(BF16),
                    kae4t=stack(kd * e_rest[:, ln]).T.astype(BF16),
                    v4=stack(v_ref[0, rows, ln]).astype(BF16)))

        def stage(fn):
            for c_ in ch:
                c_.update(fn(c_, *masks[c_['d']][:3]))

        stage(lambda c_, strict, incl, ll: dict(l_ab=jnp.where(strict, _dot_nt(c_['al4b'], c_['be4b']), 0.0)))
        stage(lambda c_, strict, incl, ll: dict(l_ak=jnp.where(strict, _dot_nt(c_['al4b'], c_['ka4b']), 0.0).astype(BF16)))
        stage(lambda c_, strict, incl, ll: dict(m_rb=jnp.where(incl, _dot_nt(c_['rh4b'], c_['be4b']), 0.0).astype(BF16)))
        stage(lambda c_, strict, incl, ll: dict(m_rk=jnp.where(incl, _dot_nt(c_['rh4b'], c_['ka4b']), 0.0).astype(BF16)))
        stage(lambda c_, strict, incl, ll: dict(x=eye - jnp.where(ll[0], c_['l_ab'], 0.0)))
        for level in range(1, 6):
            stage(lambda c_, strict, incl, ll: dict(
                xb=c_['x'].astype(BF16),
                t=_dot(jnp.where(ll[level], c_['l_ab'], 0.0).astype(BF16), c_['x'].astype(BF16)).astype(BF16)))
            stage(lambda c_, strict, incl, ll: dict(x=c_['x'] - _dot(c_['xb'], c_['t'])))
        stage(lambda c_, *_: dict(tb=c_['x'].astype(BF16)))
        stage(lambda c_, *_: dict(w4b=_dot(c_['tb'], c_['al4b']).astype(BF16)))
        stage(lambda c_, *_: dict(lv=_dot(c_['l_ak'], c_['v4']).astype(BF16)))
        stage(lambda c_, *_: dict(u04b=_dot(c_['tb'], c_['lv']).astype(BF16)))
        stage(lambda c_, *_: dict(q4b=(c_['rh4'] - _dot(c_['m_rb'], c_['w4b'])).astype(BF16)))
        stage(lambda c_, *_: dict(y04=_dot(c_['m_rk'], c_['v4']) - _dot(c_['m_rb'], c_['u04b'])))
        stage(lambda c_, *_: dict(g4=eye * c_['p_end'] - _dot(c_['bee4t'], c_['w4b'])))
        stage(lambda c_, *_: dict(h4=_dot(c_['kae4t'], c_['v4']) - _dot(c_['bee4t'], c_['u04b'])))
        stage(lambda c_, *_: dict(s=s_scr[c_['idx']]))
        stage(lambda c_, *_: dict(y4=_dot(c_['q4b'], c_['s'].astype(BF16)) + c_['y04']))
        for c_ in ch:
            s_scr[c_['idx']] = _dot3(c_['g4'], c_['s']) + c_['h4']
        for c_ in ch:
            y4 = c_['y4']
            y = y4[0:CHUNK]
            for p in range(1, HG):
                y = y + y4[p * CHUNK:(p + 1) * CHUNK]
            c_['y_ref'][0, c_['rows'], c_['ln']] = y
        return carry

    lax.fori_loop(0, chunks_per_tile, body, 0)


def _rwkv_scan(r, v, kk, kd, lw, bd):
    n_b = r.shape[0]
    bwd_tile = lambda j: jnp.where(j == 0, 0, N_QT - j)
    f3 = pl.BlockSpec((1, Q_TILE, C_W), lambda b, j: (b, j, 0))
    f4 = pl.BlockSpec((1, 1, Q_TILE, C_W), lambda b, j: (0, b, j, 0))
    b3 = pl.BlockSpec((1, Q_TILE, C_W), lambda b, j: (b, bwd_tile(j), 0))
    b4 = pl.BlockSpec((1, 1, Q_TILE, C_W), lambda b, j: (1, b, bwd_tile(j), 0))
    y_shape = jax.ShapeDtypeStruct((n_b, T_ALL, C_W), F32)
    return pl.pallas_call(
        _rwkv_scan_kernel,
        grid=(n_b, N_QT),
        in_specs=[f3, f3, f3, f4, f4, f4, b3, b3, b3, b4, b4, b4],
        out_specs=[f3, b3],
        out_shape=[y_shape, y_shape],
        scratch_shapes=[pltpu.VMEM((2 * (C_W // HGW), HGW, HGW), F32)],
        compiler_params=_cparams(("parallel", "arbitrary")),
        name="rwkv_scan",
    )(r, v, kk, kd, lw, bd, r, v, kk, kd, lw, bd)


def _rwkv_out_kernel(yf_ref, yb_ref, bonus_ref, g_ref, gw_ref, gb_ref, ones_ref, o_ref):
    y = yf_ref[0] + yb_ref[0]
    ones_bd = ones_ref[...]
    mean = _seg_sum(y, ones_bd) * (1.0 / HEAD_DIM)
    yc = y - mean
    var = _seg_sum(yc * yc, ones_bd) * (1.0 / HEAD_DIM)
    yn = yc * lax.rsqrt(var + C_GN_EPS) * gw_ref[...] + gb_ref[...]
    o_ref[0] = ((yn + bonus_ref[0]) * g_ref[0]).astype(o_ref.dtype)


def _rwkv_out(y_f, y_b, bonus, g, gn_w, gn_b):
    n_b = bonus.shape[0]
    tm = 768
    tile = lambda b, j: (b, j, 0)
    const = lambda b, j: (0, 0)
    ones_bd = jnp.asarray(np.kron(np.eye(N_HEADS), np.ones((HEAD_DIM, HEAD_DIM))), BF16)
    return pl.pallas_call(
        _rwkv_out_kernel,
        grid=(n_b, T_ALL // tm),
        in_specs=[pl.BlockSpec((1, tm, C_W), tile), pl.BlockSpec((1, tm, C_W), tile),
                  pl.BlockSpec((1, tm, C_W), tile), pl.BlockSpec((1, tm, C_W), tile),
                  pl.BlockSpec((1, C_W), const), pl.BlockSpec((1, C_W), const),
                  pl.BlockSpec((C_W, C_W), const)],
        out_specs=pl.BlockSpec((1, tm, C_W), tile),
        out_shape=jax.ShapeDtypeStruct((n_b, T_ALL, C_W), BF16),
        compiler_params=_cparams(("parallel", "parallel")),
        name="rwkv_out",
    )(y_f, y_b, bonus, g, gn_w.reshape(1, C_W), gn_b.reshape(1, C_W), ones_bd)


def _merge_kernel(x_ref, oa_ref, ob_ref, oc_ref, od_ref, gate_ref, wb_ref, wo_ref, mb_ref, mc_ref, out_ref, *, tm):
    is_ctx = _is_ctx_rows(tm, 1)
    y = None
    for i, o_ref in enumerate((oa_ref, ob_ref, oc_ref, od_ref)):
        z = _dot(o_ref[0], wb_ref[i])
        sg = _sigmoid(gate_ref[0, :, i * D_MODEL:(i + 1) * D_MODEL].astype(F32))
        y = sg * z if y is None else y + sg * z
    z = _dot(y.astype(BF16), wo_ref[...])
    out_ref[0] = x_ref[0] + _mod_vec(mb_ref, mc_ref, 2, is_ctx) * z


def _merge(x, outs, gates, wb, wo, modl):
    n_b = x.shape[0]
    tm = 768
    tile = lambda b, j: (b, j, 0)
    mb, mc = _mod_specs(n_b, 2)
    o_spec = pl.BlockSpec((1, tm, 512), tile)
    return pl.pallas_call(
        functools.partial(_merge_kernel, tm=tm),
        grid=(n_b, T_ALL // tm),
        in_specs=[pl.BlockSpec((1, tm, D_MODEL), tile), o_spec, o_spec, o_spec, o_spec,
                  pl.BlockSpec((1, tm, GATE_IN), tile),
                  pl.BlockSpec(wb.shape, lambda b, j: (0, 0, 0)),
                  pl.BlockSpec(wo.shape, lambda b, j: (0, 0)), mb, mc],
        out_specs=pl.BlockSpec((1, tm, D_MODEL), tile),
        out_shape=jax.ShapeDtypeStruct(x.shape, F32),
        compiler_params=_cparams(("parallel", "parallel")),
        name="merge",
    )(x, *outs, gates, wb, wo, modl, modl)


def _mlp_kernel(x_ref, g_ref, mb_ref, mc_ref, w1_ref, w2_ref, out_ref, h_scr, acc_scr, *, tm, n_f):
    f = pl.program_id(2)
    is_ctx = _is_ctx_rows(tm, 1)

    @pl.when(f == 0)
    def _init():
        h = _norm_mod(x_ref[0], g_ref[...], _mod_vec(mb_ref, mc_ref, 3, is_ctx), _mod_vec(mb_ref, mc_ref, 4, is_ctx))
        h_scr[...] = h.astype(BF16)
        acc_scr[...] = jnp.zeros_like(acc_scr)

    a = jnp.square(jnp.maximum(_dot(h_scr[...], w1_ref[...]), 0.0))
    acc_scr[...] += _dot(a.astype(BF16), w2_ref[...])

    @pl.when(f == n_f - 1)
    def _fin():
        out_ref[0] = x_ref[0] + _mod_vec(mb_ref, mc_ref, 5, is_ctx) * acc_scr[...]


def _mlp(x, g, w1, w2, modl):
    n_b = x.shape[0]
    tm, tf = 768, 512
    n_f = D_FF // tf
    tile = lambda b, j, f: (b, j, 0)
    mb, mc = _mod_specs(n_b, 3)
    return pl.pallas_call(
        functools.partial(_mlp_kernel, tm=tm, n_f=n_f),
        grid=(n_b, T_ALL // tm, n_f),
        in_specs=[pl.BlockSpec((1, tm, D_MODEL), tile),
                  pl.BlockSpec((1, D_MODEL), lambda b, j, f: (0, 0)), mb, mc,
                  pl.BlockSpec((D_MODEL, tf), lambda b, j, f: (0, f)),
                  pl.BlockSpec((tf, D_MODEL), lambda b, j, f: (f, 0))],
        out_specs=pl.BlockSpec((1, tm, D_MODEL), tile),
        out_shape=jax.ShapeDtypeStruct(x.shape, F32),
        scratch_shapes=[pltpu.VMEM((tm, D_MODEL), BF16), pltpu.VMEM((tm, D_MODEL), F32)],
        compiler_params=_cparams(("parallel", "parallel", "arbitrary")),
        name="mlp",
    )(x, g.reshape(1, D_MODEL), modl, modl, w1, w2)


def _final_kernel(x_ref, g_ref, o_ref):
    x = x_ref[0]
    o_ref[0] = x * lax.rsqrt(jnp.mean(x * x, axis=-1, keepdims=True) + NORM_EPS) * g_ref[...]


def _final_norm(x, g):
    n_b = x.shape[0]
    tm = Q_TILE
    return pl.pallas_call(
        _final_kernel,
        grid=(n_b, SEQ // tm),
        in_specs=[pl.BlockSpec((1, tm, D_MODEL), lambda b, j: (b, j + CTX_LEN // tm, 0)),
                  pl.BlockSpec((1, D_MODEL), lambda b, j: (0, 0))],
        out_specs=pl.BlockSpec((1, tm, D_MODEL), lambda b, j: (b, j, 0)),
        out_shape=jax.ShapeDtypeStruct((n_b, SEQ, D_MODEL), F32),
        compiler_params=_cparams(("parallel", "parallel")),
        name="final_norm",
    )(x, g.reshape(1, D_MODEL))


def _rope_tables(half, lane0):
    t = np.arange(SEQ)
    inv = ROPE_THETA ** (-np.arange(half, dtype=np.float64) / half)
    cos = np.ones((T_ALL, V7X_LANES), np.float64)
    sin = np.zeros((T_ALL, V7X_LANES), np.float64)
    for part, pos in enumerate((t // GRID_W, t % GRID_W)):
        ang = pos[:, None].astype(np.float64) * inv[None, :]
        ang = ang.astype(np.float32).astype(np.float64)
        base = lane0 + 2 * half * part
        cos[CTX_LEN:, base:base + half] = np.cos(ang)
        cos[CTX_LEN:, base + half:base + 2 * half] = np.cos(ang)
        sin[CTX_LEN:, base:base + half] = -np.sin(ang)
        sin[CTX_LEN:, base + half:base + 2 * half] = np.sin(ang)
    return jnp.asarray(cos, F32), jnp.asarray(sin, F32)


def _layer_weights(l, w_in, b_w_q_up, b_w_kv_up, w_branch, w_out, w_mlp1, w_mlp2):
    wi = w_in[l]
    o_b = A_IN
    o_c = o_b + B_IN
    o_d = o_c + C_IN
    o_g = o_d + D_IN
    w_a = wi[:, :o_b]
    wb_raw = wi[:, o_b:o_c]
    z = lambda n: jnp.zeros((D_MODEL, n), F32)
    w_b = jnp.concatenate([wb_raw[:, :B_Q_LORA + B_KV_LORA], z(64), wb_raw[:, B_Q_LORA + B_KV_LORA:], z(32)], 1)
    wq = b_w_q_up[l].reshape(B_Q_LORA, N_HEADS, B_NOPE + B_ROPE)
    wq = jnp.concatenate([wq, jnp.zeros((B_Q_LORA, N_HEADS, 128 - B_NOPE - B_ROPE), F32)], -1)
    wkv = b_w_kv_up[l].reshape(B_KV_LORA, N_HEADS, 2 * HEAD_DIM)
    zk = jnp.zeros((B_KV_LORA, N_HEADS, HEAD_DIM), F32)
    wk = jnp.concatenate([wkv[:, :, :B_NOPE], zk], -1)
    even = (jnp.arange(N_HEADS) % 2 == 0)[None, :, None]
    wv = jnp.concatenate([jnp.where(even, wkv[:, :, B_NOPE:], 0.0), jnp.where(even, 0.0, wkv[:, :, B_NOPE:])], -1)
    return dict(
        w_a=w_a.astype(BF16), w_b=w_b.astype(BF16), w_c=wi[:, o_c:o_d].astype(BF16),
        w_d=wi[:, o_d:o_g].astype(BF16), w_g=wi[:, o_g:].astype(BF16),
        wq=wq.reshape(B_Q_LORA, N_HEADS * 128).astype(BF16),
        wkv=jnp.concatenate([wk.reshape(B_KV_LORA, -1), wv.reshape(B_KV_LORA, -1)], 1).astype(BF16),
        w_branch=w_branch[l].astype(BF16), w_out=w_out[l].astype(BF16),
        w_mlp1=w_mlp1[l].astype(BF16), w_mlp2=w_mlp2[l].astype(BF16))


def kernel(x, c, ctx, c_ctx, w_ada, b_ada, g_norm1, g_norm2, w_in, a_q_gain, a_k_gain, b_q_gain, b_kv_gain,
           b_w_q_up, b_w_kv_up, c_mu, c_w0, c_w_decay, c_a0, c_w_aaa, c_w_gate, c_k_k, c_k_a, c_r_k,
           c_gn_w, c_gn_b, d_rel_bias, w_branch, w_out, w_mlp1, w_mlp2, g_final):
    n_b = x.shape[0]
    assert x.shape[1:] == (SEQ, D_MODEL) and ctx.shape[1:] == (CTX_LEN, D_MODEL)
    m = n_b * T_ALL
    mod_rows = ((n_b + 1 + 7) // 8) * 8
    cc = jnp.concatenate([c, c_ctx[None, :], jnp.zeros((mod_rows - n_b - 1, D_MODEL), F32)], 0)
    mod_all = _ada(cc, w_ada, b_ada)
    cos_a, sin_a = _rope_tables(16, 0)
    cos_b, sin_b = _rope_tables(8, B_NOPE)
    xs = jnp.concatenate([ctx, x], axis=1)
    for l in range(DEPTH):
        lw = _layer_weights(l, w_in, b_w_q_up, b_w_kv_up, w_branch, w_out, w_mlp1, w_mlp2)
        lp = dict(c_mu=c_mu[l], c_w0=c_w0[l], c_w_decay=c_w_decay[l], c_a0=c_a0[l], c_w_aaa=c_w_aaa[l],
                  c_w_gate=c_w_gate[l], c_k_k=c_k_k[l], c_k_a=c_k_a[l], c_r_k=c_r_k[l])
        modl = mod_all[l].reshape(mod_rows, 1, 6 * D_MODEL)
        h = _norm1(xs, g_norm1[l], modl).reshape(m, D_MODEL)
        p_a = _mm(h, lw['w_a'], BF16, 768, A_IN, "w_in_a").reshape(n_b, T_ALL, A_IN)
        p_b = _mm(h, lw['w_b'], BF16, 768, 768, "w_in_b").reshape(n_b, T_ALL, 768)
        p_c = _mm(h, lw['w_c'], F32, 768, 640, "w_in_c").reshape(n_b, T_ALL, C_IN)
        p_d = _mm(h, lw['w_d'], BF16, 768, 768, "w_in_d").reshape(n_b, T_ALL, D_IN)
        p_g = _mm(h, lw['w_g'], BF16, 768, 1024, "w_in_g").reshape(n_b, T_ALL, GATE_IN)
        o_a = _gqa(p_a, cos_a, sin_a, a_q_gain[l], a_k_gain[l])
        o_b = _mla(p_b, cos_b, sin_b, b_q_gain[l], b_kv_gain[l], lw['wq'], lw['wkv'])
        r, v, kk, kd, lwd, bd, bonus, g = _rwkv_prep(p_c, lp)
        y_f, y_b = _rwkv_scan(r, v, kk, kd, lwd, bd)
        o_c = _rwkv_out(y_f, y_b, bonus, g, c_gn_w[l], c_gn_b[l])
        o_d = _nat(p_d, _nat_bias_table(d_rel_bias[l]))
        xs = _merge(xs, (o_a, o_b, o_c, o_d), p_g, lw['w_branch'], lw['w_out'], modl)
        xs = _mlp(xs, g_norm2[l], lw['w_mlp1'], lw['w_mlp2'], modl)
    return _final_norm(xs, g_final)
```

```python
import functools

import numpy as np
import jax
import jax.numpy as jnp
from jax import lax
from jax.experimental import pallas as pl
from jax.experimental.pallas import tpu as pltpu

F32 = jnp.float32
BF16 = jnp.bfloat16

D_MODEL = 1024
SEQ = 2048
DEPTH = 2
GRID_W = 64
N_ROWS = SEQ // GRID_W
CTX_LEN = 256
T_ALL = CTX_LEN + SEQ
HEAD_DIM = 64
ROPE_THETA = 10000.0
NORM_EPS = 1e-6
N_HEADS = 8
A_KV_HEADS = 2
B_Q_LORA = 384
B_KV_LORA = 256
B_NOPE = 64
B_ROPE = 32
C_W = 512
C_GN_EPS = 64e-5
NA_ROWS = 8
NA_COLS = 16
D_FF = 4 * D_MODEL
A_IN = 768
B_IN = 672
C_IN = 1920
D_IN = 1536
GATE_IN = 4096

V7X_LANES = 128
V7X_VMEM_LIMIT = 56 * 1024 * 1024

Q_TILE = 256
N_QT = T_ALL // Q_TILE
NAT_QROWS = Q_TILE // GRID_W
NAT_KROWS = 12
NAT_KWIN = NAT_KROWS * GRID_W
CHUNK = 64
N_CHUNK = T_ALL // CHUNK
N_CTX_CHUNK = CTX_LEN // CHUNK
HG = 4
HGW = HG * HEAD_DIM
NEG_BIG = -1e30


def _cparams(sem, vmem=V7X_VMEM_LIMIT):
    return pltpu.CompilerParams(dimension_semantics=sem, vmem_limit_bytes=vmem)


def _split_bf16(a):
    hi = a.astype(BF16)
    lo = (a - hi.astype(F32)).astype(BF16)
    return hi, lo


def _dot(a, b):
    return jnp.dot(a, b, preferred_element_type=F32)


def _dot_nt(a, b):
    return lax.dot_general(a, b, (((1,), (1,)), ((), ())), preferred_element_type=F32)


def _dot3(a, b):
    ah, al = _split_bf16(a)
    bh, bl = _split_bf16(b)
    return _dot(ah, bh) + _dot(ah, bl) + _dot(al, bh)


def _dot3_exact_rhs(a, b_bf16):
    a0 = a.astype(BF16)
    r1 = a - a0.astype(F32)
    a1 = r1.astype(BF16)
    a2 = (r1 - a1.astype(F32)).astype(BF16)
    return _dot(a0, b_bf16) + _dot(a1, b_bf16) + _dot(a2, b_bf16)


def _sigmoid(x):
    return 1.0 / (1.0 + jnp.exp(-x))


def _lane(shape):
    return lax.broadcasted_iota(jnp.int32, shape, len(shape) - 1)


def _rope(x, cos, sin, half):
    n = x.shape[-1]
    lo = (_lane(x.shape) % (2 * half)) < half
    partner = jnp.where(lo, pltpu.roll(x, n - half, 1), pltpu.roll(x, half, 1))
    return x * cos + partner * sin


def _mod_vec(mb_ref, mc_ref, k, is_ctx):
    lat = mb_ref[0, :, k * D_MODEL:(k + 1) * D_MODEL]
    ctx = mc_ref[0, :, k * D_MODEL:(k + 1) * D_MODEL]
    return jnp.where(is_ctx, ctx, lat)


def _is_ctx_rows(tm, tile_axis):
    row = pl.program_id(tile_axis) * tm + lax.broadcasted_iota(jnp.int32, (tm, 1), 0)
    return row < CTX_LEN


def _norm_mod(x, g, shift, scale):
    y = x * lax.rsqrt(jnp.mean(x * x, axis=-1, keepdims=True) + NORM_EPS) * g
    return y * (1.0 + scale) + shift


def _ada_kernel(c_ref, w_ref, b_ref, o_ref):
    c = c_ref[...]
    s = c * _sigmoid(c)
    o_ref[0] = _dot3(s, w_ref[0]) + b_ref[0]


def _ada(cc, w_ada, b_ada):
    n_l, _, n_out = w_ada.shape
    tn = 1536
    rows = cc.shape[0]
    return pl.pallas_call(
        _ada_kernel,
        grid=(n_l, n_out // tn),
        in_specs=[pl.BlockSpec((rows, D_MODEL), lambda l, j: (0, 0)),
                  pl.BlockSpec((1, D_MODEL, tn), lambda l, j: (l, 0, j)),
                  pl.BlockSpec((1, 1, tn), lambda l, j: (l, 0, j))],
        out_specs=pl.BlockSpec((1, rows, tn), lambda l, j: (l, 0, j)),
        out_shape=jax.ShapeDtypeStruct((n_l, rows, n_out), F32),
        compiler_params=_cparams(("arbitrary", "arbitrary")),
        name="ada",
    )(cc, w_ada, b_ada.reshape(n_l, 1, n_out))


def _mod_specs(n_b, grid_rank):
    if grid_rank == 2:
        return (pl.BlockSpec((1, 1, 6 * D_MODEL), lambda b, j: (b, 0, 0)),
                pl.BlockSpec((1, 1, 6 * D_MODEL), lambda b, j: (n_b, 0, 0)))
    return (pl.BlockSpec((1, 1, 6 * D_MODEL), lambda b, j, f: (b, 0, 0)),
            pl.BlockSpec((1, 1, 6 * D_MODEL), lambda b, j, f: (n_b, 0, 0)))


def _norm_kernel(x_ref, g_ref, mb_ref, mc_ref, h_ref, *, tm):
    is_ctx = _is_ctx_rows(tm, 1)
    h = _norm_mod(x_ref[0], g_ref[...], _mod_vec(mb_ref, mc_ref, 0, is_ctx), _mod_vec(mb_ref, mc_ref, 1, is_ctx))
    h_ref[0] = h.astype(h_ref.dtype)


def _norm1(x, g, modl):
    n_b = x.shape[0]
    tm = 768
    mb, mc = _mod_specs(n_b, 2)
    return pl.pallas_call(
        functools.partial(_norm_kernel, tm=tm),
        grid=(n_b, T_ALL // tm),
        in_specs=[pl.BlockSpec((1, tm, D_MODEL), lambda b, j: (b, j, 0)),
                  pl.BlockSpec((1, D_MODEL), lambda b, j: (0, 0)), mb, mc],
        out_specs=pl.BlockSpec((1, tm, D_MODEL), lambda b, j: (b, j, 0)),
        out_shape=jax.ShapeDtypeStruct(x.shape, BF16),
        compiler_params=_cparams(("parallel", "parallel")),
        name="norm1",
    )(x, g.reshape(1, D_MODEL), modl, modl)


def _mm_kernel(a_ref, w_ref, o_ref):
    o_ref[...] = _dot(a_ref[...], w_ref[...]).astype(o_ref.dtype)


def _mm(a, w, out_dtype, tm, tn, name):
    m, k = a.shape
    n = w.shape[1]
    return pl.pallas_call(
        _mm_kernel,
        grid=(m // tm, n // tn),
        in_specs=[pl.BlockSpec((tm, k), lambda i, j: (i, 0)),
                  pl.BlockSpec((k, tn), lambda i, j: (0, j))],
        out_specs=pl.BlockSpec((tm, tn), lambda i, j: (i, j)),
        out_shape=jax.ShapeDtypeStruct((m, n), out_dtype),
        compiler_params=_cparams(("parallel", "arbitrary")),
        name=name,
    )(a, w)


LOG2E = float(np.log2(np.e))
SUM_LANE = (HEAD_DIM, 0)


def _with_ones_lane(vblk, hh):
    lane = _lane(vblk.shape)
    keep = (lane < HEAD_DIM) if hh == 0 else (lane >= HEAD_DIM)
    return jnp.where(keep, vblk, jnp.where(lane == SUM_LANE[hh], 1.0, 0.0).astype(vblk.dtype))


def _softmax_pv_pair(q_even, q_odd, k_of, v_of, extra=None):
    outs = []
    for hh, q in enumerate((q_even, q_odd)):
        s = _dot_nt(q, k_of(hh))
        m = jnp.max(s, axis=-1, keepdims=True)
        if extra is not None:
            bias, k2_of, v2_of = extra
            s = s + bias(hh)
            s2 = _dot_nt(q, k2_of(hh))
            m = jnp.maximum(jnp.max(s, axis=-1, keepdims=True), jnp.max(s2, axis=-1, keepdims=True))
        o = _dot(jnp.exp2(s - m).astype(BF16), v_of(hh))
        if extra is not None:
            o = o + _dot(jnp.exp2(s2 - m).astype(BF16), v2_of(hh))
        outs.append(o / o[:, SUM_LANE[hh]:SUM_LANE[hh] + 1])
    return jnp.where(_lane(outs[0].shape) < HEAD_DIM, outs[0], outs[1])


def _pad_heads(blk):
    lo = _lane(blk.shape) < HEAD_DIM
    return jnp.where(lo, blk, 0.0), jnp.where(lo, pltpu.roll(blk, HEAD_DIM, 1), 0.0)


def _head_rms(x, gain):
    ms = jnp.sum(x * x, axis=-1, keepdims=True) * (1.0 / HEAD_DIM)
    return x * lax.rsqrt(ms + NORM_EPS) * gain


def _gqa_kernel(pq_ref, pall_ref, cq_ref, sq_ref, call_ref, sall_ref, qg_ref, kg_ref, o_ref, k_scr, v_scr):
    j = pl.program_id(1)

    @pl.when(j == 0)
    def _prep():
        kblk = pall_ref[0, :, 512:640].astype(F32)
        for g, kh in enumerate(_pad_heads(kblk)):
            kh = _rope(_head_rms(kh, kg_ref[...]), call_ref[...], sall_ref[...], 16)
            k_scr[:, 128 * g:128 * (g + 1)] = kh.astype(BF16)
        vblk = pall_ref[0, :, 640:768].astype(F32)
        vrot = pltpu.roll(vblk, HEAD_DIM, 1)
        for i, src in enumerate((vblk, vrot, vrot, vblk)):
            v_scr[:, 128 * i:128 * (i + 1)] = _with_ones_lane(src, i % 2).astype(BF16)

    def attend(n_keys):
        for jp in range(N_HEADS // 2):
            blk = pq_ref[0, :, 128 * jp:128 * (jp + 1)].astype(F32)
            qs = [(_rope(_head_rms(qh, qg_ref[...]), cq_ref[...], sq_ref[...], 16)
                   * (HEAD_DIM ** -0.5 * LOG2E)).astype(BF16) for qh in _pad_heads(blk)]
            g = (2 * jp) // (N_HEADS // A_KV_HEADS)
            o = _softmax_pv_pair(
                qs[0], qs[1],
                lambda hh: k_scr[0:n_keys, 128 * g:128 * (g + 1)],
                lambda hh: v_scr[0:n_keys, 128 * (2 * g + hh):128 * (2 * g + hh + 1)])
            o_ref[0, :, 128 * jp:128 * (jp + 1)] = o.astype(o_ref.dtype)

    @pl.when(j == 0)
    def _ctx():
        attend(CTX_LEN)

    @pl.when(j > 0)
    def _lat():
        attend(T_ALL)


def _gqa(p_a, cos, sin, q_gain, k_gain):
    n_b = p_a.shape[0]
    pad = lambda g: jnp.concatenate([g, jnp.zeros((HEAD_DIM,), F32)]).reshape(1, 128)
    tile = lambda b, j: (b, j, 0)
    whole = lambda b, j: (b, 0, 0)
    return pl.pallas_call(
        _gqa_kernel,
        grid=(n_b, N_QT),
        in_specs=[pl.BlockSpec((1, Q_TILE, A_IN), tile),
                  pl.BlockSpec((1, T_ALL, A_IN), whole),
                  pl.BlockSpec((Q_TILE, 128), lambda b, j: (j, 0)),
                  pl.BlockSpec((Q_TILE, 128), lambda b, j: (j, 0)),
                  pl.BlockSpec((T_ALL, 128), lambda b, j: (0, 0)),
                  pl.BlockSpec((T_ALL, 128), lambda b, j: (0, 0)),
                  pl.BlockSpec((1, 128), lambda b, j: (0, 0)),
                  pl.BlockSpec((1, 128), lambda b, j: (0, 0))],
        out_specs=pl.BlockSpec((1, Q_TILE, 512), tile),
        out_shape=jax.ShapeDtypeStruct((n_b, T_ALL, 512), BF16),
        scratch_shapes=[pltpu.VMEM((T_ALL, 256), BF16), pltpu.VMEM((T_ALL, 512), BF16)],
        compiler_params=_cparams(("parallel", "arbitrary")),
        name="gqa",
    )(p_a, p_a, cos, sin, cos, sin, pad(q_gain), pad(k_gain))


def _mla_kernel(pq_ref, pall_ref, cq_ref, sq_ref, call_ref, sall_ref, qg_ref, kvg_ref, wq_ref, wkv_ref,
                o_ref, k_scr, v_scr):
    j = pl.program_id(1)
    kw = N_HEADS * 128

    @pl.when(j == 0)
    def _prep():
        def body(i, carry):
            r0 = pl.multiple_of(i * Q_TILE, Q_TILE)
            rows = pl.ds(r0, Q_TILE)
            ckv = pall_ref[0, rows, B_Q_LORA:B_Q_LORA + B_KV_LORA].astype(F32)
            n = ckv * lax.rsqrt(jnp.mean(ckv * ckv, axis=-1, keepdims=True) + NORM_EPS) * kvg_ref[...]
            kv = _dot(n.astype(BF16), wkv_ref[...])
            kr = _rope(pall_ref[0, rows, 640:768].astype(F32), call_ref[rows, :], sall_ref[rows, :], 8)
            for h in range(N_HEADS):
                k_scr[rows, 128 * h:128 * (h + 1)] = (kv[:, 128 * h:128 * (h + 1)] + kr).astype(BF16)
            for h in range(N_HEADS):
                v_scr[rows, 128 * h:128 * (h + 1)] = _with_ones_lane(kv[:, kw + 128 * h:kw + 128 * (h + 1)], h % 2).astype(BF16)
            return carry
        lax.fori_loop(0, N_QT, body, 0)

    def attend(n_keys):
        cq = pq_ref[0, :, 0:B_Q_LORA].astype(F32)
        n = cq * lax.rsqrt(jnp.mean(cq * cq, axis=-1, keepdims=True) + NORM_EPS) * qg_ref[...]
        q = _dot(n.astype(BF16), wq_ref[...])
        scale = (B_NOPE + B_ROPE) ** -0.5 * LOG2E
        for jp in range(N_HEADS // 2):
            qs = [(_rope(q[:, 128 * h:128 * (h + 1)], cq_ref[...], sq_ref[...], 8) * scale).astype(BF16)
                  for h in (2 * jp, 2 * jp + 1)]
            o = _softmax_pv_pair(
                qs[0], qs[1],
                lambda hh: k_scr[0:n_keys, 128 * (2 * jp + hh):128 * (2 * jp + hh + 1)],
                lambda hh: v_scr[0:n_keys, 128 * (2 * jp + hh):128 * (2 * jp + hh + 1)])
            o_ref[0, :, 128 * jp:128 * (jp + 1)] = o.astype(o_ref.dtype)

    @pl.when(j == 0)
    def _ctx():
        attend(CTX_LEN)

    @pl.when(j > 0)
    def _lat():
        attend(T_ALL)


def _mla(p_b, cos, sin, q_gain, kv_gain, wq, wkv):
    n_b = p_b.shape[0]
    tile = lambda b, j: (b, j, 0)
    whole = lambda b, j: (b, 0, 0)
    const = lambda b, j: (0, 0)
    w = p_b.shape[-1]
    return pl.pallas_call(
        _mla_kernel,
        grid=(n_b, N_QT),
        in_specs=[pl.BlockSpec((1, Q_TILE, w), tile),
                  pl.BlockSpec((1, T_ALL, w), whole),
                  pl.BlockSpec((Q_TILE, 128), lambda b, j: (j, 0)),
                  pl.BlockSpec((Q_TILE, 128), lambda b, j: (j, 0)),
                  pl.BlockSpec((T_ALL, 128), const),
                  pl.BlockSpec((T_ALL, 128), const),
                  pl.BlockSpec((1, B_Q_LORA), const),
                  pl.BlockSpec((1, B_KV_LORA), const),
                  pl.BlockSpec(wq.shape, const),
                  pl.BlockSpec(wkv.shape, const)],
        out_specs=pl.BlockSpec((1, Q_TILE, 512), tile),
        out_shape=jax.ShapeDtypeStruct((n_b, T_ALL, 512), BF16),
        scratch_shapes=[pltpu.VMEM((T_ALL, N_HEADS * 128), BF16), pltpu.VMEM((T_ALL, N_HEADS * 128), BF16)],
        compiler_params=_cparams(("parallel", "arbitrary")),
        name="mla",
    )(p_b, p_b, cos, sin, cos, sin, q_gain.reshape(1, -1), kv_gain.reshape(1, -1), wq, wkv)


def _nat_kernel(pq_ref, pall_ref, bias_ref, o_ref, k_scr, v_scr):
    j = pl.program_id(1)

    @pl.when(j == 0)
    def _prep():
        for jb in range(N_HEADS // 2):
            kblk = pall_ref[0, :, 512 + 128 * jb:512 + 128 * (jb + 1)].astype(F32)
            for hh, kh in enumerate(_pad_heads(kblk)):
                h = 2 * jb + hh
                k_scr[:, 128 * h:128 * (h + 1)] = kh.astype(BF16)
            vblk = pall_ref[0, :, 1024 + 128 * jb:1024 + 128 * (jb + 1)]
            for hh in range(2):
                v_scr[:, 128 * (2 * jb + hh):128 * (2 * jb + hh + 1)] = _with_ones_lane(vblk, hh)

    def q_pair(jp):
        blk = pq_ref[0, :, 128 * jp:128 * (jp + 1)].astype(F32)
        return [(qh * (HEAD_DIM ** -0.5 * LOG2E)).astype(BF16) for qh in _pad_heads(blk)]

    hs = lambda jp, hh: slice(128 * (2 * jp + hh), 128 * (2 * jp + hh + 1))

    @pl.when(j == 0)
    def _ctx():
        for jp in range(N_HEADS // 2):
            qs = q_pair(jp)
            o = _softmax_pv_pair(qs[0], qs[1],
                                 lambda hh: k_scr[0:CTX_LEN, hs(jp, hh)],
                                 lambda hh: v_scr[0:CTX_LEN, hs(jp, hh)])
            o_ref[0, :, 128 * jp:128 * (jp + 1)] = o.astype(o_ref.dtype)

    @pl.when(j > 0)
    def _lat():
        first_row = jnp.clip(NAT_QROWS * (j - 1) - NA_ROWS // 2, 0, N_ROWS - NAT_KROWS)
        start = pl.multiple_of(CTX_LEN + first_row * GRID_W, Q_TILE)
        win = pl.ds(start, NAT_KWIN)
        for jp in range(N_HEADS // 2):
            qs = q_pair(jp)
            o = _softmax_pv_pair(
                qs[0], qs[1],
                lambda hh: k_scr[win, hs(jp, hh)],
                lambda hh: v_scr[win, hs(jp, hh)],
                extra=(lambda hh: bias_ref[0, 2 * jp + hh],
                       lambda hh: k_scr[0:CTX_LEN, hs(jp, hh)],
                       lambda hh: v_scr[0:CTX_LEN, hs(jp, hh)]))
            o_ref[0, :, 128 * jp:128 * (jp + 1)] = o.astype(o_ref.dtype)


def _nat_bias_table(rel_bias):
    n_dr, n_dc = 2 * NA_ROWS - 1, 2 * NA_COLS - 1
    cols = np.arange(GRID_W)
    pick_col = (cols[None, None, :] - cols[None, :, None] + NA_COLS - 1 == np.arange(n_dc)[:, None, None])
    pick_row = np.zeros((3, NAT_QROWS, NAT_KROWS, n_dr), np.float32)
    valid = np.zeros((3, Q_TILE, NAT_KWIN), bool)
    for t, qb in enumerate((0, 1, N_ROWS // NAT_QROWS - 1)):
        first_row = int(np.clip(NAT_QROWS * qb - NA_ROWS // 2, 0, N_ROWS - NAT_KROWS))
        rq = NAT_QROWS * qb + np.arange(NAT_QROWS)
        rk = first_row + np.arange(NAT_KROWS)
        pick_row[t] = (rk[None, :, None] - rq[:, None, None] + NA_ROWS - 1 == np.arange(n_dr)[None, None, :])
        ql, kl = np.arange(Q_TILE), np.arange(NAT_KWIN)
        r, c = NAT_QROWS * qb + ql // GRID_W, ql % GRID_W
        kr, kc = first_row + kl // GRID_W, kl % GRID_W
        r0 = np.clip(r - NA_ROWS // 2, 0, N_ROWS - NA_ROWS)
        c0 = np.clip(c - NA_COLS // 2, 0, GRID_W - NA_COLS)
        valid[t] = ((kr[None, :] >= r0[:, None]) & (kr[None, :] < r0[:, None] + NA_ROWS)
                    & (kc[None, :] >= c0[:, None]) & (kc[None, :] < c0[:, None] + NA_COLS))
    by_col = jnp.einsum('hdc,cab->hdab', rel_bias.astype(F32), jnp.asarray(pick_col, F32), precision=lax.Precision.HIGHEST)
    tab = jnp.einsum('tqkd,hdab->thqakb', jnp.asarray(pick_row), by_col, precision=lax.Precision.HIGHEST)
    tab = tab.reshape(3, N_HEADS, Q_TILE, NAT_KWIN) * LOG2E
    return jnp.where(jnp.asarray(valid)[:, None], tab, NEG_BIG)


def _nat(p_d, bias_tab):
    n_b = p_d.shape[0]
    tile = lambda b, j: (b, j, 0)
    n_lat_tiles = N_QT - 1

    def bias_idx(b, j):
        qb = j - 1
        return (jnp.where(qb <= 0, 0, jnp.where(qb == n_lat_tiles - 1, 2, 1)), 0, 0, 0)

    return pl.pallas_call(
        _nat_kernel,
        grid=(n_b, N_QT),
        in_specs=[pl.BlockSpec((1, Q_TILE, D_IN), tile),
                  pl.BlockSpec((1, T_ALL, D_IN), lambda b, j: (b, 0, 0)),
                  pl.BlockSpec((1, N_HEADS, Q_TILE, NAT_KWIN), bias_idx)],
        out_specs=pl.BlockSpec((1, Q_TILE, 512), tile),
        out_shape=jax.ShapeDtypeStruct((n_b, T_ALL, 512), BF16),
        scratch_shapes=[pltpu.VMEM((T_ALL, N_HEADS * 128), BF16), pltpu.VMEM((T_ALL, N_HEADS * 128), BF16)],
        compiler_params=_cparams(("parallel", "arbitrary")),
        name="nat",
    )(p_d, p_d, bias_tab)


def _seg_sum(x, ones_bd):
    return _dot3_exact_rhs(x, ones_bd)


def _rwkv_prep_kernel(z_ref, zp_ref, zn_ref, mu_ref, w0_ref, a0_ref, kk_ref, ka_ref, rk_ref,
                      wdh_ref, wdl_ref, wah_ref, wal_ref, wgh_ref, wgl_ref, ones_ref,
                      r_out, v_out, kk_out, kd_out, lw_out, bd_out, bonus_out, g_out):
    j = pl.program_id(1)
    z = z_ref[0]
    tm = z.shape[0]
    row = lax.broadcasted_iota(jnp.int32, (tm, 1), 0)
    prev_row = jnp.where(j <= 1, 0.0, zp_ref[0, 7:8, :])
    next_row = jnp.where((j == 0) | (j == N_QT - 1), 0.0, zn_ref[0, 0:1, :])
    z_prev = jnp.where(row == 0, prev_row, pltpu.roll(z, 1, 0))
    z_next = jnp.where(row == tm - 1, next_row, pltpu.roll(z, tm - 1, 0))
    zs = z + (0.5 * (z_prev + z_next) - z) * mu_ref[...]

    r = zs[:, 0:512]
    k = zs[:, 512:1024]
    v = zs[:, 1024:1536]
    w_lo = zs[:, 1536:1664]
    a_lo = zs[:, 1664:1792]
    g_lo = zs[:, 1792:1920]
    ones_bd = ones_ref[...]

    def lora(x, wh_ref, wl_ref):
        xh, xl = _split_bf16(x)
        return _dot(xh, wh_ref[...]) + _dot(xh, wl_ref[...]) + _dot(xl, wh_ref[...])

    kkr = k * kk_ref[...]
    nrm = jnp.maximum(jnp.sqrt(_seg_sum(kkr * kkr, ones_bd)), 1e-12)
    kk = kkr / nrm
    dec = lora(jnp.tanh(w_lo), wdh_ref, wdl_ref)
    aaa = lora(a_lo, wah_ref, wal_ref)
    g = lora(_sigmoid(g_lo), wgh_ref, wgl_ref)
    r_out[0] = r
    v_out[0] = v
    kk_out[0] = kk
    g_out[0] = g
    ksum = None
    for d in range(2):
        u = -(w0_ref[d:d + 1, :] + dec[:, 512 * d:512 * (d + 1)])
        softplus = jnp.maximum(u, 0.0) + jnp.log(1.0 + jnp.exp(-jnp.abs(u)))
        logw = -softplus - 0.5
        lw_out[d, 0] = -jnp.exp(logw)
        a = _sigmoid(a0_ref[d:d + 1, :] + aaa[:, 512 * d:512 * (d + 1)])
        bd_out[d, 0] = kk * a
        kd = k * (1.0 + (a - 1.0) * ka_ref[...])
        kd_out[d, 0] = kd
        ksum = kd if ksum is None else ksum + kd
    bonus_out[0] = _seg_sum(r * ksum * rk_ref[...], ones_bd) * v


def _rwkv_prep(p_c, lp):
    n_b = p_c.shape[0]
    tile = lambda b, j: (b, j, 0)
    const = lambda b, j: (0, 0)
    blocks8 = T_ALL // 8
    tpb = Q_TILE // 8
    prev = lambda b, j: (b, jnp.maximum(j * tpb - 1, 0), 0)
    nxt = lambda b, j: (b, jnp.minimum((j + 1) * tpb, blocks8 - 1), 0)
    o3 = jax.ShapeDtypeStruct((n_b, T_ALL, C_W), F32)
    o4 = jax.ShapeDtypeStruct((2, n_b, T_ALL, C_W), F32)
    s3 = pl.BlockSpec((1, Q_TILE, C_W), tile)
    s4 = pl.BlockSpec((2, 1, Q_TILE, C_W), lambda b, j: (0, b, j, 0))
    small = [lp['c_mu'].reshape(1, C_IN), lp['c_w0'], lp['c_a0'], lp['c_k_k'].reshape(1, C_W),
             lp['c_k_a'].reshape(1, C_W), lp['c_r_k'].reshape(1, C_W)]
    bd2 = lambda w: jnp.concatenate(
        [jnp.concatenate([w[0], jnp.zeros_like(w[0])], 1), jnp.concatenate([jnp.zeros_like(w[1]), w[1]], 1)], 0)
    mats = []
    for w in (bd2(lp['c_w_decay']), bd2(lp['c_w_aaa']), lp['c_w_gate']):
        mats.extend(_split_bf16(w))
    ones_bd = jnp.asarray(np.kron(np.eye(N_HEADS), np.ones((HEAD_DIM, HEAD_DIM))), BF16)
    ins = small + mats + [ones_bd]
    return pl.pallas_call(
        _rwkv_prep_kernel,
        grid=(n_b, N_QT),
        in_specs=[pl.BlockSpec((1, Q_TILE, C_IN), tile),
                  pl.BlockSpec((1, 8, C_IN), prev),
                  pl.BlockSpec((1, 8, C_IN), nxt)] + [pl.BlockSpec(a.shape, const) for a in ins],
        out_specs=[s3, s3, s3, s4, s4, s4, s3, s3],
        out_shape=[o3, o3, o3, o4, o4, o4, o3, o3],
        compiler_params=_cparams(("parallel", "parallel")),
        name="rwkv_prep",
    )(p_c, p_c, p_c, *ins)


def _dot3_exact_rhs_lhs(m_bf16, a):
    a0 = a.astype(BF16)
    r1 = a - a0.astype(F32)
    a1 = r1.astype(BF16)
    a2 = (r1 - a1.astype(F32)).astype(BF16)
    return _dot(m_bf16, a0) + _dot(m_bf16, a1) + _dot(m_bf16, a2)


N_INV_LEVELS = 6
MK_STRICT, MK_INCL, MK_LEVEL0 = 0, 1, 2


def _scan_masks():
    n = HG * CHUNK
    ri, ci = np.arange(n)[:, None], np.arange(n)[None, :]
    same = (ri // CHUNK) == (ci // CHUNK)
    masks, cums = [], []
    for fwd in (True, False):
        pos = (lambda t: t % CHUNK) if fwd else (lambda t: CHUNK - 1 - t % CHUNK)
        tr, tc = pos(ri), pos(ci)
        levels = [same & ((tr // (2 * h)) == (tc // (2 * h))) & (((tr // h) % 2) == 1) & (((tc // h) % 2) == 0)
                  for h in (2 ** k for k in range(N_INV_LEVELS))]
        masks.append(np.stack([same & (tr > tc), same & (tr >= tc)] + levels))
        t = pos(np.arange(CHUNK))
        cums.append(t[:, None] >= t[None, :])
    return jnp.asarray(np.stack(masks), F32), jnp.asarray(np.stack(cums), BF16)


def _rwkv_scan_kernel(rf_ref, vf_ref, kkf_ref, kdf_ref, lwf_ref, bdf_ref,
                      rb_ref, vb_ref, kkb_ref, kdb_ref, lwb_ref, bdb_ref, mk_ref, cum_ref, yf_ref, yb_ref, s_scr):
    j = pl.program_id(1)
    n = HG * CHUNK
    eye = (lax.broadcasted_iota(jnp.int32, (n, n), 0) == lax.broadcasted_iota(jnp.int32, (n, n), 1)).astype(F32)
    head_of_lane = lax.broadcasted_iota(jnp.int32, (CHUNK, HGW), 1) // HEAD_DIM
    dir_refs = ((rf_ref, vf_ref, kkf_ref, kdf_ref, lwf_ref, bdf_ref, yf_ref),
                (rb_ref, vb_ref, kkb_ref, kdb_ref, lwb_ref, bdb_ref, yb_ref))
    chunks_per_tile = Q_TILE // CHUNK

    def masked(d, k, x):
        return jnp.where(mk_ref[d, k] > 0.5, x, 0.0)

    def stack(x):
        return jnp.concatenate([jnp.where(head_of_lane == p, x, 0.0) for p in range(HG)], axis=0)

    @pl.when(j == 0)
    def _init():
        s_scr[...] = jnp.zeros_like(s_scr)

    def body(i, carry):
        ch = []
        for d in range(2):
            r_ref, v_ref, kk_ref, kd_ref, lw_ref, bd_ref, y_ref = dir_refs[d]
            c = i if d == 0 else chunks_per_tile - 1 - i
            rows = pl.ds(pl.multiple_of(c * CHUNK, CHUNK), CHUNK)
            lw = lw_ref[0, 0, rows, :]
            cum = _dot3_exact_rhs_lhs(cum_ref[d], lw)
            total = jnp.sum(lw, axis=0, keepdims=True)
            e_pos, e_neg, e_prev, e_rest = jnp.exp(cum), jnp.exp(-cum), jnp.exp(cum - lw), jnp.exp(total - cum)
            p_end = jnp.broadcast_to(jnp.exp(total), (CHUNK, C_W))
            for g in range(C_W // HGW):
                ln = slice(g * HGW, (g + 1) * HGW)
                kk, bd, kd = kk_ref[0, rows, ln], bd_ref[0, 0, rows, ln], kd_ref[0, 0, rows, ln]
                ch.append(dict(
                    idx=2 * d + g, d=d, rows=rows, ln=ln, y_ref=y_ref,
                    al4b=stack(kk * e_prev[:, ln]).astype(BF16), be4b=stack(bd * e_neg[:, ln]).astype(BF16),
                    ka4b=stack(kd * e_neg[:, ln]).astype(BF16),
                    rh4b=stack(r_ref[0, rows, ln] * e_pos[:, ln]).astype(BF16),
                    bee4t=stack(bd * e_rest[:, ln]).T.astype(BF16),
                    kae4t=stack(kd * e_rest[:, ln]).T.astype(BF16),
                    p_end_t=stack(p_end[:, ln]).T,
                    v4=stack(v_ref[0, rows, ln]).astype(BF16)))

        def stage(fn):
            for c_ in ch:
                c_.update(fn(c_, c_['d']))

        stage(lambda c_, d: dict(l_ab=masked(d, MK_STRICT, _dot_nt(c_['al4b'], c_['be4b']))))
        stage(lambda c_, d: dict(l_ak=masked(d, MK_STRICT, _dot_nt(c_['al4b'], c_['ka4b'])).astype(BF16)))
        fillers = [
            lambda c_, d: dict(m_rb=masked(d, MK_INCL, _dot_nt(c_['rh4b'], c_['be4b'])).astype(BF16)),
            lambda c_, d: dict(m_rk=masked(d, MK_INCL, _dot_nt(c_['rh4b'], c_['ka4b'])).astype(BF16)),
            lambda c_, d: dict(lv=_dot(c_['l_ak'], c_['v4']).astype(BF16)),
            lambda c_, d: dict(hv=_dot(c_['kae4t'], c_['v4'])),
            lambda c_, d: dict(yv=_dot(c_['m_rk'], c_['v4'])),
        ]
        stage(lambda c_, d: dict(x=eye - masked(d, MK_LEVEL0, c_['l_ab'])))
        for level in range(1, N_INV_LEVELS):
            stage(lambda c_, d: dict(
                xb=c_['x'].astype(BF16),
                t=_dot(masked(d, MK_LEVEL0 + level, c_['l_ab']).astype(BF16), c_['x'].astype(BF16)).astype(BF16)))
            stage(fillers[level - 1])
            stage(lambda c_, d: dict(x=c_['x'] - _dot(c_['xb'], c_['t'])))
        stage(lambda c_, d: dict(tb=c_['x'].astype(BF16)))
        stage(lambda c_, d: dict(w4b=_dot(c_['tb'], c_['al4b']).astype(BF16)))
        stage(lambda c_, d: dict(u04=_dot(c_['tb'], c_['lv'])))
        stage(lambda c_, d: dict(s=s_scr[c_['idx']]))
        stage(lambda c_, d: dict(sb=c_['s'].astype(BF16)))
        stage(lambda c_, d: dict(ub=(_dot(c_['w4b'], c_['sb']) + c_['u04']).astype(BF16)))
        stage(lambda c_, d: dict(y4=_dot(c_['rh4b'], c_['sb']) - _dot(c_['m_rb'], c_['ub']) + c_['yv']))
        for c_ in ch:
            s_scr[c_['idx']] = c_['s'] * c_['p_end_t'] - _dot(c_['bee4t'], c_['ub']) + c_['hv']
        for c_ in ch:
            y4 = c_['y4']
            y = y4[0:CHUNK]
            for p in range(1, HG):
                y = y + y4[p * CHUNK:(p + 1) * CHUNK]
            c_['y_ref'][0, c_['rows'], c_['ln']] = y
        return carry

    lax.fori_loop(0, chunks_per_tile, body, 0)


def _rwkv_scan(r, v, kk, kd, lw, bd):
    n_b = r.shape[0]
    bwd_tile = lambda j: jnp.where(j == 0, 0, N_QT - j)
    f3 = pl.BlockSpec((1, Q_TILE, C_W), lambda b, j: (b, j, 0))
    f4 = pl.BlockSpec((1, 1, Q_TILE, C_W), lambda b, j: (0, b, j, 0))
    b3 = pl.BlockSpec((1, Q_TILE, C_W), lambda b, j: (b, bwd_tile(j), 0))
    b4 = pl.BlockSpec((1, 1, Q_TILE, C_W), lambda b, j: (1, b, bwd_tile(j), 0))
    y_shape = jax.ShapeDtypeStruct((n_b, T_ALL, C_W), F32)
    masks, cums = _scan_masks()
    return pl.pallas_call(
        _rwkv_scan_kernel,
        grid=(n_b, N_QT),
        in_specs=[f3, f3, f3, f4, f4, f4, b3, b3, b3, b4, b4, b4,
                  pl.BlockSpec(masks.shape, lambda b, j: (0, 0, 0, 0)),
                  pl.BlockSpec(cums.shape, lambda b, j: (0, 0, 0))],
        out_specs=[f3, b3],
        out_shape=[y_shape, y_shape],
        scratch_shapes=[pltpu.VMEM((2 * (C_W // HGW), HGW, HGW), F32)],
        compiler_params=_cparams(("parallel", "arbitrary")),
        name="rwkv_scan",
    )(r, v, kk, kd, lw, bd, r, v, kk, kd, lw, bd, masks, cums)


def _rwkv_out_kernel(yf_ref, yb_ref, bonus_ref, g_ref, gw_ref, gb_ref, ones_ref, o_ref):
    y = yf_ref[0] + yb_ref[0]
    ones_bd = ones_ref[...]
    mean = _seg_sum(y, ones_bd) * (1.0 / HEAD_DIM)
    yc = y - mean
    var = _seg_sum(yc * yc, ones_bd) * (1.0 / HEAD_DIM)
    yn = yc * lax.rsqrt(var + C_GN_EPS) * gw_ref[...] + gb_ref[...]
    o_ref[0] = ((yn + bonus_ref[0]) * g_ref[0]).astype(o_ref.dtype)


def _rwkv_out(y_f, y_b, bonus, g, gn_w, gn_b):
    n_b = bonus.shape[0]
    tm = 768
    tile = lambda b, j: (b, j, 0)
    const = lambda b, j: (0, 0)
    ones_bd = jnp.asarray(np.kron(np.eye(N_HEADS), np.ones((HEAD_DIM, HEAD_DIM))), BF16)
    return pl.pallas_call(
        _rwkv_out_kernel,
        grid=(n_b, T_ALL // tm),
        in_specs=[pl.BlockSpec((1, tm, C_W), tile), pl.BlockSpec((1, tm, C_W), tile),
                  pl.BlockSpec((1, tm, C_W), tile), pl.BlockSpec((1, tm, C_W), tile),
                  pl.BlockSpec((1, C_W), const), pl.BlockSpec((1, C_W), const),
                  pl.BlockSpec((C_W, C_W), const)],
        out_specs=pl.BlockSpec((1, tm, C_W), tile),
        out_shape=jax.ShapeDtypeStruct((n_b, T_ALL, C_W), BF16),
        compiler_params=_cparams(("parallel", "parallel")),
        name="rwkv_out",
    )(y_f, y_b, bonus, g, gn_w.reshape(1, C_W), gn_b.reshape(1, C_W), ones_bd)


def _merge_kernel(x_ref, oa_ref, ob_ref, oc_ref, od_ref, gate_ref, wb_ref, wo_ref, mb_ref, mc_ref, out_ref, *, tm):
    is_ctx = _is_ctx_rows(tm, 1)
    y = None
    for i, o_ref in enumerate((oa_ref, ob_ref, oc_ref, od_ref)):
        z = _dot(o_ref[0], wb_ref[i])
        sg = _sigmoid(gate_ref[0, :, i * D_MODEL:(i + 1) * D_MODEL].astype(F32))
        y = sg * z if y is None else y + sg * z
    z = _dot(y.astype(BF16), wo_ref[...])
    out_ref[0] = x_ref[0] + _mod_vec(mb_ref, mc_ref, 2, is_ctx) * z


def _merge(x, outs, gates, wb, wo, modl):
    n_b = x.shape[0]
    tm = 768
    tile = lambda b, j: (b, j, 0)
    mb, mc = _mod_specs(n_b, 2)
    o_spec = pl.BlockSpec((1, tm, 512), tile)
    return pl.pallas_call(
        functools.partial(_merge_kernel, tm=tm),
        grid=(n_b, T_ALL // tm),
        in_specs=[pl.BlockSpec((1, tm, D_MODEL), tile), o_spec, o_spec, o_spec, o_spec,
                  pl.BlockSpec((1, tm, GATE_IN), tile),
                  pl.BlockSpec(wb.shape, lambda b, j: (0, 0, 0)),
                  pl.BlockSpec(wo.shape, lambda b, j: (0, 0)), mb, mc],
        out_specs=pl.BlockSpec((1, tm, D_MODEL), tile),
        out_shape=jax.ShapeDtypeStruct(x.shape, F32),
        compiler_params=_cparams(("parallel", "parallel")),
        name="merge",
    )(x, *outs, gates, wb, wo, modl, modl)


def _mlp_kernel(x_ref, g_ref, mb_ref, mc_ref, w1_ref, w2_ref, out_ref, h_scr, acc_scr, *, tm, n_f):
    f = pl.program_id(2)
    is_ctx = _is_ctx_rows(tm, 1)

    @pl.when(f == 0)
    def _init():
        h = _norm_mod(x_ref[0], g_ref[...], _mod_vec(mb_ref, mc_ref, 3, is_ctx), _mod_vec(mb_ref, mc_ref, 4, is_ctx))
        h_scr[...] = h.astype(BF16)
        acc_scr[...] = jnp.zeros_like(acc_scr)

    a = jnp.square(jnp.maximum(_dot(h_scr[...], w1_ref[...]), 0.0))
    acc_scr[...] += _dot(a.astype(BF16), w2_ref[...])

    @pl.when(f == n_f - 1)
    def _fin():
        out_ref[0] = x_ref[0] + _mod_vec(mb_ref, mc_ref, 5, is_ctx) * acc_scr[...]


def _mlp(x, g, w1, w2, modl):
    n_b = x.shape[0]
    tm, tf = 768, 512
    n_f = D_FF // tf
    tile = lambda b, j, f: (b, j, 0)
    mb, mc = _mod_specs(n_b, 3)
    return pl.pallas_call(
        functools.partial(_mlp_kernel, tm=tm, n_f=n_f),
        grid=(n_b, T_ALL // tm, n_f),
        in_specs=[pl.BlockSpec((1, tm, D_MODEL), tile),
                  pl.BlockSpec((1, D_MODEL), lambda b, j, f: (0, 0)), mb, mc,
                  pl.BlockSpec((D_MODEL, tf), lambda b, j, f: (0, f)),
                  pl.BlockSpec((tf, D_MODEL), lambda b, j, f: (f, 0))],
        out_specs=pl.BlockSpec((1, tm, D_MODEL), tile),
        out_shape=jax.ShapeDtypeStruct(x.shape, F32),
        scratch_shapes=[pltpu.VMEM((tm, D_MODEL), BF16), pltpu.VMEM((tm, D_MODEL), F32)],
        compiler_params=_cparams(("parallel", "parallel", "arbitrary")),
        name="mlp",
    )(x, g.reshape(1, D_MODEL), modl, modl, w1, w2)


def _final_kernel(x_ref, g_ref, o_ref):
    x = x_ref[0]
    o_ref[0] = x * lax.rsqrt(jnp.mean(x * x, axis=-1, keepdims=True) + NORM_EPS) * g_ref[...]


def _final_norm(x, g):
    n_b = x.shape[0]
    tm = Q_TILE
    return pl.pallas_call(
        _final_kernel,
        grid=(n_b, SEQ // tm),
        in_specs=[pl.BlockSpec((1, tm, D_MODEL), lambda b, j: (b, j + CTX_LEN // tm, 0)),
                  pl.BlockSpec((1, D_MODEL), lambda b, j: (0, 0))],
        out_specs=pl.BlockSpec((1, tm, D_MODEL), lambda b, j: (b, j, 0)),
        out_shape=jax.ShapeDtypeStruct((n_b, SEQ, D_MODEL), F32),
        compiler_params=_cparams(("parallel", "parallel")),
        name="final_norm",
    )(x, g.reshape(1, D_MODEL))


def _rope_tables(half, lane0):
    t = np.arange(SEQ)
    inv = ROPE_THETA ** (-np.arange(half, dtype=np.float64) / half)
    cos = np.ones((T_ALL, V7X_LANES), np.float64)
    sin = np.zeros((T_ALL, V7X_LANES), np.float64)
    for part, pos in enumerate((t // GRID_W, t % GRID_W)):
        ang = pos[:, None].astype(np.float64) * inv[None, :]
        ang = ang.astype(np.float32).astype(np.float64)
        base = lane0 + 2 * half * part
        cos[CTX_LEN:, base:base + half] = np.cos(ang)
        cos[CTX_LEN:, base + half:base + 2 * half] = np.cos(ang)
        sin[CTX_LEN:, base:base + half] = -np.sin(ang)
        sin[CTX_LEN:, base + half:base + 2 * half] = np.sin(ang)
    return jnp.asarray(cos, F32), jnp.asarray(sin, F32)


def _layer_weights(l, w_in, b_w_q_up, b_w_kv_up, w_branch, w_out, w_mlp1, w_mlp2):
    wi = w_in[l]
    o_b = A_IN
    o_c = o_b + B_IN
    o_d = o_c + C_IN
    o_g = o_d + D_IN
    w_a = wi[:, :o_b]
    wb_raw = wi[:, o_b:o_c]
    z = lambda n: jnp.zeros((D_MODEL, n), F32)
    w_b = jnp.concatenate([wb_raw[:, :B_Q_LORA + B_KV_LORA], z(64), wb_raw[:, B_Q_LORA + B_KV_LORA:], z(32)], 1)
    wq = b_w_q_up[l].reshape(B_Q_LORA, N_HEADS, B_NOPE + B_ROPE)
    wq = jnp.concatenate([wq, jnp.zeros((B_Q_LORA, N_HEADS, 128 - B_NOPE - B_ROPE), F32)], -1)
    wkv = b_w_kv_up[l].reshape(B_KV_LORA, N_HEADS, 2 * HEAD_DIM)
    zk = jnp.zeros((B_KV_LORA, N_HEADS, HEAD_DIM), F32)
    wk = jnp.concatenate([wkv[:, :, :B_NOPE], zk], -1)
    even = (jnp.arange(N_HEADS) % 2 == 0)[None, :, None]
    wv = jnp.concatenate([jnp.where(even, wkv[:, :, B_NOPE:], 0.0), jnp.where(even, 0.0, wkv[:, :, B_NOPE:])], -1)
    return dict(
        w_a=w_a.astype(BF16), w_b=w_b.astype(BF16), w_c=wi[:, o_c:o_d].astype(BF16),
        w_d=wi[:, o_d:o_g].astype(BF16), w_g=wi[:, o_g:].astype(BF16),
        wq=wq.reshape(B_Q_LORA, N_HEADS * 128).astype(BF16),
        wkv=jnp.concatenate([wk.reshape(B_KV_LORA, -1), wv.reshape(B_KV_LORA, -1)], 1).astype(BF16),
        w_branch=w_branch[l].astype(BF16), w_out=w_out[l].astype(BF16),
        w_mlp1=w_mlp1[l].astype(BF16), w_mlp2=w_mlp2[l].astype(BF16))


def kernel(x, c, ctx, c_ctx, w_ada, b_ada, g_norm1, g_norm2, w_in, a_q_gain, a_k_gain, b_q_gain, b_kv_gain,
           b_w_q_up, b_w_kv_up, c_mu, c_w0, c_w_decay, c_a0, c_w_aaa, c_w_gate, c_k_k, c_k_a, c_r_k,
           c_gn_w, c_gn_b, d_rel_bias, w_branch, w_out, w_mlp1, w_mlp2, g_final):
    n_b = x.shape[0]
    assert x.shape[1:] == (SEQ, D_MODEL) and ctx.shape[1:] == (CTX_LEN, D_MODEL)
    m = n_b * T_ALL
    mod_rows = ((n_b + 1 + 7) // 8) * 8
    cc = jnp.concatenate([c, c_ctx[None, :], jnp.zeros((mod_rows - n_b - 1, D_MODEL), F32)], 0)
    mod_all = _ada(cc, w_ada, b_ada)
    cos_a, sin_a = _rope_tables(16, 0)
    cos_b, sin_b = _rope_tables(8, B_NOPE)
    xs = jnp.concatenate([ctx, x], axis=1)
    for l in range(DEPTH):
        lw = _layer_weights(l, w_in, b_w_q_up, b_w_kv_up, w_branch, w_out, w_mlp1, w_mlp2)
        lp = dict(c_mu=c_mu[l], c_w0=c_w0[l], c_w_decay=c_w_decay[l], c_a0=c_a0[l], c_w_aaa=c_w_aaa[l],
                  c_w_gate=c_w_gate[l], c_k_k=c_k_k[l], c_k_a=c_k_a[l], c_r_k=c_r_k[l])
        modl = mod_all[l].reshape(mod_rows, 1, 6 * D_MODEL)
        h = _norm1(xs, g_norm1[l], modl).reshape(m, D_MODEL)
        p_a = _mm(h, lw['w_a'], BF16, 768, A_IN, "w_in_a").reshape(n_b, T_ALL, A_IN)
        p_b = _mm(h, lw['w_b'], BF16, 768, 768, "w_in_b").reshape(n_b, T_ALL, 768)
        p_c = _mm(h, lw['w_c'], F32, 768, 640, "w_in_c").reshape(n_b, T_ALL, C_IN)
        p_d = _mm(h, lw['w_d'], BF16, 768, 768, "w_in_d").reshape(n_b, T_ALL, D_IN)
        p_g = _mm(h, lw['w_g'], BF16, 768, 1024, "w_in_g").reshape(n_b, T_ALL, GATE_IN)
        o_a = _gqa(p_a, cos_a, sin_a, a_q_gain[l], a_k_gain[l])
        o_b = _mla(p_b, cos_b, sin_b, b_q_gain[l], b_kv_gain[l], lw['wq'], lw['wkv'])
        r, v, kk, kd, lwd, bd, bonus, g = _rwkv_prep(p_c, lp)
        y_f, y_b = _rwkv_scan(r, v, kk, kd, lwd, bd)
        o_c = _rwkv_out(y_f, y_b, bonus, g, c_gn_w[l], c_gn_b[l])
        o_d = _nat(p_d, _nat_bias_table(d_rel_bias[l]))
        xs = _merge(xs, (o_a, o_b, o_c, o_d), p_g, lw['w_branch'], lw['w_out'], modl)
        xs = _mlp(xs, g_norm2[l], lw['w_mlp1'], lw['w_mlp2'], modl)
    return _final_norm(xs, g_final)
```

```python
import functools

import numpy as np
import jax
import jax.numpy as jnp
from jax import lax
from jax.experimental import pallas as pl
from jax.experimental.pallas import tpu as pltpu

F32 = jnp.float32
BF16 = jnp.bfloat16

D_MODEL = 1024
SEQ = 2048
DEPTH = 2
GRID_W = 64
N_ROWS = SEQ // GRID_W
CTX_LEN = 256
T_ALL = CTX_LEN + SEQ
HEAD_DIM = 64
ROPE_THETA = 10000.0
NORM_EPS = 1e-6
N_HEADS = 8
A_KV_HEADS = 2
B_Q_LORA = 384
B_KV_LORA = 256
B_NOPE = 64
B_ROPE = 32
C_W = 512
C_GN_EPS = 64e-5
NA_ROWS = 8
NA_COLS = 16
D_FF = 4 * D_MODEL
A_IN = 768
B_IN = 672
C_IN = 1920
D_IN = 1536
GATE_IN = 4096

V7X_LANES = 128
V7X_VMEM_LIMIT = 56 * 1024 * 1024

Q_TILE = 256
N_QT = T_ALL // Q_TILE
NAT_QROWS = Q_TILE // GRID_W
NAT_KROWS = 12
NAT_KWIN = NAT_KROWS * GRID_W
CHUNK = 64
N_CHUNK = T_ALL // CHUNK
N_CTX_CHUNK = CTX_LEN // CHUNK
HG = 4
HGW = HG * HEAD_DIM
NEG_BIG = -1e30


def _cparams(sem, vmem=V7X_VMEM_LIMIT):
    return pltpu.CompilerParams(dimension_semantics=sem, vmem_limit_bytes=vmem)


def _split_bf16(a):
    hi = a.astype(BF16)
    lo = (a - hi.astype(F32)).astype(BF16)
    return hi, lo


def _dot(a, b):
    return jnp.dot(a, b, preferred_element_type=F32)


def _dot_nt(a, b):
    return lax.dot_general(a, b, (((1,), (1,)), ((), ())), preferred_element_type=F32)


def _dot3(a, b):
    ah, al = _split_bf16(a)
    bh, bl = _split_bf16(b)
    return _dot(ah, bh) + _dot(ah, bl) + _dot(al, bh)


def _dot3_exact_rhs(a, b_bf16):
    a0 = a.astype(BF16)
    r1 = a - a0.astype(F32)
    a1 = r1.astype(BF16)
    a2 = (r1 - a1.astype(F32)).astype(BF16)
    return _dot(a0, b_bf16) + _dot(a1, b_bf16) + _dot(a2, b_bf16)


def _sigmoid(x):
    return 1.0 / (1.0 + jnp.exp(-x))


def _lane(shape):
    return lax.broadcasted_iota(jnp.int32, shape, len(shape) - 1)


def _rope(x, cos, sin, half):
    n = x.shape[-1]
    lo = (_lane(x.shape) % (2 * half)) < half
    partner = jnp.where(lo, pltpu.roll(x, n - half, 1), pltpu.roll(x, half, 1))
    return x * cos + partner * sin


def _mod_vec(mb_ref, mc_ref, k, is_ctx):
    lat = mb_ref[0, :, k * D_MODEL:(k + 1) * D_MODEL]
    ctx = mc_ref[0, :, k * D_MODEL:(k + 1) * D_MODEL]
    return jnp.where(is_ctx, ctx, lat)


def _is_ctx_rows(tm, tile_axis):
    row = pl.program_id(tile_axis) * tm + lax.broadcasted_iota(jnp.int32, (tm, 1), 0)
    return row < CTX_LEN


def _norm_mod(x, g, shift, scale):
    y = x * lax.rsqrt(jnp.mean(x * x, axis=-1, keepdims=True) + NORM_EPS) * g
    return y * (1.0 + scale) + shift


def _ada_kernel(c_ref, w_ref, b_ref, o_ref):
    c = c_ref[...]
    s = c * _sigmoid(c)
    o_ref[0] = _dot3(s, w_ref[0]) + b_ref[0]


def _ada(cc, w_ada, b_ada):
    n_l, _, n_out = w_ada.shape
    tn = 1536
    rows = cc.shape[0]
    return pl.pallas_call(
        _ada_kernel,
        grid=(n_l, n_out // tn),
        in_specs=[pl.BlockSpec((rows, D_MODEL), lambda l, j: (0, 0)),
                  pl.BlockSpec((1, D_MODEL, tn), lambda l, j: (l, 0, j)),
                  pl.BlockSpec((1, 1, tn), lambda l, j: (l, 0, j))],
        out_specs=pl.BlockSpec((1, rows, tn), lambda l, j: (l, 0, j)),
        out_shape=jax.ShapeDtypeStruct((n_l, rows, n_out), F32),
        compiler_params=_cparams(("arbitrary", "arbitrary")),
        name="ada",
    )(cc, w_ada, b_ada.reshape(n_l, 1, n_out))


def _mod_specs(n_b, grid_rank):
    if grid_rank == 2:
        return (pl.BlockSpec((1, 1, 6 * D_MODEL), lambda b, j: (b, 0, 0)),
                pl.BlockSpec((1, 1, 6 * D_MODEL), lambda b, j: (n_b, 0, 0)))
    return (pl.BlockSpec((1, 1, 6 * D_MODEL), lambda b, j, f: (b, 0, 0)),
            pl.BlockSpec((1, 1, 6 * D_MODEL), lambda b, j, f: (n_b, 0, 0)))


def _norm_kernel(x_ref, g_ref, mb_ref, mc_ref, h_ref, *, tm):
    is_ctx = _is_ctx_rows(tm, 1)
    h = _norm_mod(x_ref[0], g_ref[...], _mod_vec(mb_ref, mc_ref, 0, is_ctx), _mod_vec(mb_ref, mc_ref, 1, is_ctx))
    h_ref[0] = h.astype(h_ref.dtype)


def _norm1(x, g, modl):
    n_b = x.shape[0]
    tm = 768
    mb, mc = _mod_specs(n_b, 2)
    return pl.pallas_call(
        functools.partial(_norm_kernel, tm=tm),
        grid=(n_b, T_ALL // tm),
        in_specs=[pl.BlockSpec((1, tm, D_MODEL), lambda b, j: (b, j, 0)),
                  pl.BlockSpec((1, D_MODEL), lambda b, j: (0, 0)), mb, mc],
        out_specs=pl.BlockSpec((1, tm, D_MODEL), lambda b, j: (b, j, 0)),
        out_shape=jax.ShapeDtypeStruct(x.shape, BF16),
        compiler_params=_cparams(("parallel", "parallel")),
        name="norm1",
    )(x, g.reshape(1, D_MODEL), modl, modl)


def _mm_kernel(a_ref, w_ref, o_ref):
    o_ref[...] = _dot(a_ref[...], w_ref[...]).astype(o_ref.dtype)


def _mm(a, w, out_dtype, tm, tn, name):
    m, k = a.shape
    n = w.shape[1]
    return pl.pallas_call(
        _mm_kernel,
        grid=(m // tm, n // tn),
        in_specs=[pl.BlockSpec((tm, k), lambda i, j: (i, 0)),
                  pl.BlockSpec((k, tn), lambda i, j: (0, j))],
        out_specs=pl.BlockSpec((tm, tn), lambda i, j: (i, j)),
        out_shape=jax.ShapeDtypeStruct((m, n), out_dtype),
        compiler_params=_cparams(("parallel", "arbitrary")),
        name=name,
    )(a, w)


LOG2E = float(np.log2(np.e))
SUM_LANE = (HEAD_DIM, 0)


def _with_ones_lane(vblk, hh):
    lane = _lane(vblk.shape)
    keep = (lane < HEAD_DIM) if hh == 0 else (lane >= HEAD_DIM)
    return jnp.where(keep, vblk, jnp.where(lane == SUM_LANE[hh], 1.0, 0.0).astype(vblk.dtype))


def _attend_heads(o_ref, q_of, k_of, v_of, extra=None):
    def scores(h):
        q = q_of(h)
        s = _dot_nt(q, k_of(h))
        if extra is None:
            return (s,)
        return (s + extra[0](h), _dot_nt(q, extra[1](h)))

    def finish(h, sc):
        m = jnp.max(sc[0], axis=-1, keepdims=True)
        for s in sc[1:]:
            m = jnp.maximum(m, jnp.max(s, axis=-1, keepdims=True))
        o = _dot(jnp.exp2(sc[0] - m).astype(BF16), v_of(h))
        if extra is not None:
            o = o + _dot(jnp.exp2(sc[1] - m).astype(BF16), extra[2](h))
        lane = SUM_LANE[h % 2]
        return o / o[:, lane:lane + 1]

    nxt = scores(0)
    even = None
    for h in range(N_HEADS):
        cur = nxt
        if h + 1 < N_HEADS:
            nxt = scores(h + 1)
        o = finish(h, cur)
        if h % 2 == 0:
            even = o
        else:
            pair = jnp.where(_lane(o.shape) < HEAD_DIM, even, o)
            o_ref[0, :, 128 * (h // 2):128 * (h // 2 + 1)] = pair.astype(o_ref.dtype)


def _pad_head(blk, hh):
    return jnp.where(_lane(blk.shape) < HEAD_DIM, blk if hh == 0 else pltpu.roll(blk, HEAD_DIM, 1), 0.0)


def _pad_heads(blk):
    return _pad_head(blk, 0), _pad_head(blk, 1)


def _head_rms(x, gain):
    ms = jnp.sum(x * x, axis=-1, keepdims=True) * (1.0 / HEAD_DIM)
    return x * lax.rsqrt(ms + NORM_EPS) * gain


def _gqa_kernel(pq_ref, pall_ref, cq_ref, sq_ref, call_ref, sall_ref, qg_ref, kg_ref, o_ref, k_scr, v_scr):
    j = pl.program_id(1)

    @pl.when(j == 0)
    def _prep():
        kblk = pall_ref[0, :, 512:640].astype(F32)
        for g, kh in enumerate(_pad_heads(kblk)):
            kh = _rope(_head_rms(kh, kg_ref[...]), call_ref[...], sall_ref[...], 16)
            k_scr[:, 128 * g:128 * (g + 1)] = kh.astype(BF16)
        vblk = pall_ref[0, :, 640:768].astype(F32)
        vrot = pltpu.roll(vblk, HEAD_DIM, 1)
        for i, src in enumerate((vblk, vrot, vrot, vblk)):
            v_scr[:, 128 * i:128 * (i + 1)] = _with_ones_lane(src, i % 2).astype(BF16)

    def attend(n_keys):
        group = N_HEADS // A_KV_HEADS

        def q_of(h):
            qh = _pad_head(pq_ref[0, :, 128 * (h // 2):128 * (h // 2 + 1)].astype(F32), h % 2)
            qh = _rope(_head_rms(qh, qg_ref[...]), cq_ref[...], sq_ref[...], 16)
            return (qh * (HEAD_DIM ** -0.5 * LOG2E)).astype(BF16)

        def v_of(h):
            i = 2 * (h // group) + h % 2
            return v_scr[0:n_keys, 128 * i:128 * (i + 1)]

        _attend_heads(o_ref, q_of, lambda h: k_scr[0:n_keys, 128 * (h // group):128 * (h // group + 1)], v_of)

    @pl.when(j == 0)
    def _ctx():
        attend(CTX_LEN)

    @pl.when(j > 0)
    def _lat():
        attend(T_ALL)


def _gqa(p_a, cos, sin, q_gain, k_gain):
    n_b = p_a.shape[0]
    pad = lambda g: jnp.concatenate([g, jnp.zeros((HEAD_DIM,), F32)]).reshape(1, 128)
    tile = lambda b, j: (b, j, 0)
    whole = lambda b, j: (b, 0, 0)
    return pl.pallas_call(
        _gqa_kernel,
        grid=(n_b, N_QT),
        in_specs=[pl.BlockSpec((1, Q_TILE, A_IN), tile),
                  pl.BlockSpec((1, T_ALL, A_IN), whole),
                  pl.BlockSpec((Q_TILE, 128), lambda b, j: (j, 0)),
                  pl.BlockSpec((Q_TILE, 128), lambda b, j: (j, 0)),
                  pl.BlockSpec((T_ALL, 128), lambda b, j: (0, 0)),
                  pl.BlockSpec((T_ALL, 128), lambda b, j: (0, 0)),
                  pl.BlockSpec((1, 128), lambda b, j: (0, 0)),
                  pl.BlockSpec((1, 128), lambda b, j: (0, 0))],
        out_specs=pl.BlockSpec((1, Q_TILE, 512), tile),
        out_shape=jax.ShapeDtypeStruct((n_b, T_ALL, 512), BF16),
        scratch_shapes=[pltpu.VMEM((T_ALL, 256), BF16), pltpu.VMEM((T_ALL, 512), BF16)],
        compiler_params=_cparams(("parallel", "arbitrary")),
        name="gqa",
    )(p_a, p_a, cos, sin, cos, sin, pad(q_gain), pad(k_gain))


def _mla_kernel(pq_ref, pall_ref, cq_ref, sq_ref, call_ref, sall_ref, qg_ref, kvg_ref, wq_ref, wkv_ref,
                o_ref, k_scr, v_scr):
    j = pl.program_id(1)
    kw = N_HEADS * 128

    @pl.when(j == 0)
    def _prep():
        for i in range(N_QT):
            rows = slice(i * Q_TILE, (i + 1) * Q_TILE)
            ckv = pall_ref[0, rows, B_Q_LORA:B_Q_LORA + B_KV_LORA].astype(F32)
            n = ckv * lax.rsqrt(jnp.mean(ckv * ckv, axis=-1, keepdims=True) + NORM_EPS) * kvg_ref[...]
            kv = _dot(n.astype(BF16), wkv_ref[...])
            kr = _rope(pall_ref[0, rows, 640:768].astype(F32), call_ref[rows, :], sall_ref[rows, :], 8)
            for h in range(N_HEADS):
                k_scr[rows, 128 * h:128 * (h + 1)] = (kv[:, 128 * h:128 * (h + 1)] + kr).astype(BF16)
            for h in range(N_HEADS):
                v_scr[rows, 128 * h:128 * (h + 1)] = _with_ones_lane(kv[:, kw + 128 * h:kw + 128 * (h + 1)], h % 2).astype(BF16)

    def attend(n_keys):
        cq = pq_ref[0, :, 0:B_Q_LORA].astype(F32)
        n = cq * lax.rsqrt(jnp.mean(cq * cq, axis=-1, keepdims=True) + NORM_EPS) * qg_ref[...]
        q = _dot(n.astype(BF16), wq_ref[...])
        scale = (B_NOPE + B_ROPE) ** -0.5 * LOG2E
        _attend_heads(
            o_ref,
            lambda h: (_rope(q[:, 128 * h:128 * (h + 1)], cq_ref[...], sq_ref[...], 8) * scale).astype(BF16),
            lambda h: k_scr[0:n_keys, 128 * h:128 * (h + 1)],
            lambda h: v_scr[0:n_keys, 128 * h:128 * (h + 1)])

    @pl.when(j == 0)
    def _ctx():
        attend(CTX_LEN)

    @pl.when(j > 0)
    def _lat():
        attend(T_ALL)


def _mla(p_b, cos, sin, q_gain, kv_gain, wq, wkv):
    n_b = p_b.shape[0]
    tile = lambda b, j: (b, j, 0)
    whole = lambda b, j: (b, 0, 0)
    const = lambda b, j: (0, 0)
    w = p_b.shape[-1]
    return pl.pallas_call(
        _mla_kernel,
        grid=(n_b, N_QT),
        in_specs=[pl.BlockSpec((1, Q_TILE, w), tile),
                  pl.BlockSpec((1, T_ALL, w), whole),
                  pl.BlockSpec((Q_TILE, 128), lambda b, j: (j, 0)),
                  pl.BlockSpec((Q_TILE, 128), lambda b, j: (j, 0)),
                  pl.BlockSpec((T_ALL, 128), const),
                  pl.BlockSpec((T_ALL, 128), const),
                  pl.BlockSpec((1, B_Q_LORA), const),
                  pl.BlockSpec((1, B_KV_LORA), const),
                  pl.BlockSpec(wq.shape, const),
                  pl.BlockSpec(wkv.shape, const)],
        out_specs=pl.BlockSpec((1, Q_TILE, 512), tile),
        out_shape=jax.ShapeDtypeStruct((n_b, T_ALL, 512), BF16),
        scratch_shapes=[pltpu.VMEM((T_ALL, N_HEADS * 128), BF16), pltpu.VMEM((T_ALL, N_HEADS * 128), BF16)],
        compiler_params=_cparams(("parallel", "arbitrary")),
        name="mla",
    )(p_b, p_b, cos, sin, cos, sin, q_gain.reshape(1, -1), kv_gain.reshape(1, -1), wq, wkv)


def _nat_kernel(pq_ref, pall_ref, bias_ref, o_ref, k_scr, v_scr):
    j = pl.program_id(1)

    @pl.when(j == 0)
    def _prep():
        for jb in range(N_HEADS // 2):
            kblk = pall_ref[0, :, 512 + 128 * jb:512 + 128 * (jb + 1)].astype(F32)
            for hh, kh in enumerate(_pad_heads(kblk)):
                h = 2 * jb + hh
                k_scr[:, 128 * h:128 * (h + 1)] = kh.astype(BF16)
            vblk = pall_ref[0, :, 1024 + 128 * jb:1024 + 128 * (jb + 1)]
            for hh in range(2):
                v_scr[:, 128 * (2 * jb + hh):128 * (2 * jb + hh + 1)] = _with_ones_lane(vblk, hh)

    def q_of(h):
        qh = _pad_head(pq_ref[0, :, 128 * (h // 2):128 * (h // 2 + 1)].astype(F32), h % 2)
        return (qh * (HEAD_DIM ** -0.5 * LOG2E)).astype(BF16)

    hs = lambda h: slice(128 * h, 128 * (h + 1))

    @pl.when(j == 0)
    def _ctx():
        _attend_heads(o_ref, q_of, lambda h: k_scr[0:CTX_LEN, hs(h)], lambda h: v_scr[0:CTX_LEN, hs(h)])

    @pl.when(j > 0)
    def _lat():
        first_row = jnp.clip(NAT_QROWS * (j - 1) - NA_ROWS // 2, 0, N_ROWS - NAT_KROWS)
        start = pl.multiple_of(CTX_LEN + first_row * GRID_W, Q_TILE)
        win = pl.ds(start, NAT_KWIN)
        _attend_heads(o_ref, q_of, lambda h: k_scr[win, hs(h)], lambda h: v_scr[win, hs(h)],
                      extra=(lambda h: bias_ref[0, h],
                             lambda h: k_scr[0:CTX_LEN, hs(h)],
                             lambda h: v_scr[0:CTX_LEN, hs(h)]))


def _nat_bias_table(rel_bias):
    n_dr, n_dc = 2 * NA_ROWS - 1, 2 * NA_COLS - 1
    cols = np.arange(GRID_W)
    pick_col = (cols[None, None, :] - cols[None, :, None] + NA_COLS - 1 == np.arange(n_dc)[:, None, None])
    pick_row = np.zeros((3, NAT_QROWS, NAT_KROWS, n_dr), np.float32)
    valid = np.zeros((3, Q_TILE, NAT_KWIN), bool)
    for t, qb in enumerate((0, 1, N_ROWS // NAT_QROWS - 1)):
        first_row = int(np.clip(NAT_QROWS * qb - NA_ROWS // 2, 0, N_ROWS - NAT_KROWS))
        rq = NAT_QROWS * qb + np.arange(NAT_QROWS)
        rk = first_row + np.arange(NAT_KROWS)
        pick_row[t] = (rk[None, :, None] - rq[:, None, None] + NA_ROWS - 1 == np.arange(n_dr)[None, None, :])
        ql, kl = np.arange(Q_TILE), np.arange(NAT_KWIN)
        r, c = NAT_QROWS * qb + ql // GRID_W, ql % GRID_W
        kr, kc = first_row + kl // GRID_W, kl % GRID_W
        r0 = np.clip(r - NA_ROWS // 2, 0, N_ROWS - NA_ROWS)
        c0 = np.clip(c - NA_COLS // 2, 0, GRID_W - NA_COLS)
        valid[t] = ((kr[None, :] >= r0[:, None]) & (kr[None, :] < r0[:, None] + NA_ROWS)
                    & (kc[None, :] >= c0[:, None]) & (kc[None, :] < c0[:, None] + NA_COLS))
    by_col = jnp.einsum('hdc,cab->hdab', rel_bias.astype(F32), jnp.asarray(pick_col, F32), precision=lax.Precision.HIGHEST)
    tab = jnp.einsum('tqkd,hdab->thqakb', jnp.asarray(pick_row), by_col, precision=lax.Precision.HIGHEST)
    tab = tab.reshape(3, N_HEADS, Q_TILE, NAT_KWIN) * LOG2E
    return jnp.where(jnp.asarray(valid)[:, None], tab, NEG_BIG)


def _nat(p_d, bias_tab):
    n_b = p_d.shape[0]
    tile = lambda b, j: (b, j, 0)
    n_lat_tiles = N_QT - 1

    def bias_idx(b, j):
        qb = j - 1
        return (jnp.where(qb <= 0, 0, jnp.where(qb == n_lat_tiles - 1, 2, 1)), 0, 0, 0)

    return pl.pallas_call(
        _nat_kernel,
        grid=(n_b, N_QT),
        in_specs=[pl.BlockSpec((1, Q_TILE, D_IN), tile),
                  pl.BlockSpec((1, T_ALL, D_IN), lambda b, j: (b, 0, 0)),
                  pl.BlockSpec((1, N_HEADS, Q_TILE, NAT_KWIN), bias_idx)],
        out_specs=pl.BlockSpec((1, Q_TILE, 512), tile),
        out_shape=jax.ShapeDtypeStruct((n_b, T_ALL, 512), BF16),
        scratch_shapes=[pltpu.VMEM((T_ALL, N_HEADS * 128), BF16), pltpu.VMEM((T_ALL, N_HEADS * 128), BF16)],
        compiler_params=_cparams(("parallel", "arbitrary")),
        name="nat",
    )(p_d, p_d, bias_tab)


def _seg_sum(x, ones_bd):
    return _dot3_exact_rhs(x, ones_bd)


def _rwkv_prep_kernel(z_ref, zp_ref, zn_ref, mu_ref, w0_ref, a0_ref, kk_ref, ka_ref, rk_ref,
                      wdh_ref, wdl_ref, wah_ref, wal_ref, wgh_ref, wgl_ref, ones_ref,
                      r_out, v_out, kk_out, kd_out, lw_out, bd_out, bonus_out, g_out):
    j = pl.program_id(1)
    z = z_ref[0]
    tm = z.shape[0]
    row = lax.broadcasted_iota(jnp.int32, (tm, 1), 0)
    prev_row = jnp.where(j <= 1, 0.0, zp_ref[0, 7:8, :])
    next_row = jnp.where((j == 0) | (j == N_QT - 1), 0.0, zn_ref[0, 0:1, :])
    z_prev = jnp.where(row == 0, prev_row, pltpu.roll(z, 1, 0))
    z_next = jnp.where(row == tm - 1, next_row, pltpu.roll(z, tm - 1, 0))
    zs = z + (0.5 * (z_prev + z_next) - z) * mu_ref[...]

    r = zs[:, 0:512]
    k = zs[:, 512:1024]
    v = zs[:, 1024:1536]
    w_lo = zs[:, 1536:1664]
    a_lo = zs[:, 1664:1792]
    g_lo = zs[:, 1792:1920]
    ones_bd = ones_ref[...]

    def lora(x, wh_ref, wl_ref):
        xh, xl = _split_bf16(x)
        return _dot(xh, wh_ref[...]) + _dot(xh, wl_ref[...]) + _dot(xl, wh_ref[...])

    kkr = k * kk_ref[...]
    nrm = jnp.maximum(jnp.sqrt(_seg_sum(kkr * kkr, ones_bd)), 1e-12)
    kk = kkr / nrm
    dec = lora(jnp.tanh(w_lo), wdh_ref, wdl_ref)
    aaa = lora(a_lo, wah_ref, wal_ref)
    g = lora(_sigmoid(g_lo), wgh_ref, wgl_ref)
    r_out[0] = r
    v_out[0] = v
    kk_out[0] = kk
    g_out[0] = g
    ksum = None
    for d in range(2):
        u = -(w0_ref[d:d + 1, :] + dec[:, 512 * d:512 * (d + 1)])
        softplus = jnp.maximum(u, 0.0) + jnp.log(1.0 + jnp.exp(-jnp.abs(u)))
        logw = -softplus - 0.5
        lw_out[d, 0] = -jnp.exp(logw)
        a = _sigmoid(a0_ref[d:d + 1, :] + aaa[:, 512 * d:512 * (d + 1)])
        bd_out[d, 0] = kk * a
        kd = k * (1.0 + (a - 1.0) * ka_ref[...])
        kd_out[d, 0] = kd
        ksum = kd if ksum is None else ksum + kd
    bonus_out[0] = _seg_sum(r * ksum * rk_ref[...], ones_bd) * v


def _rwkv_prep(p_c, lp):
    n_b = p_c.shape[0]
    tile = lambda b, j: (b, j, 0)
    const = lambda b, j: (0, 0)
    blocks8 = T_ALL // 8
    tpb = Q_TILE // 8
    prev = lambda b, j: (b, jnp.maximum(j * tpb - 1, 0), 0)
    nxt = lambda b, j: (b, jnp.minimum((j + 1) * tpb, blocks8 - 1), 0)
    o3 = jax.ShapeDtypeStruct((n_b, T_ALL, C_W), F32)
    o4 = jax.ShapeDtypeStruct((2, n_b, T_ALL, C_W), F32)
    s3 = pl.BlockSpec((1, Q_TILE, C_W), tile)
    s4 = pl.BlockSpec((2, 1, Q_TILE, C_W), lambda b, j: (0, b, j, 0))
    small = [lp['c_mu'].reshape(1, C_IN), lp['c_w0'], lp['c_a0'], lp['c_k_k'].reshape(1, C_W),
             lp['c_k_a'].reshape(1, C_W), lp['c_r_k'].reshape(1, C_W)]
    bd2 = lambda w: jnp.concatenate(
        [jnp.concatenate([w[0], jnp.zeros_like(w[0])], 1), jnp.concatenate([jnp.zeros_like(w[1]), w[1]], 1)], 0)
    mats = []
    for w in (bd2(lp['c_w_decay']), bd2(lp['c_w_aaa']), lp['c_w_gate']):
        mats.extend(_split_bf16(w))
    ones_bd = jnp.asarray(np.kron(np.eye(N_HEADS), np.ones((HEAD_DIM, HEAD_DIM))), BF16)
    ins = small + mats + [ones_bd]
    return pl.pallas_call(
        _rwkv_prep_kernel,
        grid=(n_b, N_QT),
        in_specs=[pl.BlockSpec((1, Q_TILE, C_IN), tile),
                  pl.BlockSpec((1, 8, C_IN), prev),
                  pl.BlockSpec((1, 8, C_IN), nxt)] + [pl.BlockSpec(a.shape, const) for a in ins],
        out_specs=[s3, s3, s3, s4, s4, s4, s3, s3],
        out_shape=[o3, o3, o3, o4, o4, o4, o3, o3],
        compiler_params=_cparams(("parallel", "parallel")),
        name="rwkv_prep",
    )(p_c, p_c, p_c, *ins)


def _dot3_exact_rhs_lhs(m_bf16, a):
    a0 = a.astype(BF16)
    r1 = a - a0.astype(F32)
    a1 = r1.astype(BF16)
    a2 = (r1 - a1.astype(F32)).astype(BF16)
    return _dot(m_bf16, a0) + _dot(m_bf16, a1) + _dot(m_bf16, a2)


N_INV_LEVELS = 6
MK_STRICT, MK_INCL, MK_LEVEL0 = 0, 1, 2


def _scan_masks():
    n = HG * CHUNK
    ri, ci = np.arange(n)[:, None], np.arange(n)[None, :]
    same = (ri // CHUNK) == (ci // CHUNK)
    masks, cums = [], []
    for fwd in (True, False):
        pos = (lambda t: t % CHUNK) if fwd else (lambda t: CHUNK - 1 - t % CHUNK)
        tr, tc = pos(ri), pos(ci)
        levels = [same & ((tr // (2 * h)) == (tc // (2 * h))) & (((tr // h) % 2) == 1) & (((tc // h) % 2) == 0)
                  for h in (2 ** k for k in range(N_INV_LEVELS))]
        masks.append(np.stack([same & (tr > tc), same & (tr >= tc)] + levels))
        t = pos(np.arange(CHUNK))
        cums.append(t[:, None] >= t[None, :])
    return jnp.asarray(np.stack(masks), F32), jnp.asarray(np.stack(cums), BF16)


def _rwkv_scan_kernel(rf_ref, vf_ref, kkf_ref, kdf_ref, lwf_ref, bdf_ref,
                      rb_ref, vb_ref, kkb_ref, kdb_ref, lwb_ref, bdb_ref, mk_ref, cum_ref, yf_ref, yb_ref, s_scr):
    j = pl.program_id(1)
    n = HG * CHUNK
    eye = (lax.broadcasted_iota(jnp.int32, (n, n), 0) == lax.broadcasted_iota(jnp.int32, (n, n), 1)).astype(F32)
    head_of_lane = lax.broadcasted_iota(jnp.int32, (CHUNK, HGW), 1) // HEAD_DIM
    dir_refs = ((rf_ref, vf_ref, kkf_ref, kdf_ref, lwf_ref, bdf_ref, yf_ref),
                (rb_ref, vb_ref, kkb_ref, kdb_ref, lwb_ref, bdb_ref, yb_ref))
    chunks_per_tile = Q_TILE // CHUNK

    def masked(d, k, x):
        return jnp.where(mk_ref[d, k] > 0.5, x, 0.0)

    def stack(x):
        return jnp.concatenate([jnp.where(head_of_lane == p, x, 0.0) for p in range(HG)], axis=0)

    @pl.when(j == 0)
    def _init():
        s_scr[...] = jnp.zeros_like(s_scr)

    def body(i, carry):
        ch = []
        for d in range(2):
            r_ref, v_ref, kk_ref, kd_ref, lw_ref, bd_ref, y_ref = dir_refs[d]
            c = i if d == 0 else chunks_per_tile - 1 - i
            rows = pl.ds(pl.multiple_of(c * CHUNK, CHUNK), CHUNK)
            lw = lw_ref[0, 0, rows, :]
            cum = _dot3_exact_rhs_lhs(cum_ref[d], lw)
            total = jnp.sum(lw, axis=0, keepdims=True)
            e_pos, e_neg, e_prev, e_rest = jnp.exp(cum), jnp.exp(-cum), jnp.exp(cum - lw), jnp.exp(total - cum)
            p_end = jnp.broadcast_to(jnp.exp(total), (CHUNK, C_W))
            for g in range(C_W // HGW):
                ln = slice(g * HGW, (g + 1) * HGW)
                kk, bd, kd = kk_ref[0, rows, ln], bd_ref[0, 0, rows, ln], kd_ref[0, 0, rows, ln]
                ch.append(dict(
                    idx=2 * d + g, d=d, rows=rows, ln=ln, y_ref=y_ref,
                    al4b=stack(kk * e_prev[:, ln]).astype(BF16), be4b=stack(bd * e_neg[:, ln]).astype(BF16),
                    ka4b=stack(kd * e_neg[:, ln]).astype(BF16),
                    rh4b=stack(r_ref[0, rows, ln] * e_pos[:, ln]).astype(BF16),
                    bee4t=stack(bd * e_rest[:, ln]).T.astype(BF16),
                    kae4t=stack(kd * e_rest[:, ln]).T.astype(BF16),
                    p_end_t=stack(p_end[:, ln]).T,
                    v4=stack(v_ref[0, rows, ln]).astype(BF16)))

        def stage(fn):
            for c_ in ch:
                c_.update(fn(c_, c_['d']))

        stage(lambda c_, d: dict(l_ab=masked(d, MK_STRICT, _dot_nt(c_['al4b'], c_['be4b']))))
        stage(lambda c_, d: dict(l_ak=masked(d, MK_STRICT, _dot_nt(c_['al4b'], c_['ka4b'])).astype(BF16)))
        fillers = [
            lambda c_, d: dict(m_rb=masked(d, MK_INCL, _dot_nt(c_['rh4b'], c_['be4b'])).astype(BF16)),
            lambda c_, d: dict(m_rk=masked(d, MK_INCL, _dot_nt(c_['rh4b'], c_['ka4b'])).astype(BF16)),
            lambda c_, d: dict(lv=_dot(c_['l_ak'], c_['v4']).astype(BF16)),
            lambda c_, d: dict(hv=_dot(c_['kae4t'], c_['v4'])),
            lambda c_, d: dict(yv=_dot(c_['m_rk'], c_['v4'])),
        ]
        stage(lambda c_, d: dict(x=eye - masked(d, MK_LEVEL0, c_['l_ab'])))
        for level in range(1, N_INV_LEVELS):
            stage(lambda c_, d: dict(
                xb=c_['x'].astype(BF16),
                t=_dot(masked(d, MK_LEVEL0 + level, c_['l_ab']).astype(BF16), c_['x'].astype(BF16)).astype(BF16)))
            stage(fillers[level - 1])
            stage(lambda c_, d: dict(x=c_['x'] - _dot(c_['xb'], c_['t'])))
        stage(lambda c_, d: dict(tb=c_['x'].astype(BF16)))
        stage(lambda c_, d: dict(w4b=_dot(c_['tb'], c_['al4b']).astype(BF16)))
        stage(lambda c_, d: dict(u04=_dot(c_['tb'], c_['lv'])))
        stage(lambda c_, d: dict(s=s_scr[c_['idx']]))
        stage(lambda c_, d: dict(sb=c_['s'].astype(BF16)))
        stage(lambda c_, d: dict(ub=(_dot(c_['w4b'], c_['sb']) + c_['u04']).astype(BF16)))
        stage(lambda c_, d: dict(y4=_dot(c_['rh4b'], c_['sb']) - _dot(c_['m_rb'], c_['ub']) + c_['yv']))
        for c_ in ch:
            s_scr[c_['idx']] = c_['s'] * c_['p_end_t'] - _dot(c_['bee4t'], c_['ub']) + c_['hv']
        for c_ in ch:
            y4 = c_['y4']
            y = y4[0:CHUNK]
            for p in range(1, HG):
                y = y + y4[p * CHUNK:(p + 1) * CHUNK]
            c_['y_ref'][0, c_['rows'], c_['ln']] = y
        return carry

    lax.fori_loop(0, chunks_per_tile, body, 0, unroll=2)


def _rwkv_scan(r, v, kk, kd, lw, bd):
    n_b = r.shape[0]
    bwd_tile = lambda j: jnp.where(j == 0, 0, N_QT - j)
    f3 = pl.BlockSpec((1, Q_TILE, C_W), lambda b, j: (b, j, 0))
    f4 = pl.BlockSpec((1, 1, Q_TILE, C_W), lambda b, j: (0, b, j, 0))
    b3 = pl.BlockSpec((1, Q_TILE, C_W), lambda b, j: (b, bwd_tile(j), 0))
    b4 = pl.BlockSpec((1, 1, Q_TILE, C_W), lambda b, j: (1, b, bwd_tile(j), 0))
    y_shape = jax.ShapeDtypeStruct((n_b, T_ALL, C_W), F32)
    masks, cums = _scan_masks()
    return pl.pallas_call(
        _rwkv_scan_kernel,
        grid=(n_b, N_QT),
        in_specs=[f3, f3, f3, f4, f4, f4, b3, b3, b3, b4, b4, b4,
                  pl.BlockSpec(masks.shape, lambda b, j: (0, 0, 0, 0)),
                  pl.BlockSpec(cums.shape, lambda b, j: (0, 0, 0))],
        out_specs=[f3, b3],
        out_shape=[y_shape, y_shape],
        scratch_shapes=[pltpu.VMEM((2 * (C_W // HGW), HGW, HGW), F32)],
        compiler_params=_cparams(("parallel", "arbitrary")),
        name="rwkv_scan",
    )(r, v, kk, kd, lw, bd, r, v, kk, kd, lw, bd, masks, cums)


def _rwkv_out_kernel(yf_ref, yb_ref, bonus_ref, g_ref, gw_ref, gb_ref, ones_ref, o_ref):
    y = yf_ref[0] + yb_ref[0]
    ones_bd = ones_ref[...]
    mean = _seg_sum(y, ones_bd) * (1.0 / HEAD_DIM)
    yc = y - mean
    var = _seg_sum(yc * yc, ones_bd) * (1.0 / HEAD_DIM)
    yn = yc * lax.rsqrt(var + C_GN_EPS) * gw_ref[...] + gb_ref[...]
    o_ref[0] = ((yn + bonus_ref[0]) * g_ref[0]).astype(o_ref.dtype)


def _rwkv_out(y_f, y_b, bonus, g, gn_w, gn_b):
    n_b = bonus.shape[0]
    tm = 768
    tile = lambda b, j: (b, j, 0)
    const = lambda b, j: (0, 0)
    ones_bd = jnp.asarray(np.kron(np.eye(N_HEADS), np.ones((HEAD_DIM, HEAD_DIM))), BF16)
    return pl.pallas_call(
        _rwkv_out_kernel,
        grid=(n_b, T_ALL // tm),
        in_specs=[pl.BlockSpec((1, tm, C_W), tile), pl.BlockSpec((1, tm, C_W), tile),
                  pl.BlockSpec((1, tm, C_W), tile), pl.BlockSpec((1, tm, C_W), tile),
                  pl.BlockSpec((1, C_W), const), pl.BlockSpec((1, C_W), const),
                  pl.BlockSpec((C_W, C_W), const)],
        out_specs=pl.BlockSpec((1, tm, C_W), tile),
        out_shape=jax.ShapeDtypeStruct((n_b, T_ALL, C_W), BF16),
        compiler_params=_cparams(("parallel", "parallel")),
        name="rwkv_out",
    )(y_f, y_b, bonus, g, gn_w.reshape(1, C_W), gn_b.reshape(1, C_W), ones_bd)


def _merge_kernel(x_ref, oa_ref, ob_ref, oc_ref, od_ref, gate_ref, wb_ref, wo_ref, mb_ref, mc_ref, out_ref, *, tm):
    is_ctx = _is_ctx_rows(tm, 1)
    y = None
    for i, o_ref in enumerate((oa_ref, ob_ref, oc_ref, od_ref)):
        z = _dot(o_ref[0], wb_ref[i])
        sg = _sigmoid(gate_ref[0, :, i * D_MODEL:(i + 1) * D_MODEL].astype(F32))
        y = sg * z if y is None else y + sg * z
    z = _dot(y.astype(BF16), wo_ref[...])
    out_ref[0] = x_ref[0] + _mod_vec(mb_ref, mc_ref, 2, is_ctx) * z


def _merge(x, outs, gates, wb, wo, modl):
    n_b = x.shape[0]
    tm = 768
    tile = lambda b, j: (b, j, 0)
    mb, mc = _mod_specs(n_b, 2)
    o_spec = pl.BlockSpec((1, tm, 512), tile)
    return pl.pallas_call(
        functools.partial(_merge_kernel, tm=tm),
        grid=(n_b, T_ALL // tm),
        in_specs=[pl.BlockSpec((1, tm, D_MODEL), tile), o_spec, o_spec, o_spec, o_spec,
                  pl.BlockSpec((1, tm, GATE_IN), tile),
                  pl.BlockSpec(wb.shape, lambda b, j: (0, 0, 0)),
                  pl.BlockSpec(wo.shape, lambda b, j: (0, 0)), mb, mc],
        out_specs=pl.BlockSpec((1, tm, D_MODEL), tile),
        out_shape=jax.ShapeDtypeStruct(x.shape, F32),
        compiler_params=_cparams(("parallel", "parallel")),
        name="merge",
    )(x, *outs, gates, wb, wo, modl, modl)


def _mlp_kernel(x_ref, g_ref, mb_ref, mc_ref, w1_ref, w2_ref, out_ref, h_scr, acc_scr, *, tm, n_f):
    f = pl.program_id(2)
    is_ctx = _is_ctx_rows(tm, 1)

    @pl.when(f == 0)
    def _init():
        h = _norm_mod(x_ref[0], g_ref[...], _mod_vec(mb_ref, mc_ref, 3, is_ctx), _mod_vec(mb_ref, mc_ref, 4, is_ctx))
        h_scr[...] = h.astype(BF16)
        acc_scr[...] = jnp.zeros_like(acc_scr)

    a = jnp.square(jnp.maximum(_dot(h_scr[...], w1_ref[...]), 0.0))
    acc_scr[...] += _dot(a.astype(BF16), w2_ref[...])

    @pl.when(f == n_f - 1)
    def _fin():
        out_ref[0] = x_ref[0] + _mod_vec(mb_ref, mc_ref, 5, is_ctx) * acc_scr[...]


def _mlp(x, g, w1, w2, modl):
    n_b = x.shape[0]
    tm, tf = 768, 512
    n_f = D_FF // tf
    tile = lambda b, j, f: (b, j, 0)
    mb, mc = _mod_specs(n_b, 3)
    return pl.pallas_call(
        functools.partial(_mlp_kernel, tm=tm, n_f=n_f),
        grid=(n_b, T_ALL // tm, n_f),
        in_specs=[pl.BlockSpec((1, tm, D_MODEL), tile),
                  pl.BlockSpec((1, D_MODEL), lambda b, j, f: (0, 0)), mb, mc,
                  pl.BlockSpec((D_MODEL, tf), lambda b, j, f: (0, f)),
                  pl.BlockSpec((tf, D_MODEL), lambda b, j, f: (f, 0))],
        out_specs=pl.BlockSpec((1, tm, D_MODEL), tile),
        out_shape=jax.ShapeDtypeStruct(x.shape, F32),
        scratch_shapes=[pltpu.VMEM((tm, D_MODEL), BF16), pltpu.VMEM((tm, D_MODEL), F32)],
        compiler_params=_cparams(("parallel", "parallel", "arbitrary")),
        name="mlp",
    )(x, g.reshape(1, D_MODEL), modl, modl, w1, w2)


def _final_kernel(x_ref, g_ref, o_ref):
    x = x_ref[0]
    o_ref[0] = x * lax.rsqrt(jnp.mean(x * x, axis=-1, keepdims=True) + NORM_EPS) * g_ref[...]


def _final_norm(x, g):
    n_b = x.shape[0]
    tm = Q_TILE
    return pl.pallas_call(
        _final_kernel,
        grid=(n_b, SEQ // tm),
        in_specs=[pl.BlockSpec((1, tm, D_MODEL), lambda b, j: (b, j + CTX_LEN // tm, 0)),
                  pl.BlockSpec((1, D_MODEL), lambda b, j: (0, 0))],
        out_specs=pl.BlockSpec((1, tm, D_MODEL), lambda b, j: (b, j, 0)),
        out_shape=jax.ShapeDtypeStruct((n_b, SEQ, D_MODEL), F32),
        compiler_params=_cparams(("parallel", "parallel")),
        name="final_norm",
    )(x, g.reshape(1, D_MODEL))


def _rope_tables(half, lane0):
    t = np.arange(SEQ)
    inv = ROPE_THETA ** (-np.arange(half, dtype=np.float64) / half)
    cos = np.ones((T_ALL, V7X_LANES), np.float64)
    sin = np.zeros((T_ALL, V7X_LANES), np.float64)
    for part, pos in enumerate((t // GRID_W, t % GRID_W)):
        ang = pos[:, None].astype(np.float64) * inv[None, :]
        ang = ang.astype(np.float32).astype(np.float64)
        base = lane0 + 2 * half * part
        cos[CTX_LEN:, base:base + half] = np.cos(ang)
        cos[CTX_LEN:, base + half:base + 2 * half] = np.cos(ang)
        sin[CTX_LEN:, base:base + half] = -np.sin(ang)
        sin[CTX_LEN:, base + half:base + 2 * half] = np.sin(ang)
    return jnp.asarray(cos, F32), jnp.asarray(sin, F32)


def _layer_weights(l, w_in, b_w_q_up, b_w_kv_up, w_branch, w_out, w_mlp1, w_mlp2):
    wi = w_in[l]
    o_b = A_IN
    o_c = o_b + B_IN
    o_d = o_c + C_IN
    o_g = o_d + D_IN
    w_a = wi[:, :o_b]
    wb_raw = wi[:, o_b:o_c]
    z = lambda n: jnp.zeros((D_MODEL, n), F32)
    w_b = jnp.concatenate([wb_raw[:, :B_Q_LORA + B_KV_LORA], z(64), wb_raw[:, B_Q_LORA + B_KV_LORA:], z(32)], 1)
    wq = b_w_q_up[l].reshape(B_Q_LORA, N_HEADS, B_NOPE + B_ROPE)
    wq = jnp.concatenate([wq, jnp.zeros((B_Q_LORA, N_HEADS, 128 - B_NOPE - B_ROPE), F32)], -1)
    wkv = b_w_kv_up[l].reshape(B_KV_LORA, N_HEADS, 2 * HEAD_DIM)
    zk = jnp.zeros((B_KV_LORA, N_HEADS, HEAD_DIM), F32)
    wk = jnp.concatenate([wkv[:, :, :B_NOPE], zk], -1)
    even = (jnp.arange(N_HEADS) % 2 == 0)[None, :, None]
    wv = jnp.concatenate([jnp.where(even, wkv[:, :, B_NOPE:], 0.0), jnp.where(even, 0.0, wkv[:, :, B_NOPE:])], -1)
    return dict(
        w_a=w_a.astype(BF16), w_b=w_b.astype(BF16), w_c=wi[:, o_c:o_d].astype(BF16),
        w_d=wi[:, o_d:o_g].astype(BF16), w_g=wi[:, o_g:].astype(BF16),
        wq=wq.reshape(B_Q_LORA, N_HEADS * 128).astype(BF16),
        wkv=jnp.concatenate([wk.reshape(B_KV_LORA, -1), wv.reshape(B_KV_LORA, -1)], 1).astype(BF16),
        w_branch=w_branch[l].astype(BF16), w_out=w_out[l].astype(BF16),
        w_mlp1=w_mlp1[l].astype(BF16), w_mlp2=w_mlp2[l].astype(BF16))


def kernel(x, c, ctx, c_ctx, w_ada, b_ada, g_norm1, g_norm2, w_in, a_q_gain, a_k_gain, b_q_gain, b_kv_gain,
           b_w_q_up, b_w_kv_up, c_mu, c_w0, c_w_decay, c_a0, c_w_aaa, c_w_gate, c_k_k, c_k_a, c_r_k,
           c_gn_w, c_gn_b, d_rel_bias, w_branch, w_out, w_mlp1, w_mlp2, g_final):
    n_b = x.shape[0]
    assert x.shape[1:] == (SEQ, D_MODEL) and ctx.shape[1:] == (CTX_LEN, D_MODEL)
    m = n_b * T_ALL
    mod_rows = ((n_b + 1 + 7) // 8) * 8
    cc = jnp.concatenate([c, c_ctx[None, :], jnp.zeros((mod_rows - n_b - 1, D_MODEL), F32)], 0)
    mod_all = _ada(cc, w_ada, b_ada)
    cos_a, sin_a = _rope_tables(16, 0)
    cos_b, sin_b = _rope_tables(8, B_NOPE)
    xs = jnp.concatenate([ctx, x], axis=1)
    for l in range(DEPTH):
        lw = _layer_weights(l, w_in, b_w_q_up, b_w_kv_up, w_branch, w_out, w_mlp1, w_mlp2)
        lp = dict(c_mu=c_mu[l], c_w0=c_w0[l], c_w_decay=c_w_decay[l], c_a0=c_a0[l], c_w_aaa=c_w_aaa[l],
                  c_w_gate=c_w_gate[l], c_k_k=c_k_k[l], c_k_a=c_k_a[l], c_r_k=c_r_k[l])
        modl = mod_all[l].reshape(mod_rows, 1, 6 * D_MODEL)
        h = _norm1(xs, g_norm1[l], modl).reshape(m, D_MODEL)
        p_a = _mm(h, lw['w_a'], BF16, 768, A_IN, "w_in_a").reshape(n_b, T_ALL, A_IN)
        p_b = _mm(h, lw['w_b'], BF16, 768, 768, "w_in_b").reshape(n_b, T_ALL, 768)
        p_c = _mm(h, lw['w_c'], F32, 768, 640, "w_in_c").reshape(n_b, T_ALL, C_IN)
        p_d = _mm(h, lw['w_d'], BF16, 768, 768, "w_in_d").reshape(n_b, T_ALL, D_IN)
        p_g = _mm(h, lw['w_g'], BF16, 768, 1024, "w_in_g").reshape(n_b, T_ALL, GATE_IN)
        o_a = _gqa(p_a, cos_a, sin_a, a_q_gain[l], a_k_gain[l])
        o_b = _mla(p_b, cos_b, sin_b, b_q_gain[l], b_kv_gain[l], lw['wq'], lw['wkv'])
        r, v, kk, kd, lwd, bd, bonus, g = _rwkv_prep(p_c, lp)
        y_f, y_b = _rwkv_scan(r, v, kk, kd, lwd, bd)
        o_c = _rwkv_out(y_f, y_b, bonus, g, c_gn_w[l], c_gn_b[l])
        o_d = _nat(p_d, _nat_bias_table(d_rel_bias[l]))
        xs = _merge(xs, (o_a, o_b, o_c, o_d), p_g, lw['w_branch'], lw['w_out'], modl)
        xs = _mlp(xs, g_norm2[l], lw['w_mlp1'], lw['w_mlp2'], modl)
    return _final_norm(xs, g_final)
```

```python
import functools

import numpy as np
import jax
import jax.numpy as jnp
from jax import lax
from jax.experimental import pallas as pl
from jax.experimental.pallas import tpu as pltpu

F32 = jnp.float32
BF16 = jnp.bfloat16

D_MODEL = 1024
SEQ = 2048
DEPTH = 2
GRID_W = 64
N_ROWS = SEQ // GRID_W
CTX_LEN = 256
T_ALL = CTX_LEN + SEQ
HEAD_DIM = 64
ROPE_THETA = 10000.0
NORM_EPS = 1e-6
N_HEADS = 8
A_KV_HEADS = 2
B_Q_LORA = 384
B_KV_LORA = 256
B_NOPE = 64
B_ROPE = 32
C_W = 512
C_GN_EPS = 64e-5
NA_ROWS = 8
NA_COLS = 16
D_FF = 4 * D_MODEL
A_IN = 768
B_IN = 672
C_IN = 1920
D_IN = 1536
GATE_IN = 4096

V7X_LANES = 128
V7X_VMEM_LIMIT = 56 * 1024 * 1024

Q_TILE = 256
N_QT = T_ALL // Q_TILE
NAT_QROWS = Q_TILE // GRID_W
NAT_KROWS = 12
NAT_KWIN = NAT_KROWS * GRID_W
CHUNK = 64
N_CHUNK = T_ALL // CHUNK
N_CTX_CHUNK = CTX_LEN // CHUNK
HG = 4
HGW = HG * HEAD_DIM
NEG_BIG = -1e30


def _cparams(sem, vmem=V7X_VMEM_LIMIT):
    return pltpu.CompilerParams(dimension_semantics=sem, vmem_limit_bytes=vmem)


def _split_bf16(a):
    hi = a.astype(BF16)
    lo = (a - hi.astype(F32)).astype(BF16)
    return hi, lo


def _dot(a, b):
    return jnp.dot(a, b, preferred_element_type=F32)


def _dot_nt(a, b):
    return lax.dot_general(a, b, (((1,), (1,)), ((), ())), preferred_element_type=F32)


def _dot3(a, b):
    ah, al = _split_bf16(a)
    bh, bl = _split_bf16(b)
    return _dot(ah, bh) + _dot(ah, bl) + _dot(al, bh)


def _dot3_exact_rhs(a, b_bf16):
    a0 = a.astype(BF16)
    r1 = a - a0.astype(F32)
    a1 = r1.astype(BF16)
    a2 = (r1 - a1.astype(F32)).astype(BF16)
    return _dot(a0, b_bf16) + _dot(a1, b_bf16) + _dot(a2, b_bf16)


def _sigmoid(x):
    return 1.0 / (1.0 + jnp.exp(-x))


def _lane(shape):
    return lax.broadcasted_iota(jnp.int32, shape, len(shape) - 1)


def _rope(x, cos, sin, half):
    n = x.shape[-1]
    lo = (_lane(x.shape) % (2 * half)) < half
    partner = jnp.where(lo, pltpu.roll(x, n - half, 1), pltpu.roll(x, half, 1))
    return x * cos + partner * sin


def _mod_vec(mb_ref, mc_ref, k, is_ctx):
    lat = mb_ref[0, :, k * D_MODEL:(k + 1) * D_MODEL]
    ctx = mc_ref[0, :, k * D_MODEL:(k + 1) * D_MODEL]
    return jnp.where(is_ctx, ctx, lat)


def _is_ctx_rows(tm, tile_axis):
    row = pl.program_id(tile_axis) * tm + lax.broadcasted_iota(jnp.int32, (tm, 1), 0)
    return row < CTX_LEN


def _norm_mod(x, g, shift, scale):
    y = x * lax.rsqrt(jnp.mean(x * x, axis=-1, keepdims=True) + NORM_EPS) * g
    return y * (1.0 + scale) + shift


def _ada_kernel(c_ref, w_ref, b_ref, o_ref):
    c = c_ref[...]
    s = c * _sigmoid(c)
    o_ref[0] = _dot3(s, w_ref[0]) + b_ref[0]


def _ada(cc, w_ada, b_ada):
    n_l, _, n_out = w_ada.shape
    tn = 1536
    rows = cc.shape[0]
    return pl.pallas_call(
        _ada_kernel,
        grid=(n_l, n_out // tn),
        in_specs=[pl.BlockSpec((rows, D_MODEL), lambda l, j: (0, 0)),
                  pl.BlockSpec((1, D_MODEL, tn), lambda l, j: (l, 0, j)),
                  pl.BlockSpec((1, 1, tn), lambda l, j: (l, 0, j))],
        out_specs=pl.BlockSpec((1, rows, tn), lambda l, j: (l, 0, j)),
        out_shape=jax.ShapeDtypeStruct((n_l, rows, n_out), F32),
        compiler_params=_cparams(("arbitrary", "arbitrary")),
        name="ada",
    )(cc, w_ada, b_ada.reshape(n_l, 1, n_out))


def _mod_specs(n_b, grid_rank):
    if grid_rank == 2:
        return (pl.BlockSpec((1, 1, 6 * D_MODEL), lambda b, j: (b, 0, 0)),
                pl.BlockSpec((1, 1, 6 * D_MODEL), lambda b, j: (n_b, 0, 0)))
    return (pl.BlockSpec((1, 1, 6 * D_MODEL), lambda b, j, f: (b, 0, 0)),
            pl.BlockSpec((1, 1, 6 * D_MODEL), lambda b, j, f: (n_b, 0, 0)))


def _norm_kernel(x_ref, g_ref, mb_ref, mc_ref, h_ref, *, tm):
    is_ctx = _is_ctx_rows(tm, 1)
    h = _norm_mod(x_ref[0], g_ref[...], _mod_vec(mb_ref, mc_ref, 0, is_ctx), _mod_vec(mb_ref, mc_ref, 1, is_ctx))
    h_ref[0] = h.astype(h_ref.dtype)


def _norm1(x, g, modl):
    n_b = x.shape[0]
    tm = 768
    mb, mc = _mod_specs(n_b, 2)
    return pl.pallas_call(
        functools.partial(_norm_kernel, tm=tm),
        grid=(n_b, T_ALL // tm),
        in_specs=[pl.BlockSpec((1, tm, D_MODEL), lambda b, j: (b, j, 0)),
                  pl.BlockSpec((1, D_MODEL), lambda b, j: (0, 0)), mb, mc],
        out_specs=pl.BlockSpec((1, tm, D_MODEL), lambda b, j: (b, j, 0)),
        out_shape=jax.ShapeDtypeStruct(x.shape, BF16),
        compiler_params=_cparams(("parallel", "parallel")),
        name="norm1",
    )(x, g.reshape(1, D_MODEL), modl, modl)


def _mm_kernel(a_ref, w_ref, o_ref):
    o_ref[...] = _dot(a_ref[...], w_ref[...]).astype(o_ref.dtype)


def _mm(a, w, out_dtype, tm, tn, name):
    m, k = a.shape
    n = w.shape[1]
    return pl.pallas_call(
        _mm_kernel,
        grid=(m // tm, n // tn),
        in_specs=[pl.BlockSpec((tm, k), lambda i, j: (i, 0)),
                  pl.BlockSpec((k, tn), lambda i, j: (0, j))],
        out_specs=pl.BlockSpec((tm, tn), lambda i, j: (i, j)),
        out_shape=jax.ShapeDtypeStruct((m, n), out_dtype),
        compiler_params=_cparams(("parallel", "arbitrary")),
        name=name,
    )(a, w)


LOG2E = float(np.log2(np.e))
SUM_LANE = (HEAD_DIM, 0)


def _with_ones_lane(vblk, hh):
    lane = _lane(vblk.shape)
    keep = (lane < HEAD_DIM) if hh == 0 else (lane >= HEAD_DIM)
    return jnp.where(keep, vblk, jnp.where(lane == SUM_LANE[hh], 1.0, 0.0).astype(vblk.dtype))


def _attend_heads(o_ref, q_of, k_of, v_of, extra=None):
    def scores(h):
        q = q_of(h)
        s = _dot_nt(q, k_of(h))
        if extra is None:
            return (s,)
        return (s + extra[0](h), _dot_nt(q, extra[1](h)))

    def finish(h, sc):
        m = jnp.max(sc[0], axis=-1, keepdims=True)
        for s in sc[1:]:
            m = jnp.maximum(m, jnp.max(s, axis=-1, keepdims=True))
        o = _dot(jnp.exp2(sc[0] - m).astype(BF16), v_of(h))
        if extra is not None:
            o = o + _dot(jnp.exp2(sc[1] - m).astype(BF16), extra[2](h))
        lane = SUM_LANE[h % 2]
        return o / o[:, lane:lane + 1]

    nxt = scores(0)
    even = None
    for h in range(N_HEADS):
        cur = nxt
        if h + 1 < N_HEADS:
            nxt = scores(h + 1)
        o = finish(h, cur)
        if h % 2 == 0:
            even = o
        else:
            pair = jnp.where(_lane(o.shape) < HEAD_DIM, even, o)
            o_ref[0, :, 128 * (h // 2):128 * (h // 2 + 1)] = pair.astype(o_ref.dtype)


def _pad_head(blk, hh):
    return jnp.where(_lane(blk.shape) < HEAD_DIM, blk if hh == 0 else pltpu.roll(blk, HEAD_DIM, 1), 0.0)


def _pad_heads(blk):
    return _pad_head(blk, 0), _pad_head(blk, 1)


def _head_rms(x, gain):
    ms = jnp.sum(x * x, axis=-1, keepdims=True) * (1.0 / HEAD_DIM)
    return x * lax.rsqrt(ms + NORM_EPS) * gain


def _gqa_kernel(pq_ref, pall_ref, cq_ref, sq_ref, call_ref, sall_ref, qg_ref, kg_ref, o_ref, k_scr, v_scr):
    j = pl.program_id(1)

    @pl.when(j == 0)
    def _prep():
        kblk = pall_ref[0, :, 512:640].astype(F32)
        for g, kh in enumerate(_pad_heads(kblk)):
            kh = _rope(_head_rms(kh, kg_ref[...]), call_ref[...], sall_ref[...], 16)
            k_scr[:, 128 * g:128 * (g + 1)] = kh.astype(BF16)
        vblk = pall_ref[0, :, 640:768].astype(F32)
        vrot = pltpu.roll(vblk, HEAD_DIM, 1)
        for i, src in enumerate((vblk, vrot, vrot, vblk)):
            v_scr[:, 128 * i:128 * (i + 1)] = _with_ones_lane(src, i % 2).astype(BF16)

    def attend(n_keys):
        group = N_HEADS // A_KV_HEADS

        def q_of(h):
            qh = _pad_head(pq_ref[0, :, 128 * (h // 2):128 * (h // 2 + 1)].astype(F32), h % 2)
            qh = _rope(_head_rms(qh, qg_ref[...]), cq_ref[...], sq_ref[...], 16)
            return (qh * (HEAD_DIM ** -0.5 * LOG2E)).astype(BF16)

        def v_of(h):
            i = 2 * (h // group) + h % 2
            return v_scr[0:n_keys, 128 * i:128 * (i + 1)]

        _attend_heads(o_ref, q_of, lambda h: k_scr[0:n_keys, 128 * (h // group):128 * (h // group + 1)], v_of)

    @pl.when(j == 0)
    def _ctx():
        attend(CTX_LEN)

    @pl.when(j > 0)
    def _lat():
        attend(T_ALL)


def _gqa(p_a, cos, sin, q_gain, k_gain):
    n_b = p_a.shape[0]
    pad = lambda g: jnp.concatenate([g, jnp.zeros((HEAD_DIM,), F32)]).reshape(1, 128)
    tile = lambda b, j: (b, j, 0)
    whole = lambda b, j: (b, 0, 0)
    return pl.pallas_call(
        _gqa_kernel,
        grid=(n_b, N_QT),
        in_specs=[pl.BlockSpec((1, Q_TILE, A_IN), tile),
                  pl.BlockSpec((1, T_ALL, A_IN), whole),
                  pl.BlockSpec((Q_TILE, 128), lambda b, j: (j, 0)),
                  pl.BlockSpec((Q_TILE, 128), lambda b, j: (j, 0)),
                  pl.BlockSpec((T_ALL, 128), lambda b, j: (0, 0)),
                  pl.BlockSpec((T_ALL, 128), lambda b, j: (0, 0)),
                  pl.BlockSpec((1, 128), lambda b, j: (0, 0)),
                  pl.BlockSpec((1, 128), lambda b, j: (0, 0))],
        out_specs=pl.BlockSpec((1, Q_TILE, 512), tile),
        out_shape=jax.ShapeDtypeStruct((n_b, T_ALL, 512), BF16),
        scratch_shapes=[pltpu.VMEM((T_ALL, 256), BF16), pltpu.VMEM((T_ALL, 512), BF16)],
        compiler_params=_cparams(("parallel", "arbitrary")),
        name="gqa",
    )(p_a, p_a, cos, sin, cos, sin, pad(q_gain), pad(k_gain))


def _mla_kernel(pq_ref, pall_ref, cq_ref, sq_ref, call_ref, sall_ref, qg_ref, kvg_ref, wq_ref, wkv_ref,
                o_ref, k_scr, v_scr):
    j = pl.program_id(1)
    kw = N_HEADS * 128

    @pl.when(j == 0)
    def _prep():
        for i in range(N_QT):
            rows = slice(i * Q_TILE, (i + 1) * Q_TILE)
            ckv = pall_ref[0, rows, B_Q_LORA:B_Q_LORA + B_KV_LORA].astype(F32)
            n = ckv * lax.rsqrt(jnp.mean(ckv * ckv, axis=-1, keepdims=True) + NORM_EPS) * kvg_ref[...]
            kv = _dot(n.astype(BF16), wkv_ref[...])
            kr = _rope(pall_ref[0, rows, 640:768].astype(F32), call_ref[rows, :], sall_ref[rows, :], 8)
            for h in range(N_HEADS):
                k_scr[rows, 128 * h:128 * (h + 1)] = (kv[:, 128 * h:128 * (h + 1)] + kr).astype(BF16)
            for h in range(N_HEADS):
                v_scr[rows, 128 * h:128 * (h + 1)] = _with_ones_lane(kv[:, kw + 128 * h:kw + 128 * (h + 1)], h % 2).astype(BF16)

    def attend(n_keys):
        cq = pq_ref[0, :, 0:B_Q_LORA].astype(F32)
        n = cq * lax.rsqrt(jnp.mean(cq * cq, axis=-1, keepdims=True) + NORM_EPS) * qg_ref[...]
        q = _dot(n.astype(BF16), wq_ref[...])
        scale = (B_NOPE + B_ROPE) ** -0.5 * LOG2E
        _attend_heads(
            o_ref,
            lambda h: (_rope(q[:, 128 * h:128 * (h + 1)], cq_ref[...], sq_ref[...], 8) * scale).astype(BF16),
            lambda h: k_scr[0:n_keys, 128 * h:128 * (h + 1)],
            lambda h: v_scr[0:n_keys, 128 * h:128 * (h + 1)])

    @pl.when(j == 0)
    def _ctx():
        attend(CTX_LEN)

    @pl.when(j > 0)
    def _lat():
        attend(T_ALL)


def _mla(p_b, cos, sin, q_gain, kv_gain, wq, wkv):
    n_b = p_b.shape[0]
    tile = lambda b, j: (b, j, 0)
    whole = lambda b, j: (b, 0, 0)
    const = lambda b, j: (0, 0)
    w = p_b.shape[-1]
    return pl.pallas_call(
        _mla_kernel,
        grid=(n_b, N_QT),
        in_specs=[pl.BlockSpec((1, Q_TILE, w), tile),
                  pl.BlockSpec((1, T_ALL, w), whole),
                  pl.BlockSpec((Q_TILE, 128), lambda b, j: (j, 0)),
                  pl.BlockSpec((Q_TILE, 128), lambda b, j: (j, 0)),
                  pl.BlockSpec((T_ALL, 128), const),
                  pl.BlockSpec((T_ALL, 128), const),
                  pl.BlockSpec((1, B_Q_LORA), const),
                  pl.BlockSpec((1, B_KV_LORA), const),
                  pl.BlockSpec(wq.shape, const),
                  pl.BlockSpec(wkv.shape, const)],
        out_specs=pl.BlockSpec((1, Q_TILE, 512), tile),
        out_shape=jax.ShapeDtypeStruct((n_b, T_ALL, 512), BF16),
        scratch_shapes=[pltpu.VMEM((T_ALL, N_HEADS * 128), BF16), pltpu.VMEM((T_ALL, N_HEADS * 128), BF16)],
        compiler_params=_cparams(("parallel", "arbitrary")),
        name="mla",
    )(p_b, p_b, cos, sin, cos, sin, q_gain.reshape(1, -1), kv_gain.reshape(1, -1), wq, wkv)


def _nat_kernel(pq_ref, pall_ref, bias_ref, o_ref, k_scr, v_scr):
    j = pl.program_id(1)

    @pl.when(j == 0)
    def _prep():
        for jb in range(N_HEADS // 2):
            kblk = pall_ref[0, :, 512 + 128 * jb:512 + 128 * (jb + 1)].astype(F32)
            for hh, kh in enumerate(_pad_heads(kblk)):
                h = 2 * jb + hh
                k_scr[:, 128 * h:128 * (h + 1)] = kh.astype(BF16)
            vblk = pall_ref[0, :, 1024 + 128 * jb:1024 + 128 * (jb + 1)]
            for hh in range(2):
                v_scr[:, 128 * (2 * jb + hh):128 * (2 * jb + hh + 1)] = _with_ones_lane(vblk, hh)

    def q_of(h):
        qh = _pad_head(pq_ref[0, :, 128 * (h // 2):128 * (h // 2 + 1)].astype(F32), h % 2)
        return (qh * (HEAD_DIM ** -0.5 * LOG2E)).astype(BF16)

    hs = lambda h: slice(128 * h, 128 * (h + 1))

    @pl.when(j == 0)
    def _ctx():
        _attend_heads(o_ref, q_of, lambda h: k_scr[0:CTX_LEN, hs(h)], lambda h: v_scr[0:CTX_LEN, hs(h)])

    @pl.when(j > 0)
    def _lat():
        first_row = jnp.clip(NAT_QROWS * (j - 1) - NA_ROWS // 2, 0, N_ROWS - NAT_KROWS)
        start = pl.multiple_of(CTX_LEN + first_row * GRID_W, Q_TILE)
        win = pl.ds(start, NAT_KWIN)
        _attend_heads(o_ref, q_of, lambda h: k_scr[win, hs(h)], lambda h: v_scr[win, hs(h)],
                      extra=(lambda h: bias_ref[0, h],
                             lambda h: k_scr[0:CTX_LEN, hs(h)],
                             lambda h: v_scr[0:CTX_LEN, hs(h)]))


def _nat_bias_table(rel_bias):
    n_dr, n_dc = 2 * NA_ROWS - 1, 2 * NA_COLS - 1
    cols = np.arange(GRID_W)
    pick_col = (cols[None, None, :] - cols[None, :, None] + NA_COLS - 1 == np.arange(n_dc)[:, None, None])
    pick_row = np.zeros((3, NAT_QROWS, NAT_KROWS, n_dr), np.float32)
    valid = np.zeros((3, Q_TILE, NAT_KWIN), bool)
    for t, qb in enumerate((0, 1, N_ROWS // NAT_QROWS - 1)):
        first_row = int(np.clip(NAT_QROWS * qb - NA_ROWS // 2, 0, N_ROWS - NAT_KROWS))
        rq = NAT_QROWS * qb + np.arange(NAT_QROWS)
        rk = first_row + np.arange(NAT_KROWS)
        pick_row[t] = (rk[None, :, None] - rq[:, None, None] + NA_ROWS - 1 == np.arange(n_dr)[None, None, :])
        ql, kl = np.arange(Q_TILE), np.arange(NAT_KWIN)
        r, c = NAT_QROWS * qb + ql // GRID_W, ql % GRID_W
        kr, kc = first_row + kl // GRID_W, kl % GRID_W
        r0 = np.clip(r - NA_ROWS // 2, 0, N_ROWS - NA_ROWS)
        c0 = np.clip(c - NA_COLS // 2, 0, GRID_W - NA_COLS)
        valid[t] = ((kr[None, :] >= r0[:, None]) & (kr[None, :] < r0[:, None] + NA_ROWS)
                    & (kc[None, :] >= c0[:, None]) & (kc[None, :] < c0[:, None] + NA_COLS))
    by_col = jnp.einsum('hdc,cab->hdab', rel_bias.astype(F32), jnp.asarray(pick_col, F32), precision=lax.Precision.HIGHEST)
    tab = jnp.einsum('tqkd,hdab->thqakb', jnp.asarray(pick_row), by_col, precision=lax.Precision.HIGHEST)
    tab = tab.reshape(3, N_HEADS, Q_TILE, NAT_KWIN) * LOG2E
    return jnp.where(jnp.asarray(valid)[:, None], tab, NEG_BIG)


def _nat(p_d, bias_tab):
    n_b = p_d.shape[0]
    tile = lambda b, j: (b, j, 0)
    n_lat_tiles = N_QT - 1

    def bias_idx(b, j):
        qb = j - 1
        return (jnp.where(qb <= 0, 0, jnp.where(qb == n_lat_tiles - 1, 2, 1)), 0, 0, 0)

    return pl.pallas_call(
        _nat_kernel,
        grid=(n_b, N_QT),
        in_specs=[pl.BlockSpec((1, Q_TILE, D_IN), tile),
                  pl.BlockSpec((1, T_ALL, D_IN), lambda b, j: (b, 0, 0)),
                  pl.BlockSpec((1, N_HEADS, Q_TILE, NAT_KWIN), bias_idx)],
        out_specs=pl.BlockSpec((1, Q_TILE, 512), tile),
        out_shape=jax.ShapeDtypeStruct((n_b, T_ALL, 512), BF16),
        scratch_shapes=[pltpu.VMEM((T_ALL, N_HEADS * 128), BF16), pltpu.VMEM((T_ALL, N_HEADS * 128), BF16)],
        compiler_params=_cparams(("parallel", "arbitrary")),
        name="nat",
    )(p_d, p_d, bias_tab)


def _seg_sum(x, ones_bd):
    return _dot3_exact_rhs(x, ones_bd)


def _rwkv_prep_kernel(z_ref, zp_ref, zn_ref, mu_ref, w0_ref, a0_ref, kk_ref, ka_ref, rk_ref,
                      wdh_ref, wdl_ref, wah_ref, wal_ref, wgh_ref, wgl_ref, ones_ref,
                      r_out, v_out, kk_out, kd_out, lw_out, bd_out, bonus_out, g_out):
    j = pl.program_id(1)
    z = z_ref[0]
    tm = z.shape[0]
    row = lax.broadcasted_iota(jnp.int32, (tm, 1), 0)
    prev_row = jnp.where(j <= 1, 0.0, zp_ref[0, 7:8, :])
    next_row = jnp.where((j == 0) | (j == N_QT - 1), 0.0, zn_ref[0, 0:1, :])
    z_prev = jnp.where(row == 0, prev_row, pltpu.roll(z, 1, 0))
    z_next = jnp.where(row == tm - 1, next_row, pltpu.roll(z, tm - 1, 0))
    zs = z + (0.5 * (z_prev + z_next) - z) * mu_ref[...]

    r = zs[:, 0:512]
    k = zs[:, 512:1024]
    v = zs[:, 1024:1536]
    w_lo = zs[:, 1536:1664]
    a_lo = zs[:, 1664:1792]
    g_lo = zs[:, 1792:1920]
    ones_bd = ones_ref[...]

    def lora(x, wh_ref, wl_ref):
        xh, xl = _split_bf16(x)
        return _dot(xh, wh_ref[...]) + _dot(xh, wl_ref[...]) + _dot(xl, wh_ref[...])

    kkr = k * kk_ref[...]
    nrm = jnp.maximum(jnp.sqrt(_seg_sum(kkr * kkr, ones_bd)), 1e-12)
    kk = kkr / nrm
    dec = lora(jnp.tanh(w_lo), wdh_ref, wdl_ref)
    aaa = lora(a_lo, wah_ref, wal_ref)
    g = lora(_sigmoid(g_lo), wgh_ref, wgl_ref)
    r_out[0] = r
    v_out[0] = v
    kk_out[0] = kk
    g_out[0] = g
    ksum = None
    for d in range(2):
        u = -(w0_ref[d:d + 1, :] + dec[:, 512 * d:512 * (d + 1)])
        softplus = jnp.maximum(u, 0.0) + jnp.log(1.0 + jnp.exp(-jnp.abs(u)))
        logw = -softplus - 0.5
        lw_out[d, 0] = -jnp.exp(logw)
        a = _sigmoid(a0_ref[d:d + 1, :] + aaa[:, 512 * d:512 * (d + 1)])
        bd_out[d, 0] = kk * a
        kd = k * (1.0 + (a - 1.0) * ka_ref[...])
        kd_out[d, 0] = kd
        ksum = kd if ksum is None else ksum + kd
    bonus_out[0] = _seg_sum(r * ksum * rk_ref[...], ones_bd) * v


def _rwkv_prep(p_c, lp):
    n_b = p_c.shape[0]
    tile = lambda b, j: (b, j, 0)
    const = lambda b, j: (0, 0)
    blocks8 = T_ALL // 8
    tpb = Q_TILE // 8
    prev = lambda b, j: (b, jnp.maximum(j * tpb - 1, 0), 0)
    nxt = lambda b, j: (b, jnp.minimum((j + 1) * tpb, blocks8 - 1), 0)
    o3 = jax.ShapeDtypeStruct((n_b, T_ALL, C_W), F32)
    o4 = jax.ShapeDtypeStruct((2, n_b, T_ALL, C_W), F32)
    s3 = pl.BlockSpec((1, Q_TILE, C_W), tile)
    s4 = pl.BlockSpec((2, 1, Q_TILE, C_W), lambda b, j: (0, b, j, 0))
    small = [lp['c_mu'].reshape(1, C_IN), lp['c_w0'], lp['c_a0'], lp['c_k_k'].reshape(1, C_W),
             lp['c_k_a'].reshape(1, C_W), lp['c_r_k'].reshape(1, C_W)]
    bd2 = lambda w: jnp.concatenate(
        [jnp.concatenate([w[0], jnp.zeros_like(w[0])], 1), jnp.concatenate([jnp.zeros_like(w[1]), w[1]], 1)], 0)
    mats = []
    for w in (bd2(lp['c_w_decay']), bd2(lp['c_w_aaa']), lp['c_w_gate']):
        mats.extend(_split_bf16(w))
    ones_bd = jnp.asarray(np.kron(np.eye(N_HEADS), np.ones((HEAD_DIM, HEAD_DIM))), BF16)
    ins = small + mats + [ones_bd]
    return pl.pallas_call(
        _rwkv_prep_kernel,
        grid=(n_b, N_QT),
        in_specs=[pl.BlockSpec((1, Q_TILE, C_IN), tile),
                  pl.BlockSpec((1, 8, C_IN), prev),
                  pl.BlockSpec((1, 8, C_IN), nxt)] + [pl.BlockSpec(a.shape, const) for a in ins],
        out_specs=[s3, s3, s3, s4, s4, s4, s3, s3],
        out_shape=[o3, o3, o3, o4, o4, o4, o3, o3],
        compiler_params=_cparams(("parallel", "parallel")),
        name="rwkv_prep",
    )(p_c, p_c, p_c, *ins)


def _dot3_exact_rhs_lhs(m_bf16, a):
    a0 = a.astype(BF16)
    r1 = a - a0.astype(F32)
    a1 = r1.astype(BF16)
    a2 = (r1 - a1.astype(F32)).astype(BF16)
    return _dot(m_bf16, a0) + _dot(m_bf16, a1) + _dot(m_bf16, a2)


N_INV_LEVELS = 6
SCAN_BATCH = 2
MK_STRICT, MK_INCL, MK_LEVEL0 = 0, 1, 2


def _scan_masks():
    n = HG * CHUNK
    ri, ci = np.arange(n)[:, None], np.arange(n)[None, :]
    same = (ri // CHUNK) == (ci // CHUNK)
    masks, cums = [], []
    for fwd in (True, False):
        pos = (lambda t: t % CHUNK) if fwd else (lambda t: CHUNK - 1 - t % CHUNK)
        tr, tc = pos(ri), pos(ci)
        levels = [same & ((tr // (2 * h)) == (tc // (2 * h))) & (((tr // h) % 2) == 1) & (((tc // h) % 2) == 0)
                  for h in (2 ** k for k in range(N_INV_LEVELS))]
        masks.append(np.stack([same & (tr > tc), same & (tr >= tc)] + levels))
        t = pos(np.arange(CHUNK))
        cums.append(t[:, None] >= t[None, :])
    return jnp.asarray(np.stack(masks), F32), jnp.asarray(np.stack(cums), BF16)


def _rwkv_scan_kernel(rf_ref, vf_ref, kkf_ref, kdf_ref, lwf_ref, bdf_ref,
                      rb_ref, vb_ref, kkb_ref, kdb_ref, lwb_ref, bdb_ref, mk_ref, cum_ref, yf_ref, yb_ref, s_scr):
    j = pl.program_id(1)
    n = HG * CHUNK
    eye = (lax.broadcasted_iota(jnp.int32, (n, n), 0) == lax.broadcasted_iota(jnp.int32, (n, n), 1)).astype(F32)
    head_of_lane = lax.broadcasted_iota(jnp.int32, (CHUNK, HGW), 1) // HEAD_DIM
    dir_refs = ((rf_ref, vf_ref, kkf_ref, kdf_ref, lwf_ref, bdf_ref, yf_ref),
                (rb_ref, vb_ref, kkb_ref, kdb_ref, lwb_ref, bdb_ref, yb_ref))
    chunks_per_tile = Q_TILE // CHUNK

    def masked(d, k, x):
        return jnp.where(mk_ref[d, k] > 0.5, x, 0.0)

    def stack(x):
        return jnp.concatenate([jnp.where(head_of_lane == p, x, jnp.zeros_like(x)) for p in range(HG)], axis=0)

    @pl.when(j == 0)
    def _init():
        s_scr[...] = jnp.zeros_like(s_scr)

    def body(i, carry):
        ch = []
        for bi in range(SCAN_BATCH):
            for d in range(2):
                r_ref, v_ref, kk_ref, kd_ref, lw_ref, bd_ref, y_ref = dir_refs[d]
                c = i if d == 0 else chunks_per_tile - 1 - i
                rows = pl.ds(pl.multiple_of(c * CHUNK, CHUNK), CHUNK)
                lw = lw_ref[0, bi, rows, :]
                cum = _dot3_exact_rhs_lhs(cum_ref[d], lw)
                total = jnp.sum(lw, axis=0, keepdims=True)
                e_pos, e_neg, e_prev, e_rest = jnp.exp(cum), jnp.exp(-cum), jnp.exp(cum - lw), jnp.exp(total - cum)
                p_end = jnp.broadcast_to(jnp.exp(total), (CHUNK, C_W))
                for g in range(C_W // HGW):
                    ln = slice(g * HGW, (g + 1) * HGW)
                    kk, bd, kd = kk_ref[bi, rows, ln], bd_ref[0, bi, rows, ln], kd_ref[0, bi, rows, ln]
                    ch.append(dict(
                        idx=(2 * bi + d) * (C_W // HGW) + g, d=d, bi=bi, rows=rows, ln=ln, y_ref=y_ref,
                        al4b=stack((kk * e_prev[:, ln]).astype(BF16)), be4b=stack((bd * e_neg[:, ln]).astype(BF16)),
                        ka4b=stack((kd * e_neg[:, ln]).astype(BF16)),
                        rh4b=stack((r_ref[bi, rows, ln] * e_pos[:, ln]).astype(BF16)),
                        bee4t=stack(bd * e_rest[:, ln]).T.astype(BF16),
                        kae4t=stack(kd * e_rest[:, ln]).T.astype(BF16),
                        p_end_t=stack(p_end[:, ln]).T,
                        v4=stack(v_ref[bi, rows, ln].astype(BF16))))

        def stage(fn):
            for c_ in ch:
                c_.update(fn(c_, c_['d']))

        stage(lambda c_, d: dict(l_ab=masked(d, MK_STRICT, _dot_nt(c_['al4b'], c_['be4b']))))
        stage(lambda c_, d: dict(l_ak=masked(d, MK_STRICT, _dot_nt(c_['al4b'], c_['ka4b'])).astype(BF16)))
        fillers = [
            lambda c_, d: dict(m_rb=masked(d, MK_INCL, _dot_nt(c_['rh4b'], c_['be4b'])).astype(BF16)),
            lambda c_, d: dict(m_rk=masked(d, MK_INCL, _dot_nt(c_['rh4b'], c_['ka4b'])).astype(BF16)),
            lambda c_, d: dict(lv=_dot(c_['l_ak'], c_['v4']).astype(BF16)),
            lambda c_, d: dict(hv=_dot(c_['kae4t'], c_['v4'])),
            lambda c_, d: dict(yv=_dot(c_['m_rk'], c_['v4'])),
        ]
        stage(lambda c_, d: dict(tb=(eye - masked(d, MK_LEVEL0, c_['l_ab'])).astype(BF16)))
        for level in range(1, N_INV_LEVELS):
            stage(lambda c_, d: dict(
                t=_dot(masked(d, MK_LEVEL0 + level, c_['l_ab']).astype(BF16), c_['tb']).astype(BF16)))
            stage(fillers[level - 1])
            stage(lambda c_, d: dict(tb=c_['tb'] - _dot(c_['tb'], c_['t']).astype(BF16)))
        stage(lambda c_, d: dict(w4b=_dot(c_['tb'], c_['al4b']).astype(BF16)))
        stage(lambda c_, d: dict(u04=_dot(c_['tb'], c_['lv'])))
        stage(lambda c_, d: dict(s=s_scr[c_['idx']]))
        stage(lambda c_, d: dict(sb=c_['s'].astype(BF16)))
        stage(lambda c_, d: dict(ub=(_dot(c_['w4b'], c_['sb']) + c_['u04']).astype(BF16)))
        stage(lambda c_, d: dict(y4=_dot(c_['rh4b'], c_['sb']) - _dot(c_['m_rb'], c_['ub']) + c_['yv']))
        for c_ in ch:
            s_scr[c_['idx']] = c_['s'] * c_['p_end_t'] - _dot(c_['bee4t'], c_['ub']) + c_['hv']
        for c_ in ch:
            y4 = c_['y4']
            y = y4[0:CHUNK]
            for p in range(1, HG):
                y = y + y4[p * CHUNK:(p + 1) * CHUNK]
            c_['y_ref'][c_['bi'], c_['rows'], c_['ln']] = y
        return carry

    lax.fori_loop(0, chunks_per_tile, body, 0)


def _rwkv_scan(r, v, kk, kd, lw, bd):
    n_b = r.shape[0]
    assert n_b % SCAN_BATCH == 0
    bwd_tile = lambda j: jnp.where(j == 0, 0, N_QT - j)
    f3 = pl.BlockSpec((SCAN_BATCH, Q_TILE, C_W), lambda b, j: (b, j, 0))
    f4 = pl.BlockSpec((1, SCAN_BATCH, Q_TILE, C_W), lambda b, j: (0, b, j, 0))
    b3 = pl.BlockSpec((SCAN_BATCH, Q_TILE, C_W), lambda b, j: (b, bwd_tile(j), 0))
    b4 = pl.BlockSpec((1, SCAN_BATCH, Q_TILE, C_W), lambda b, j: (1, b, bwd_tile(j), 0))
    y_shape = jax.ShapeDtypeStruct((n_b, T_ALL, C_W), F32)
    masks, cums = _scan_masks()
    return pl.pallas_call(
        _rwkv_scan_kernel,
        grid=(n_b // SCAN_BATCH, N_QT),
        in_specs=[f3, f3, f3, f4, f4, f4, b3, b3, b3, b4, b4, b4,
                  pl.BlockSpec(masks.shape, lambda b, j: (0, 0, 0, 0)),
                  pl.BlockSpec(cums.shape, lambda b, j: (0, 0, 0))],
        out_specs=[f3, b3],
        out_shape=[y_shape, y_shape],
        scratch_shapes=[pltpu.VMEM((SCAN_BATCH * 2 * (C_W // HGW), HGW, HGW), F32)],
        compiler_params=_cparams(("parallel", "arbitrary")),
        name="rwkv_scan",
    )(r, v, kk, kd, lw, bd, r, v, kk, kd, lw, bd, masks, cums)


def _rwkv_out_kernel(yf_ref, yb_ref, bonus_ref, g_ref, gw_ref, gb_ref, ones_ref, o_ref):
    y = yf_ref[0] + yb_ref[0]
    ones_bd = ones_ref[...]
    mean = _seg_sum(y, ones_bd) * (1.0 / HEAD_DIM)
    yc = y - mean
    var = _seg_sum(yc * yc, ones_bd) * (1.0 / HEAD_DIM)
    yn = yc * lax.rsqrt(var + C_GN_EPS) * gw_ref[...] + gb_ref[...]
    o_ref[0] = ((yn + bonus_ref[0]) * g_ref[0]).astype(o_ref.dtype)


def _rwkv_out(y_f, y_b, bonus, g, gn_w, gn_b):
    n_b = bonus.shape[0]
    tm = 768
    tile = lambda b, j: (b, j, 0)
    const = lambda b, j: (0, 0)
    ones_bd = jnp.asarray(np.kron(np.eye(N_HEADS), np.ones((HEAD_DIM, HEAD_DIM))), BF16)
    return pl.pallas_call(
        _rwkv_out_kernel,
        grid=(n_b, T_ALL // tm),
        in_specs=[pl.BlockSpec((1, tm, C_W), tile), pl.BlockSpec((1, tm, C_W), tile),
                  pl.BlockSpec((1, tm, C_W), tile), pl.BlockSpec((1, tm, C_W), tile),
                  pl.BlockSpec((1, C_W), const), pl.BlockSpec((1, C_W), const),
                  pl.BlockSpec((C_W, C_W), const)],
        out_specs=pl.BlockSpec((1, tm, C_W), tile),
        out_shape=jax.ShapeDtypeStruct((n_b, T_ALL, C_W), BF16),
        compiler_params=_cparams(("parallel", "parallel")),
        name="rwkv_out",
    )(y_f, y_b, bonus, g, gn_w.reshape(1, C_W), gn_b.reshape(1, C_W), ones_bd)


def _merge_kernel(x_ref, oa_ref, ob_ref, oc_ref, od_ref, gate_ref, wb_ref, wo_ref, mb_ref, mc_ref, out_ref, *, tm):
    is_ctx = _is_ctx_rows(tm, 1)
    y = None
    for i, o_ref in enumerate((oa_ref, ob_ref, oc_ref, od_ref)):
        z = _dot(o_ref[0], wb_ref[i])
        sg = _sigmoid(gate_ref[0, :, i * D_MODEL:(i + 1) * D_MODEL].astype(F32))
        y = sg * z if y is None else y + sg * z
    z = _dot(y.astype(BF16), wo_ref[...])
    out_ref[0] = x_ref[0] + _mod_vec(mb_ref, mc_ref, 2, is_ctx) * z


def _merge(x, outs, gates, wb, wo, modl):
    n_b = x.shape[0]
    tm = 768
    tile = lambda b, j: (b, j, 0)
    mb, mc = _mod_specs(n_b, 2)
    o_spec = pl.BlockSpec((1, tm, 512), tile)
    return pl.pallas_call(
        functools.partial(_merge_kernel, tm=tm),
        grid=(n_b, T_ALL // tm),
        in_specs=[pl.BlockSpec((1, tm, D_MODEL), tile), o_spec, o_spec, o_spec, o_spec,
                  pl.BlockSpec((1, tm, GATE_IN), tile),
                  pl.BlockSpec(wb.shape, lambda b, j: (0, 0, 0)),
                  pl.BlockSpec(wo.shape, lambda b, j: (0, 0)), mb, mc],
        out_specs=pl.BlockSpec((1, tm, D_MODEL), tile),
        out_shape=jax.ShapeDtypeStruct(x.shape, F32),
        compiler_params=_cparams(("parallel", "parallel")),
        name="merge",
    )(x, *outs, gates, wb, wo, modl, modl)


def _mlp_kernel(x_ref, g_ref, mb_ref, mc_ref, w1_ref, w2_ref, out_ref, h_scr, acc_scr, *, tm, n_f):
    f = pl.program_id(2)
    is_ctx = _is_ctx_rows(tm, 1)

    @pl.when(f == 0)
    def _init():
        h = _norm_mod(x_ref[0], g_ref[...], _mod_vec(mb_ref, mc_ref, 3, is_ctx), _mod_vec(mb_ref, mc_ref, 4, is_ctx))
        h_scr[...] = h.astype(BF16)
        acc_scr[...] = jnp.zeros_like(acc_scr)

    a = jnp.square(jnp.maximum(_dot(h_scr[...], w1_ref[...]), 0.0))
    acc_scr[...] += _dot(a.astype(BF16), w2_ref[...])

    @pl.when(f == n_f - 1)
    def _fin():
        out_ref[0] = x_ref[0] + _mod_vec(mb_ref, mc_ref, 5, is_ctx) * acc_scr[...]


def _mlp(x, g, w1, w2, modl):
    n_b = x.shape[0]
    tm, tf = 1152, 1024
    n_f = D_FF // tf
    tile = lambda b, j, f: (b, j, 0)
    mb, mc = _mod_specs(n_b, 3)
    return pl.pallas_call(
        functools.partial(_mlp_kernel, tm=tm, n_f=n_f),
        grid=(n_b, T_ALL // tm, n_f),
        in_specs=[pl.BlockSpec((1, tm, D_MODEL), tile),
                  pl.BlockSpec((1, D_MODEL), lambda b, j, f: (0, 0)), mb, mc,
                  pl.BlockSpec((D_MODEL, tf), lambda b, j, f: (0, f)),
                  pl.BlockSpec((tf, D_MODEL), lambda b, j, f: (f, 0))],
        out_specs=pl.BlockSpec((1, tm, D_MODEL), tile),
        out_shape=jax.ShapeDtypeStruct(x.shape, F32),
        scratch_shapes=[pltpu.VMEM((tm, D_MODEL), BF16), pltpu.VMEM((tm, D_MODEL), F32)],
        compiler_params=_cparams(("parallel", "parallel", "arbitrary")),
        name="mlp",
    )(x, g.reshape(1, D_MODEL), modl, modl, w1, w2)


def _final_kernel(x_ref, g_ref, o_ref):
    x = x_ref[0]
    o_ref[0] = x * lax.rsqrt(jnp.mean(x * x, axis=-1, keepdims=True) + NORM_EPS) * g_ref[...]


def _final_norm(x, g):
    n_b = x.shape[0]
    tm = Q_TILE
    return pl.pallas_call(
        _final_kernel,
        grid=(n_b, SEQ // tm),
        in_specs=[pl.BlockSpec((1, tm, D_MODEL), lambda b, j: (b, j + CTX_LEN // tm, 0)),
                  pl.BlockSpec((1, D_MODEL), lambda b, j: (0, 0))],
        out_specs=pl.BlockSpec((1, tm, D_MODEL), lambda b, j: (b, j, 0)),
        out_shape=jax.ShapeDtypeStruct((n_b, SEQ, D_MODEL), F32),
        compiler_params=_cparams(("parallel", "parallel")),
        name="final_norm",
    )(x, g.reshape(1, D_MODEL))


def _rope_tables(half, lane0):
    t = np.arange(SEQ)
    inv = ROPE_THETA ** (-np.arange(half, dtype=np.float64) / half)
    cos = np.ones((T_ALL, V7X_LANES), np.float64)
    sin = np.zeros((T_ALL, V7X_LANES), np.float64)
    for part, pos in enumerate((t // GRID_W, t % GRID_W)):
        ang = pos[:, None].astype(np.float64) * inv[None, :]
        ang = ang.astype(np.float32).astype(np.float64)
        base = lane0 + 2 * half * part
        cos[CTX_LEN:, base:base + half] = np.cos(ang)
        cos[CTX_LEN:, base + half:base + 2 * half] = np.cos(ang)
        sin[CTX_LEN:, base:base + half] = -np.sin(ang)
        sin[CTX_LEN:, base + half:base + 2 * half] = np.sin(ang)
    return jnp.asarray(cos, F32), jnp.asarray(sin, F32)


def _layer_weights(l, w_in, b_w_q_up, b_w_kv_up, w_branch, w_out, w_mlp1, w_mlp2):
    wi = w_in[l]
    o_b = A_IN
    o_c = o_b + B_IN
    o_d = o_c + C_IN
    o_g = o_d + D_IN
    w_a = wi[:, :o_b]
    wb_raw = wi[:, o_b:o_c]
    z = lambda n: jnp.zeros((D_MODEL, n), F32)
    w_b = jnp.concatenate([wb_raw[:, :B_Q_LORA + B_KV_LORA], z(64), wb_raw[:, B_Q_LORA + B_KV_LORA:], z(32)], 1)
    wq = b_w_q_up[l].reshape(B_Q_LORA, N_HEADS, B_NOPE + B_ROPE)
    wq = jnp.concatenate([wq, jnp.zeros((B_Q_LORA, N_HEADS, 128 - B_NOPE - B_ROPE), F32)], -1)
    wkv = b_w_kv_up[l].reshape(B_KV_LORA, N_HEADS, 2 * HEAD_DIM)
    zk = jnp.zeros((B_KV_LORA, N_HEADS, HEAD_DIM), F32)
    wk = jnp.concatenate([wkv[:, :, :B_NOPE], zk], -1)
    even = (jnp.arange(N_HEADS) % 2 == 0)[None, :, None]
    wv = jnp.concatenate([jnp.where(even, wkv[:, :, B_NOPE:], 0.0), jnp.where(even, 0.0, wkv[:, :, B_NOPE:])], -1)
    return dict(
        w_a=w_a.astype(BF16), w_b=w_b.astype(BF16), w_c=wi[:, o_c:o_d].astype(BF16),
        w_d=wi[:, o_d:o_g].astype(BF16), w_g=wi[:, o_g:].astype(BF16),
        wq=wq.reshape(B_Q_LORA, N_HEADS * 128).astype(BF16),
        wkv=jnp.concatenate([wk.reshape(B_KV_LORA, -1), wv.reshape(B_KV_LORA, -1)], 1).astype(BF16),
        w_branch=w_branch[l].astype(BF16), w_out=w_out[l].astype(BF16),
        w_mlp1=w_mlp1[l].astype(BF16), w_mlp2=w_mlp2[l].astype(BF16))


def kernel(x, c, ctx, c_ctx, w_ada, b_ada, g_norm1, g_norm2, w_in, a_q_gain, a_k_gain, b_q_gain, b_kv_gain,
           b_w_q_up, b_w_kv_up, c_mu, c_w0, c_w_decay, c_a0, c_w_aaa, c_w_gate, c_k_k, c_k_a, c_r_k,
           c_gn_w, c_gn_b, d_rel_bias, w_branch, w_out, w_mlp1, w_mlp2, g_final):
    n_b = x.shape[0]
    assert x.shape[1:] == (SEQ, D_MODEL) and ctx.shape[1:] == (CTX_LEN, D_MODEL)
    m = n_b * T_ALL
    mod_rows = ((n_b + 1 + 7) // 8) * 8
    cc = jnp.concatenate([c, c_ctx[None, :], jnp.zeros((mod_rows - n_b - 1, D_MODEL), F32)], 0)
    mod_all = _ada(cc, w_ada, b_ada)
    cos_a, sin_a = _rope_tables(16, 0)
    cos_b, sin_b = _rope_tables(8, B_NOPE)
    xs = jnp.concatenate([ctx, x], axis=1)
    for l in range(DEPTH):
        lw = _layer_weights(l, w_in, b_w_q_up, b_w_kv_up, w_branch, w_out, w_mlp1, w_mlp2)
        lp = dict(c_mu=c_mu[l], c_w0=c_w0[l], c_w_decay=c_w_decay[l], c_a0=c_a0[l], c_w_aaa=c_w_aaa[l],
                  c_w_gate=c_w_gate[l], c_k_k=c_k_k[l], c_k_a=c_k_a[l], c_r_k=c_r_k[l])
        modl = mod_all[l].reshape(mod_rows, 1, 6 * D_MODEL)
        h = _norm1(xs, g_norm1[l], modl).reshape(m, D_MODEL)
        p_a = _mm(h, lw['w_a'], BF16, T_ALL, A_IN, "w_in_a").reshape(n_b, T_ALL, A_IN)
        p_b = _mm(h, lw['w_b'], BF16, T_ALL, 768, "w_in_b").reshape(n_b, T_ALL, 768)
        p_c = _mm(h, lw['w_c'], F32, T_ALL, 640, "w_in_c").reshape(n_b, T_ALL, C_IN)
        p_d = _mm(h, lw['w_d'], BF16, T_ALL, 768, "w_in_d").reshape(n_b, T_ALL, D_IN)
        p_g = _mm(h, lw['w_g'], BF16, T_ALL, 1024, "w_in_g").reshape(n_b, T_ALL, GATE_IN)
        o_a = _gqa(p_a, cos_a, sin_a, a_q_gain[l], a_k_gain[l])
        o_b = _mla(p_b, cos_b, sin_b, b_q_gain[l], b_kv_gain[l], lw['wq'], lw['wkv'])
        r, v, kk, kd, lwd, bd, bonus, g = _rwkv_prep(p_c, lp)
        y_f, y_b = _rwkv_scan(r, v, kk, kd, lwd, bd)
        o_c = _rwkv_out(y_f, y_b, bonus, g, c_gn_w[l], c_gn_b[l])
        o_d = _nat(p_d, _nat_bias_table(d_rel_bias[l]))
        xs = _merge(xs, (o_a, o_b, o_c, o_d), p_g, lw['w_branch'], lw['w_out'], modl)
        xs = _mlp(xs, g_norm2[l], lw['w_mlp1'], lw['w_mlp2'], modl)
    return _final_norm(xs, g_final)
```

```python
import functools

import numpy as np
import jax
import jax.numpy as jnp
from jax import lax
from jax.experimental import pallas as pl
from jax.experimental.pallas import tpu as pltpu

F32 = jnp.float32
BF16 = jnp.bfloat16

D_MODEL = 1024
SEQ = 2048
DEPTH = 2
GRID_W = 64
N_ROWS = SEQ // GRID_W
CTX_LEN = 256
T_ALL = CTX_LEN + SEQ
HEAD_DIM = 64
ROPE_THETA = 10000.0
NORM_EPS = 1e-6
N_HEADS = 8
A_KV_HEADS = 2
B_Q_LORA = 384
B_KV_LORA = 256
B_NOPE = 64
B_ROPE = 32
C_W = 512
C_GN_EPS = 64e-5
NA_ROWS = 8
NA_COLS = 16
D_FF = 4 * D_MODEL
A_IN = 768
B_IN = 672
C_IN = 1920
D_IN = 1536
GATE_IN = 4096

V7X_LANES = 128
V7X_VMEM_LIMIT = 56 * 1024 * 1024

Q_TILE = 256
N_QT = T_ALL // Q_TILE
NAT_QROWS = Q_TILE // GRID_W
NAT_KROWS = 12
NAT_KWIN = NAT_KROWS * GRID_W
CHUNK = 64
N_CHUNK = T_ALL // CHUNK
N_CTX_CHUNK = CTX_LEN // CHUNK
HG = 4
HGW = HG * HEAD_DIM
NEG_BIG = -1e30


def _cparams(sem, vmem=V7X_VMEM_LIMIT):
    return pltpu.CompilerParams(dimension_semantics=sem, vmem_limit_bytes=vmem)


def _split_bf16(a):
    hi = a.astype(BF16)
    lo = (a - hi.astype(F32)).astype(BF16)
    return hi, lo


def _dot(a, b):
    return jnp.dot(a, b, preferred_element_type=F32)


def _dot_nt(a, b):
    return lax.dot_general(a, b, (((1,), (1,)), ((), ())), preferred_element_type=F32)


def _dot3(a, b):
    ah, al = _split_bf16(a)
    bh, bl = _split_bf16(b)
    return _dot(ah, bh) + _dot(ah, bl) + _dot(al, bh)


def _dot3_exact_rhs(a, b_bf16):
    a0 = a.astype(BF16)
    r1 = a - a0.astype(F32)
    a1 = r1.astype(BF16)
    a2 = (r1 - a1.astype(F32)).astype(BF16)
    return _dot(a0, b_bf16) + _dot(a1, b_bf16) + _dot(a2, b_bf16)


def _sigmoid(x):
    return 1.0 / (1.0 + jnp.exp(-x))


def _lane(shape):
    return lax.broadcasted_iota(jnp.int32, shape, len(shape) - 1)


def _rope(x, cos, sin, half):
    n = x.shape[-1]
    lo = (_lane(x.shape) % (2 * half)) < half
    partner = jnp.where(lo, pltpu.roll(x, n - half, 1), pltpu.roll(x, half, 1))
    return x * cos + partner * sin


def _mod_vec(mb_ref, mc_ref, k, is_ctx):
    lat = mb_ref[0, :, k * D_MODEL:(k + 1) * D_MODEL]
    ctx = mc_ref[0, :, k * D_MODEL:(k + 1) * D_MODEL]
    return jnp.where(is_ctx, ctx, lat)


def _is_ctx_rows(tm, tile_axis):
    row = pl.program_id(tile_axis) * tm + lax.broadcasted_iota(jnp.int32, (tm, 1), 0)
    return row < CTX_LEN


def _norm_mod(x, g, shift, scale):
    y = x * lax.rsqrt(jnp.mean(x * x, axis=-1, keepdims=True) + NORM_EPS) * g
    return y * (1.0 + scale) + shift


def _ada_kernel(c_ref, w_ref, b_ref, o_ref):
    c = c_ref[...]
    s = c * _sigmoid(c)
    o_ref[0] = _dot3(s, w_ref[0]) + b_ref[0]


def _ada(cc, w_ada, b_ada):
    n_l, _, n_out = w_ada.shape
    tn = 1536
    rows = cc.shape[0]
    return pl.pallas_call(
        _ada_kernel,
        grid=(n_l, n_out // tn),
        in_specs=[pl.BlockSpec((rows, D_MODEL), lambda l, j: (0, 0)),
                  pl.BlockSpec((1, D_MODEL, tn), lambda l, j: (l, 0, j)),
                  pl.BlockSpec((1, 1, tn), lambda l, j: (l, 0, j))],
        out_specs=pl.BlockSpec((1, rows, tn), lambda l, j: (l, 0, j)),
        out_shape=jax.ShapeDtypeStruct((n_l, rows, n_out), F32),
        compiler_params=_cparams(("arbitrary", "arbitrary")),
        name="ada",
    )(cc, w_ada, b_ada.reshape(n_l, 1, n_out))


def _mod_specs(n_b, grid_rank):
    if grid_rank == 2:
        return (pl.BlockSpec((1, 1, 6 * D_MODEL), lambda b, j: (b, 0, 0)),
                pl.BlockSpec((1, 1, 6 * D_MODEL), lambda b, j: (n_b, 0, 0)))
    return (pl.BlockSpec((1, 1, 6 * D_MODEL), lambda b, j, f: (b, 0, 0)),
            pl.BlockSpec((1, 1, 6 * D_MODEL), lambda b, j, f: (n_b, 0, 0)))


def _norm_kernel(x_ref, g_ref, mb_ref, mc_ref, h_ref, *, tm):
    is_ctx = _is_ctx_rows(tm, 1)
    h = _norm_mod(x_ref[0], g_ref[...], _mod_vec(mb_ref, mc_ref, 0, is_ctx), _mod_vec(mb_ref, mc_ref, 1, is_ctx))
    h_ref[0] = h.astype(h_ref.dtype)


def _norm1(x, g, modl):
    n_b = x.shape[0]
    tm = 768
    mb, mc = _mod_specs(n_b, 2)
    return pl.pallas_call(
        functools.partial(_norm_kernel, tm=tm),
        grid=(n_b, T_ALL // tm),
        in_specs=[pl.BlockSpec((1, tm, D_MODEL), lambda b, j: (b, j, 0)),
                  pl.BlockSpec((1, D_MODEL), lambda b, j: (0, 0)), mb, mc],
        out_specs=pl.BlockSpec((1, tm, D_MODEL), lambda b, j: (b, j, 0)),
        out_shape=jax.ShapeDtypeStruct(x.shape, BF16),
        compiler_params=_cparams(("parallel", "parallel")),
        name="norm1",
    )(x, g.reshape(1, D_MODEL), modl, modl)


def _mm_kernel(a_ref, w_ref, o_ref):
    o_ref[...] = _dot(a_ref[...], w_ref[...]).astype(o_ref.dtype)


def _mm(a, w, out_dtype, tm, tn, name):
    m, k = a.shape
    n = w.shape[1]
    return pl.pallas_call(
        _mm_kernel,
        grid=(m // tm, n // tn),
        in_specs=[pl.BlockSpec((tm, k), lambda i, j: (i, 0)),
                  pl.BlockSpec((k, tn), lambda i, j: (0, j))],
        out_specs=pl.BlockSpec((tm, tn), lambda i, j: (i, j)),
        out_shape=jax.ShapeDtypeStruct((m, n), out_dtype),
        compiler_params=_cparams(("parallel", "arbitrary")),
        name=name,
    )(a, w)


LOG2E = float(np.log2(np.e))
SUM_LANE = (HEAD_DIM, 0)


def _with_ones_lane(vblk, hh):
    lane = _lane(vblk.shape)
    keep = (lane < HEAD_DIM) if hh == 0 else (lane >= HEAD_DIM)
    return jnp.where(keep, vblk, jnp.where(lane == SUM_LANE[hh], 1.0, 0.0).astype(vblk.dtype))


def _attend_heads(o_ref, q_of, k_of, v_of, extra=None):
    def scores(h):
        q = q_of(h)
        s = _dot_nt(q, k_of(h))
        if extra is None:
            return (s,)
        return (s + extra[0](h), _dot_nt(q, extra[1](h)))

    def finish(h, sc):
        m = jnp.max(sc[0], axis=-1, keepdims=True)
        for s in sc[1:]:
            m = jnp.maximum(m, jnp.max(s, axis=-1, keepdims=True))
        o = _dot(jnp.exp2(sc[0] - m).astype(BF16), v_of(h))
        if extra is not None:
            o = o + _dot(jnp.exp2(sc[1] - m).astype(BF16), extra[2](h))
        lane = SUM_LANE[h % 2]
        return o / o[:, lane:lane + 1]

    nxt = scores(0)
    even = None
    for h in range(N_HEADS):
        cur = nxt
        if h + 1 < N_HEADS:
            nxt = scores(h + 1)
        o = finish(h, cur)
        if h % 2 == 0:
            even = o
        else:
            pair = jnp.where(_lane(o.shape) < HEAD_DIM, even, o)
            o_ref[0, :, 128 * (h // 2):128 * (h // 2 + 1)] = pair.astype(o_ref.dtype)


def _pad_head(blk, hh):
    return jnp.where(_lane(blk.shape) < HEAD_DIM, blk if hh == 0 else pltpu.roll(blk, HEAD_DIM, 1), 0.0)


def _pad_heads(blk):
    return _pad_head(blk, 0), _pad_head(blk, 1)


def _head_rms(x, gain):
    ms = jnp.sum(x * x, axis=-1, keepdims=True) * (1.0 / HEAD_DIM)
    return x * lax.rsqrt(ms + NORM_EPS) * gain


def _gqa_kernel(pq_ref, pall_ref, cq_ref, sq_ref, call_ref, sall_ref, qg_ref, kg_ref, o_ref, k_scr, v_scr):
    j = pl.program_id(1)

    @pl.when(j == 0)
    def _prep():
        kblk = pall_ref[0, :, 512:640].astype(F32)
        for g, kh in enumerate(_pad_heads(kblk)):
            kh = _rope(_head_rms(kh, kg_ref[...]), call_ref[...], sall_ref[...], 16)
            k_scr[:, 128 * g:128 * (g + 1)] = kh.astype(BF16)
        vblk = pall_ref[0, :, 640:768].astype(F32)
        vrot = pltpu.roll(vblk, HEAD_DIM, 1)
        for i, src in enumerate((vblk, vrot, vrot, vblk)):
            v_scr[:, 128 * i:128 * (i + 1)] = _with_ones_lane(src, i % 2).astype(BF16)

    def attend(n_keys):
        group = N_HEADS // A_KV_HEADS

        def q_of(h):
            qh = _pad_head(pq_ref[0, :, 128 * (h // 2):128 * (h // 2 + 1)].astype(F32), h % 2)
            qh = _rope(_head_rms(qh, qg_ref[...]), cq_ref[...], sq_ref[...], 16)
            return (qh * (HEAD_DIM ** -0.5 * LOG2E)).astype(BF16)

        def v_of(h):
            i = 2 * (h // group) + h % 2
            return v_scr[0:n_keys, 128 * i:128 * (i + 1)]

        _attend_heads(o_ref, q_of, lambda h: k_scr[0:n_keys, 128 * (h // group):128 * (h // group + 1)], v_of)

    @pl.when(j == 0)
    def _ctx():
        attend(CTX_LEN)

    @pl.when(j > 0)
    def _lat():
        attend(T_ALL)


def _gqa(p_a, cos, sin, q_gain, k_gain):
    n_b = p_a.shape[0]
    pad = lambda g: jnp.concatenate([g, jnp.zeros((HEAD_DIM,), F32)]).reshape(1, 128)
    tile = lambda b, j: (b, j, 0)
    whole = lambda b, j: (b, 0, 0)
    return pl.pallas_call(
        _gqa_kernel,
        grid=(n_b, N_QT),
        in_specs=[pl.BlockSpec((1, Q_TILE, A_IN), tile),
                  pl.BlockSpec((1, T_ALL, A_IN), whole),
                  pl.BlockSpec((Q_TILE, 128), lambda b, j: (j, 0)),
                  pl.BlockSpec((Q_TILE, 128), lambda b, j: (j, 0)),
                  pl.BlockSpec((T_ALL, 128), lambda b, j: (0, 0)),
                  pl.BlockSpec((T_ALL, 128), lambda b, j: (0, 0)),
                  pl.BlockSpec((1, 128), lambda b, j: (0, 0)),
                  pl.BlockSpec((1, 128), lambda b, j: (0, 0))],
        out_specs=pl.BlockSpec((1, Q_TILE, 512), tile),
        out_shape=jax.ShapeDtypeStruct((n_b, T_ALL, 512), BF16),
        scratch_shapes=[pltpu.VMEM((T_ALL, 256), BF16), pltpu.VMEM((T_ALL, 512), BF16)],
        compiler_params=_cparams(("parallel", "arbitrary")),
        name="gqa",
    )(p_a, p_a, cos, sin, cos, sin, pad(q_gain), pad(k_gain))


def _mla_kernel(pq_ref, pall_ref, cq_ref, sq_ref, call_ref, sall_ref, qg_ref, kvg_ref, wq_ref, wkv_ref,
                o_ref, k_scr, v_scr):
    j = pl.program_id(1)
    kw = N_HEADS * 128

    @pl.when(j == 0)
    def _prep():
        for i in range(N_QT):
            rows = slice(i * Q_TILE, (i + 1) * Q_TILE)
            ckv = pall_ref[0, rows, B_Q_LORA:B_Q_LORA + B_KV_LORA].astype(F32)
            n = ckv * lax.rsqrt(jnp.mean(ckv * ckv, axis=-1, keepdims=True) + NORM_EPS) * kvg_ref[...]
            kv = _dot(n.astype(BF16), wkv_ref[...])
            kr = _rope(pall_ref[0, rows, 640:768].astype(F32), call_ref[rows, :], sall_ref[rows, :], 8)
            for h in range(N_HEADS):
                k_scr[rows, 128 * h:128 * (h + 1)] = (kv[:, 128 * h:128 * (h + 1)] + kr).astype(BF16)
            for h in range(N_HEADS):
                v_scr[rows, 128 * h:128 * (h + 1)] = _with_ones_lane(kv[:, kw + 128 * h:kw + 128 * (h + 1)], h % 2).astype(BF16)

    def attend(n_keys):
        cq = pq_ref[0, :, 0:B_Q_LORA].astype(F32)
        n = cq * lax.rsqrt(jnp.mean(cq * cq, axis=-1, keepdims=True) + NORM_EPS) * qg_ref[...]
        q = _dot(n.astype(BF16), wq_ref[...])
        scale = (B_NOPE + B_ROPE) ** -0.5 * LOG2E
        _attend_heads(
            o_ref,
            lambda h: (_rope(q[:, 128 * h:128 * (h + 1)], cq_ref[...], sq_ref[...], 8) * scale).astype(BF16),
            lambda h: k_scr[0:n_keys, 128 * h:128 * (h + 1)],
            lambda h: v_scr[0:n_keys, 128 * h:128 * (h + 1)])

    @pl.when(j == 0)
    def _ctx():
        attend(CTX_LEN)

    @pl.when(j > 0)
    def _lat():
        attend(T_ALL)


def _mla(p_b, cos, sin, q_gain, kv_gain, wq, wkv):
    n_b = p_b.shape[0]
    tile = lambda b, j: (b, j, 0)
    whole = lambda b, j: (b, 0, 0)
    const = lambda b, j: (0, 0)
    w = p_b.shape[-1]
    return pl.pallas_call(
        _mla_kernel,
        grid=(n_b, N_QT),
        in_specs=[pl.BlockSpec((1, Q_TILE, w), tile),
                  pl.BlockSpec((1, T_ALL, w), whole),
                  pl.BlockSpec((Q_TILE, 128), lambda b, j: (j, 0)),
                  pl.BlockSpec((Q_TILE, 128), lambda b, j: (j, 0)),
                  pl.BlockSpec((T_ALL, 128), const),
                  pl.BlockSpec((T_ALL, 128), const),
                  pl.BlockSpec((1, B_Q_LORA), const),
                  pl.BlockSpec((1, B_KV_LORA), const),
                  pl.BlockSpec(wq.shape, const),
                  pl.BlockSpec(wkv.shape, const)],
        out_specs=pl.BlockSpec((1, Q_TILE, 512), tile),
        out_shape=jax.ShapeDtypeStruct((n_b, T_ALL, 512), BF16),
        scratch_shapes=[pltpu.VMEM((T_ALL, N_HEADS * 128), BF16), pltpu.VMEM((T_ALL, N_HEADS * 128), BF16)],
        compiler_params=_cparams(("parallel", "arbitrary")),
        name="mla",
    )(p_b, p_b, cos, sin, cos, sin, q_gain.reshape(1, -1), kv_gain.reshape(1, -1), wq, wkv)


def _nat_kernel(pq_ref, pall_ref, bias_ref, o_ref, k_scr, v_scr):
    j = pl.program_id(1)

    @pl.when(j == 0)
    def _prep():
        for jb in range(N_HEADS // 2):
            kblk = pall_ref[0, :, 512 + 128 * jb:512 + 128 * (jb + 1)].astype(F32)
            for hh, kh in enumerate(_pad_heads(kblk)):
                h = 2 * jb + hh
                k_scr[:, 128 * h:128 * (h + 1)] = kh.astype(BF16)
            vblk = pall_ref[0, :, 1024 + 128 * jb:1024 + 128 * (jb + 1)]
            for hh in range(2):
                v_scr[:, 128 * (2 * jb + hh):128 * (2 * jb + hh + 1)] = _with_ones_lane(vblk, hh)

    def q_of(h):
        qh = _pad_head(pq_ref[0, :, 128 * (h // 2):128 * (h // 2 + 1)].astype(F32), h % 2)
        return (qh * (HEAD_DIM ** -0.5 * LOG2E)).astype(BF16)

    hs = lambda h: slice(128 * h, 128 * (h + 1))

    @pl.when(j == 0)
    def _ctx():
        _attend_heads(o_ref, q_of, lambda h: k_scr[0:CTX_LEN, hs(h)], lambda h: v_scr[0:CTX_LEN, hs(h)])

    @pl.when(j > 0)
    def _lat():
        first_row = jnp.clip(NAT_QROWS * (j - 1) - NA_ROWS // 2, 0, N_ROWS - NAT_KROWS)
        start = pl.multiple_of(CTX_LEN + first_row * GRID_W, Q_TILE)
        win = pl.ds(start, NAT_KWIN)
        _attend_heads(o_ref, q_of, lambda h: k_scr[win, hs(h)], lambda h: v_scr[win, hs(h)],
                      extra=(lambda h: bias_ref[0, h],
                             lambda h: k_scr[0:CTX_LEN, hs(h)],
                             lambda h: v_scr[0:CTX_LEN, hs(h)]))


def _nat_bias_table(rel_bias):
    n_dr, n_dc = 2 * NA_ROWS - 1, 2 * NA_COLS - 1
    cols = np.arange(GRID_W)
    pick_col = (cols[None, None, :] - cols[None, :, None] + NA_COLS - 1 == np.arange(n_dc)[:, None, None])
    pick_row = np.zeros((3, NAT_QROWS, NAT_KROWS, n_dr), np.float32)
    valid = np.zeros((3, Q_TILE, NAT_KWIN), bool)
    for t, qb in enumerate((0, 1, N_ROWS // NAT_QROWS - 1)):
        first_row = int(np.clip(NAT_QROWS * qb - NA_ROWS // 2, 0, N_ROWS - NAT_KROWS))
        rq = NAT_QROWS * qb + np.arange(NAT_QROWS)
        rk = first_row + np.arange(NAT_KROWS)
        pick_row[t] = (rk[None, :, None] - rq[:, None, None] + NA_ROWS - 1 == np.arange(n_dr)[None, None, :])
        ql, kl = np.arange(Q_TILE), np.arange(NAT_KWIN)
        r, c = NAT_QROWS * qb + ql // GRID_W, ql % GRID_W
        kr, kc = first_row + kl // GRID_W, kl % GRID_W
        r0 = np.clip(r - NA_ROWS // 2, 0, N_ROWS - NA_ROWS)
        c0 = np.clip(c - NA_COLS // 2, 0, GRID_W - NA_COLS)
        valid[t] = ((kr[None, :] >= r0[:, None]) & (kr[None, :] < r0[:, None] + NA_ROWS)
                    & (kc[None, :] >= c0[:, None]) & (kc[None, :] < c0[:, None] + NA_COLS))
    by_col = jnp.einsum('hdc,cab->hdab', rel_bias.astype(F32), jnp.asarray(pick_col, F32), precision=lax.Precision.HIGHEST)
    tab = jnp.einsum('tqkd,hdab->thqakb', jnp.asarray(pick_row), by_col, precision=lax.Precision.HIGHEST)
    tab = tab.reshape(3, N_HEADS, Q_TILE, NAT_KWIN) * LOG2E
    return jnp.where(jnp.asarray(valid)[:, None], tab, NEG_BIG)


def _nat(p_d, bias_tab):
    n_b = p_d.shape[0]
    tile = lambda b, j: (b, j, 0)
    n_lat_tiles = N_QT - 1

    def bias_idx(b, j):
        qb = j - 1
        return (jnp.where(qb <= 0, 0, jnp.where(qb == n_lat_tiles - 1, 2, 1)), 0, 0, 0)

    return pl.pallas_call(
        _nat_kernel,
        grid=(n_b, N_QT),
        in_specs=[pl.BlockSpec((1, Q_TILE, D_IN), tile),
                  pl.BlockSpec((1, T_ALL, D_IN), lambda b, j: (b, 0, 0)),
                  pl.BlockSpec((1, N_HEADS, Q_TILE, NAT_KWIN), bias_idx)],
        out_specs=pl.BlockSpec((1, Q_TILE, 512), tile),
        out_shape=jax.ShapeDtypeStruct((n_b, T_ALL, 512), BF16),
        scratch_shapes=[pltpu.VMEM((T_ALL, N_HEADS * 128), BF16), pltpu.VMEM((T_ALL, N_HEADS * 128), BF16)],
        compiler_params=_cparams(("parallel", "arbitrary")),
        name="nat",
    )(p_d, p_d, bias_tab)


def _seg_sum(x, ones_bd):
    return _dot3_exact_rhs(x, ones_bd)


def _rwkv_prep_kernel(z_ref, zp_ref, zn_ref, mu_ref, w0_ref, a0_ref, kk_ref, ka_ref, rk_ref,
                      wdh_ref, wdl_ref, wah_ref, wal_ref, wgh_ref, wgl_ref, ones_ref,
                      r_out, v_out, kk_out, kd_out, lw_out, bd_out, bonus_out, g_out):
    j = pl.program_id(1)
    z = z_ref[0]
    tm = z.shape[0]
    row = lax.broadcasted_iota(jnp.int32, (tm, 1), 0)
    prev_row = jnp.where(j <= 1, 0.0, zp_ref[0, 7:8, :])
    next_row = jnp.where((j == 0) | (j == N_QT - 1), 0.0, zn_ref[0, 0:1, :])
    z_prev = jnp.where(row == 0, prev_row, pltpu.roll(z, 1, 0))
    z_next = jnp.where(row == tm - 1, next_row, pltpu.roll(z, tm - 1, 0))
    zs = z + (0.5 * (z_prev + z_next) - z) * mu_ref[...]

    r = zs[:, 0:512]
    k = zs[:, 512:1024]
    v = zs[:, 1024:1536]
    w_lo = zs[:, 1536:1664]
    a_lo = zs[:, 1664:1792]
    g_lo = zs[:, 1792:1920]
    ones_bd = ones_ref[...]

    def lora(x, wh_ref, wl_ref):
        xh, xl = _split_bf16(x)
        return _dot(xh, wh_ref[...]) + _dot(xh, wl_ref[...]) + _dot(xl, wh_ref[...])

    kkr = k * kk_ref[...]
    nrm = jnp.maximum(jnp.sqrt(_seg_sum(kkr * kkr, ones_bd)), 1e-12)
    kk = kkr / nrm
    dec = lora(jnp.tanh(w_lo), wdh_ref, wdl_ref)
    aaa = lora(a_lo, wah_ref, wal_ref)
    g = lora(_sigmoid(g_lo), wgh_ref, wgl_ref)
    r_out[0] = r
    v_out[0] = v
    kk_out[0] = kk
    g_out[0] = g
    ksum = None
    for d in range(2):
        u = -(w0_ref[d:d + 1, :] + dec[:, 512 * d:512 * (d + 1)])
        softplus = jnp.maximum(u, 0.0) + jnp.log(1.0 + jnp.exp(-jnp.abs(u)))
        logw = -softplus - 0.5
        lw_out[d, 0] = -jnp.exp(logw)
        a = _sigmoid(a0_ref[d:d + 1, :] + aaa[:, 512 * d:512 * (d + 1)])
        bd_out[d, 0] = kk * a
        kd = k * (1.0 + (a - 1.0) * ka_ref[...])
        kd_out[d, 0] = kd
        ksum = kd if ksum is None else ksum + kd
    bonus_out[0] = _seg_sum(r * ksum * rk_ref[...], ones_bd) * v


def _rwkv_prep(p_c, lp):
    n_b = p_c.shape[0]
    tile = lambda b, j: (b, j, 0)
    const = lambda b, j: (0, 0)
    blocks8 = T_ALL // 8
    tpb = Q_TILE // 8
    prev = lambda b, j: (b, jnp.maximum(j * tpb - 1, 0), 0)
    nxt = lambda b, j: (b, jnp.minimum((j + 1) * tpb, blocks8 - 1), 0)
    o3 = jax.ShapeDtypeStruct((n_b, T_ALL, C_W), F32)
    o4 = jax.ShapeDtypeStruct((2, n_b, T_ALL, C_W), F32)
    s3 = pl.BlockSpec((1, Q_TILE, C_W), tile)
    s4 = pl.BlockSpec((2, 1, Q_TILE, C_W), lambda b, j: (0, b, j, 0))
    small = [lp['c_mu'].reshape(1, C_IN), lp['c_w0'], lp['c_a0'], lp['c_k_k'].reshape(1, C_W),
             lp['c_k_a'].reshape(1, C_W), lp['c_r_k'].reshape(1, C_W)]
    bd2 = lambda w: jnp.concatenate(
        [jnp.concatenate([w[0], jnp.zeros_like(w[0])], 1), jnp.concatenate([jnp.zeros_like(w[1]), w[1]], 1)], 0)
    mats = []
    for w in (bd2(lp['c_w_decay']), bd2(lp['c_w_aaa']), lp['c_w_gate']):
        mats.extend(_split_bf16(w))
    ones_bd = jnp.asarray(np.kron(np.eye(N_HEADS), np.ones((HEAD_DIM, HEAD_DIM))), BF16)
    ins = small + mats + [ones_bd]
    return pl.pallas_call(
        _rwkv_prep_kernel,
        grid=(n_b, N_QT),
        in_specs=[pl.BlockSpec((1, Q_TILE, C_IN), tile),
                  pl.BlockSpec((1, 8, C_IN), prev),
                  pl.BlockSpec((1, 8, C_IN), nxt)] + [pl.BlockSpec(a.shape, const) for a in ins],
        out_specs=[s3, s3, s3, s4, s4, s4, s3, s3],
        out_shape=[o3, o3, o3, o4, o4, o4, o3, o3],
        compiler_params=_cparams(("parallel", "parallel")),
        name="rwkv_prep",
    )(p_c, p_c, p_c, *ins)


def _dot3_exact_rhs_lhs(m_bf16, a):
    a0 = a.astype(BF16)
    r1 = a - a0.astype(F32)
    a1 = r1.astype(BF16)
    a2 = (r1 - a1.astype(F32)).astype(BF16)
    return _dot(m_bf16, a0) + _dot(m_bf16, a1) + _dot(m_bf16, a2)


N_INV_LEVELS = 6
SCAN_BATCH = 2
SCAN_TILE = 256
MK_STRICT, MK_INCL, MK_LEVEL0 = 0, 1, 2


def _scan_masks():
    masks, cums = [], []
    for fwd in (True, False):
        t = np.arange(CHUNK) if fwd else CHUNK - 1 - np.arange(CHUNK)
        tr, tc = t[:, None], t[None, :]
        levels = [((tr // (2 * h)) == (tc // (2 * h))) & (((tr // h) % 2) == 1) & (((tc // h) % 2) == 0)
                  for h in (2 ** k for k in range(N_INV_LEVELS))]
        masks.append(np.tile(np.stack([tr > tc, tr >= tc] + levels), (1, 1, HG)))
        cums.append(tr >= tc)
    return jnp.asarray(np.stack(masks), F32), jnp.asarray(np.stack(cums), BF16)


def _rwkv_scan_kernel(rf_ref, vf_ref, kkf_ref, kdf_ref, lwf_ref, bdf_ref,
                      rb_ref, vb_ref, kkb_ref, kdb_ref, lwb_ref, bdb_ref, mk_ref, cum_ref, yf_ref, yb_ref, s_scr):
    j = pl.program_id(1)
    lane_head = lax.broadcasted_iota(jnp.int32, (CHUNK, HGW), 1) // HEAD_DIM
    eye = (lax.broadcasted_iota(jnp.int32, (CHUNK, HGW), 0)
           == lax.broadcasted_iota(jnp.int32, (CHUNK, HGW), 1) % CHUNK).astype(F32)
    dir_refs = ((rf_ref, vf_ref, kkf_ref, kdf_ref, lwf_ref, bdf_ref, yf_ref),
                (rb_ref, vb_ref, kkb_ref, kdb_ref, lwb_ref, bdb_ref, yb_ref))
    chunks_per_tile = SCAN_TILE // CHUNK

    def masked(d, k, x):
        return jnp.where(mk_ref[d, k] > 0.5, x, 0.0)

    def stack(x):
        return jnp.concatenate([jnp.where(lane_head == p, x, jnp.zeros_like(x)) for p in range(HG)], axis=0)

    def head_transpose(x):
        t = stack(x).T
        return t[0:HEAD_DIM] + t[HEAD_DIM:2 * HEAD_DIM] + t[2 * HEAD_DIM:3 * HEAD_DIM] + t[3 * HEAD_DIM:4 * HEAD_DIM]

    @pl.when(j == 0)
    def _init():
        s_scr[...] = jnp.zeros_like(s_scr)

    def body(i, carry):
        ch = []
        for bi in range(SCAN_BATCH):
            for d in range(2):
                r_ref, v_ref, kk_ref, kd_ref, lw_ref, bd_ref, y_ref = dir_refs[d]
                c = i if d == 0 else chunks_per_tile - 1 - i
                rows = pl.ds(pl.multiple_of(c * CHUNK, CHUNK), CHUNK)
                lw = lw_ref[0, bi, rows, :]
                cum = _dot3_exact_rhs_lhs(cum_ref[d], lw)
                total = jnp.sum(lw, axis=0, keepdims=True)
                e_pos, e_neg, e_prev, e_rest = jnp.exp(cum), jnp.exp(-cum), jnp.exp(cum - lw), jnp.exp(total - cum)
                p_end = jnp.broadcast_to(jnp.exp(total), (CHUNK, C_W))
                for g in range(C_W // HGW):
                    ln = slice(g * HGW, (g + 1) * HGW)
                    kk, bd, kd = kk_ref[bi, rows, ln], bd_ref[0, bi, rows, ln], kd_ref[0, bi, rows, ln]
                    al = (kk * e_prev[:, ln]).astype(BF16)
                    rh = (r_ref[bi, rows, ln] * e_pos[:, ln]).astype(BF16)
                    ch.append(dict(
                        idx=(2 * bi + d) * (C_W // HGW) + g, d=d, bi=bi, rows=rows, ln=ln, y_ref=y_ref,
                        al=al, rh=rh, al_rh=jnp.concatenate([al, rh], axis=0), al4=stack(al),
                        be4=stack((bd * e_neg[:, ln]).astype(BF16)), ka4=stack((kd * e_neg[:, ln]).astype(BF16)),
                        bee_t=head_transpose(bd * e_rest[:, ln]).astype(BF16),
                        kae_t=head_transpose(kd * e_rest[:, ln]).astype(BF16),
                        p_end_t=head_transpose(p_end[:, ln]),
                        v4=stack(v_ref[bi, rows, ln].astype(BF16))))

        def stage(fn):
            for c_ in ch:
                c_.update(fn(c_, c_['d']))

        def nt_masks(c_, d, key, lo_name, hi_name, lo_f32=False):
            prod = _dot_nt(c_['al_rh'], c_[key])
            lo = masked(d, MK_STRICT, prod[0:CHUNK])
            return {lo_name: lo if lo_f32 else lo.astype(BF16),
                    hi_name: masked(d, MK_INCL, prod[CHUNK:2 * CHUNK]).astype(BF16)}

        stage(lambda c_, d: nt_masks(c_, d, 'be4', 'l_ab', 'm_rb', lo_f32=True))
        stage(lambda c_, d: nt_masks(c_, d, 'ka4', 'l_ak', 'm_rk'))

        def value_products(c_, d):
            prod = _dot(jnp.concatenate([c_['l_ak'], c_['kae_t'], c_['m_rk']], axis=0), c_['v4'])
            return dict(lv4=stack(prod[0:CHUNK].astype(BF16)), hv=prod[CHUNK:2 * CHUNK], yv=prod[2 * CHUNK:3 * CHUNK])

        stage(lambda c_, d: dict(tb=(eye - masked(d, MK_LEVEL0, c_['l_ab'])).astype(BF16)))
        for level in range(1, N_INV_LEVELS):
            stage(lambda c_, d: dict(
                t4=stack(_dot(masked(d, MK_LEVEL0 + level, c_['l_ab']).astype(BF16), stack(c_['tb'])).astype(BF16))))
            if level == 1:
                stage(value_products)
            stage(lambda c_, d: dict(tb=c_['tb'] - _dot(c_['tb'], c_['t4']).astype(BF16)))
        stage(lambda c_, d: dict(w=_dot(c_['tb'], c_['al4']).astype(BF16)))
        stage(lambda c_, d: dict(u0=_dot(c_['tb'], c_['lv4'])))
        stage(lambda c_, d: dict(s=s_scr[c_['idx']]))
        stage(lambda c_, d: dict(on_s=_dot(jnp.concatenate([c_['w'], c_['rh']], axis=0), stack(c_['s'].astype(BF16)))))
        stage(lambda c_, d: dict(u4=stack((c_['on_s'][0:CHUNK] + c_['u0']).astype(BF16))))
        stage(lambda c_, d: dict(on_u=_dot(jnp.concatenate([c_['m_rb'], c_['bee_t']], axis=0), c_['u4'])))
        for c_ in ch:
            s_scr[c_['idx']] = c_['s'] * c_['p_end_t'] - c_['on_u'][CHUNK:2 * CHUNK] + c_['hv']
        for c_ in ch:
            c_['y_ref'][c_['bi'], c_['rows'], c_['ln']] = c_['on_s'][CHUNK:2 * CHUNK] - c_['on_u'][0:CHUNK] + c_['yv']
        return carry

    lax.fori_loop(0, chunks_per_tile, body, 0)


def _rwkv_scan(r, v, kk, kd, lw, bd):
    n_b = r.shape[0]
    assert n_b % SCAN_BATCH == 0
    n_tiles, ctx_tiles = T_ALL // SCAN_TILE, CTX_LEN // SCAN_TILE
    bwd_tile = lambda j: jnp.where(j < ctx_tiles, ctx_tiles - 1 - j, n_tiles + ctx_tiles - 1 - j)
    f3 = pl.BlockSpec((SCAN_BATCH, SCAN_TILE, C_W), lambda b, j: (b, j, 0))
    f4 = pl.BlockSpec((1, SCAN_BATCH, SCAN_TILE, C_W), lambda b, j: (0, b, j, 0))
    b3 = pl.BlockSpec((SCAN_BATCH, SCAN_TILE, C_W), lambda b, j: (b, bwd_tile(j), 0))
    b4 = pl.BlockSpec((1, SCAN_BATCH, SCAN_TILE, C_W), lambda b, j: (1, b, bwd_tile(j), 0))
    y_shape = jax.ShapeDtypeStruct((n_b, T_ALL, C_W), F32)
    masks, cums = _scan_masks()
    return pl.pallas_call(
        _rwkv_scan_kernel,
        grid=(n_b // SCAN_BATCH, n_tiles),
        in_specs=[f3, f3, f3, f4, f4, f4, b3, b3, b3, b4, b4, b4,
                  pl.BlockSpec(masks.shape, lambda b, j: (0, 0, 0, 0)),
                  pl.BlockSpec(cums.shape, lambda b, j: (0, 0, 0))],
        out_specs=[f3, b3],
        out_shape=[y_shape, y_shape],
        scratch_shapes=[pltpu.VMEM((SCAN_BATCH * 2 * (C_W // HGW), HEAD_DIM, HGW), F32)],
        compiler_params=_cparams(("parallel", "arbitrary")),
        name="rwkv_scan",
    )(r, v, kk, kd, lw, bd, r, v, kk, kd, lw, bd, masks, cums)


def _rwkv_out_kernel(yf_ref, yb_ref, bonus_ref, g_ref, gw_ref, gb_ref, ones_ref, o_ref):
    y = yf_ref[0] + yb_ref[0]
    ones_bd = ones_ref[...]
    mean = _seg_sum(y, ones_bd) * (1.0 / HEAD_DIM)
    yc = y - mean
    var = _seg_sum(yc * yc, ones_bd) * (1.0 / HEAD_DIM)
    yn = yc * lax.rsqrt(var + C_GN_EPS) * gw_ref[...] + gb_ref[...]
    o_ref[0] = ((yn + bonus_ref[0]) * g_ref[0]).astype(o_ref.dtype)


def _rwkv_out(y_f, y_b, bonus, g, gn_w, gn_b):
    n_b = bonus.shape[0]
    tm = 768
    tile = lambda b, j: (b, j, 0)
    const = lambda b, j: (0, 0)
    ones_bd = jnp.asarray(np.kron(np.eye(N_HEADS), np.ones((HEAD_DIM, HEAD_DIM))), BF16)
    return pl.pallas_call(
        _rwkv_out_kernel,
        grid=(n_b, T_ALL // tm),
        in_specs=[pl.BlockSpec((1, tm, C_W), tile), pl.BlockSpec((1, tm, C_W), tile),
                  pl.BlockSpec((1, tm, C_W), tile), pl.BlockSpec((1, tm, C_W), tile),
                  pl.BlockSpec((1, C_W), const), pl.BlockSpec((1, C_W), const),
                  pl.BlockSpec((C_W, C_W), const)],
        out_specs=pl.BlockSpec((1, tm, C_W), tile),
        out_shape=jax.ShapeDtypeStruct((n_b, T_ALL, C_W), BF16),
        compiler_params=_cparams(("parallel", "parallel")),
        name="rwkv_out",
    )(y_f, y_b, bonus, g, gn_w.reshape(1, C_W), gn_b.reshape(1, C_W), ones_bd)


def _merge_kernel(x_ref, oa_ref, ob_ref, oc_ref, od_ref, gate_ref, wb_ref, wo_ref, mb_ref, mc_ref, out_ref, *, tm):
    is_ctx = _is_ctx_rows(tm, 1)
    y = None
    for i, o_ref in enumerate((oa_ref, ob_ref, oc_ref, od_ref)):
        z = _dot(o_ref[0], wb_ref[i])
        sg = _sigmoid(gate_ref[0, :, i * D_MODEL:(i + 1) * D_MODEL].astype(F32))
        y = sg * z if y is None else y + sg * z
    z = _dot(y.astype(BF16), wo_ref[...])
    out_ref[0] = x_ref[0] + _mod_vec(mb_ref, mc_ref, 2, is_ctx) * z


def _merge(x, outs, gates, wb, wo, modl):
    n_b = x.shape[0]
    tm = 768
    tile = lambda b, j: (b, j, 0)
    mb, mc = _mod_specs(n_b, 2)
    o_spec = pl.BlockSpec((1, tm, 512), tile)
    return pl.pallas_call(
        functools.partial(_merge_kernel, tm=tm),
        grid=(n_b, T_ALL // tm),
        in_specs=[pl.BlockSpec((1, tm, D_MODEL), tile), o_spec, o_spec, o_spec, o_spec,
                  pl.BlockSpec((1, tm, GATE_IN), tile),
                  pl.BlockSpec(wb.shape, lambda b, j: (0, 0, 0)),
                  pl.BlockSpec(wo.shape, lambda b, j: (0, 0)), mb, mc],
        out_specs=pl.BlockSpec((1, tm, D_MODEL), tile),
        out_shape=jax.ShapeDtypeStruct(x.shape, F32),
        compiler_params=_cparams(("parallel", "parallel")),
        name="merge",
    )(x, *outs, gates, wb, wo, modl, modl)


def _mlp_kernel(x_ref, g_ref, mb_ref, mc_ref, w1_ref, w2_ref, out_ref, h_scr, acc_scr, *, tm, n_f):
    f = pl.program_id(2)
    is_ctx = _is_ctx_rows(tm, 1)

    @pl.when(f == 0)
    def _init():
        h = _norm_mod(x_ref[0], g_ref[...], _mod_vec(mb_ref, mc_ref, 3, is_ctx), _mod_vec(mb_ref, mc_ref, 4, is_ctx))
        h_scr[...] = h.astype(BF16)
        acc_scr[...] = jnp.zeros_like(acc_scr)

    a = jnp.square(jnp.maximum(_dot(h_scr[...], w1_ref[...]), 0.0))
    acc_scr[...] += _dot(a.astype(BF16), w2_ref[...])

    @pl.when(f == n_f - 1)
    def _fin():
        out_ref[0] = x_ref[0] + _mod_vec(mb_ref, mc_ref, 5, is_ctx) * acc_scr[...]


def _mlp(x, g, w1, w2, modl):
    n_b = x.shape[0]
    tm, tf = 1152, 1024
    n_f = D_FF // tf
    tile = lambda b, j, f: (b, j, 0)
    mb, mc = _mod_specs(n_b, 3)
    return pl.pallas_call(
        functools.partial(_mlp_kernel, tm=tm, n_f=n_f),
        grid=(n_b, T_ALL // tm, n_f),
        in_specs=[pl.BlockSpec((1, tm, D_MODEL), tile),
                  pl.BlockSpec((1, D_MODEL), lambda b, j, f: (0, 0)), mb, mc,
                  pl.BlockSpec((D_MODEL, tf), lambda b, j, f: (0, f)),
                  pl.BlockSpec((tf, D_MODEL), lambda b, j, f: (f, 0))],
        out_specs=pl.BlockSpec((1, tm, D_MODEL), tile),
        out_shape=jax.ShapeDtypeStruct(x.shape, F32),
        scratch_shapes=[pltpu.VMEM((tm, D_MODEL), BF16), pltpu.VMEM((tm, D_MODEL), F32)],
        compiler_params=_cparams(("parallel", "parallel", "arbitrary")),
        name="mlp",
    )(x, g.reshape(1, D_MODEL), modl, modl, w1, w2)


def _final_kernel(x_ref, g_ref, o_ref):
    x = x_ref[0]
    o_ref[0] = x * lax.rsqrt(jnp.mean(x * x, axis=-1, keepdims=True) + NORM_EPS) * g_ref[...]


def _final_norm(x, g):
    n_b = x.shape[0]
    tm = Q_TILE
    return pl.pallas_call(
        _final_kernel,
        grid=(n_b, SEQ // tm),
        in_specs=[pl.BlockSpec((1, tm, D_MODEL), lambda b, j: (b, j + CTX_LEN // tm, 0)),
                  pl.BlockSpec((1, D_MODEL), lambda b, j: (0, 0))],
        out_specs=pl.BlockSpec((1, tm, D_MODEL), lambda b, j: (b, j, 0)),
        out_shape=jax.ShapeDtypeStruct((n_b, SEQ, D_MODEL), F32),
        compiler_params=_cparams(("parallel", "parallel")),
        name="final_norm",
    )(x, g.reshape(1, D_MODEL))


def _rope_tables(half, lane0):
    t = np.arange(SEQ)
    inv = ROPE_THETA ** (-np.arange(half, dtype=np.float64) / half)
    cos = np.ones((T_ALL, V7X_LANES), np.float64)
    sin = np.zeros((T_ALL, V7X_LANES), np.float64)
    for part, pos in enumerate((t // GRID_W, t % GRID_W)):
        ang = pos[:, None].astype(np.float64) * inv[None, :]
        ang = ang.astype(np.float32).astype(np.float64)
        base = lane0 + 2 * half * part
        cos[CTX_LEN:, base:base + half] = np.cos(ang)
        cos[CTX_LEN:, base + half:base + 2 * half] = np.cos(ang)
        sin[CTX_LEN:, base:base + half] = -np.sin(ang)
        sin[CTX_LEN:, base + half:base + 2 * half] = np.sin(ang)
    return jnp.asarray(cos, F32), jnp.asarray(sin, F32)


def _layer_weights(l, w_in, b_w_q_up, b_w_kv_up, w_branch, w_out, w_mlp1, w_mlp2):
    wi = w_in[l]
    o_b = A_IN
    o_c = o_b + B_IN
    o_d = o_c + C_IN
    o_g = o_d + D_IN
    w_a = wi[:, :o_b]
    wb_raw = wi[:, o_b:o_c]
    z = lambda n: jnp.zeros((D_MODEL, n), F32)
    w_b = jnp.concatenate([wb_raw[:, :B_Q_LORA + B_KV_LORA], z(64), wb_raw[:, B_Q_LORA + B_KV_LORA:], z(32)], 1)
    wq = b_w_q_up[l].reshape(B_Q_LORA, N_HEADS, B_NOPE + B_ROPE)
    wq = jnp.concatenate([wq, jnp.zeros((B_Q_LORA, N_HEADS, 128 - B_NOPE - B_ROPE), F32)], -1)
    wkv = b_w_kv_up[l].reshape(B_KV_LORA, N_HEADS, 2 * HEAD_DIM)
    zk = jnp.zeros((B_KV_LORA, N_HEADS, HEAD_DIM), F32)
    wk = jnp.concatenate([wkv[:, :, :B_NOPE], zk], -1)
    even = (jnp.arange(N_HEADS) % 2 == 0)[None, :, None]
    wv = jnp.concatenate([jnp.where(even, wkv[:, :, B_NOPE:], 0.0), jnp.where(even, 0.0, wkv[:, :, B_NOPE:])], -1)
    return dict(
        w_a=w_a.astype(BF16), w_b=w_b.astype(BF16), w_c=wi[:, o_c:o_d].astype(BF16),
        w_d=wi[:, o_d:o_g].astype(BF16), w_g=wi[:, o_g:].astype(BF16),
        wq=wq.reshape(B_Q_LORA, N_HEADS * 128).astype(BF16),
        wkv=jnp.concatenate([wk.reshape(B_KV_LORA, -1), wv.reshape(B_KV_LORA, -1)], 1).astype(BF16),
        w_branch=w_branch[l].astype(BF16), w_out=w_out[l].astype(BF16),
        w_mlp1=w_mlp1[l].astype(BF16), w_mlp2=w_mlp2[l].astype(BF16))


def kernel(x, c, ctx, c_ctx, w_ada, b_ada, g_norm1, g_norm2, w_in, a_q_gain, a_k_gain, b_q_gain, b_kv_gain,
           b_w_q_up, b_w_kv_up, c_mu, c_w0, c_w_decay, c_a0, c_w_aaa, c_w_gate, c_k_k, c_k_a, c_r_k,
           c_gn_w, c_gn_b, d_rel_bias, w_branch, w_out, w_mlp1, w_mlp2, g_final):
    n_b = x.shape[0]
    assert x.shape[1:] == (SEQ, D_MODEL) and ctx.shape[1:] == (CTX_LEN, D_MODEL)
    m = n_b * T_ALL
    mod_rows = ((n_b + 1 + 7) // 8) * 8
    cc = jnp.concatenate([c, c_ctx[None, :], jnp.zeros((mod_rows - n_b - 1, D_MODEL), F32)], 0)
    mod_all = _ada(cc, w_ada, b_ada)
    cos_a, sin_a = _rope_tables(16, 0)
    cos_b, sin_b = _rope_tables(8, B_NOPE)
    xs = jnp.concatenate([ctx, x], axis=1)
    for l in range(DEPTH):
        lw = _layer_weights(l, w_in, b_w_q_up, b_w_kv_up, w_branch, w_out, w_mlp1, w_mlp2)
        lp = dict(c_mu=c_mu[l], c_w0=c_w0[l], c_w_decay=c_w_decay[l], c_a0=c_a0[l], c_w_aaa=c_w_aaa[l],
                  c_w_gate=c_w_gate[l], c_k_k=c_k_k[l], c_k_a=c_k_a[l], c_r_k=c_r_k[l])
        modl = mod_all[l].reshape(mod_rows, 1, 6 * D_MODEL)
        h = _norm1(xs, g_norm1[l], modl).reshape(m, D_MODEL)
        p_a = _mm(h, lw['w_a'], BF16, T_ALL, A_IN, "w_in_a").reshape(n_b, T_ALL, A_IN)
        p_b = _mm(h, lw['w_b'], BF16, T_ALL, 768, "w_in_b").reshape(n_b, T_ALL, 768)
        p_c = _mm(h, lw['w_c'], F32, T_ALL, 640, "w_in_c").reshape(n_b, T_ALL, C_IN)
        p_d = _mm(h, lw['w_d'], BF16, T_ALL, 768, "w_in_d").reshape(n_b, T_ALL, D_IN)
        p_g = _mm(h, lw['w_g'], BF16, T_ALL, 1024, "w_in_g").reshape(n_b, T_ALL, GATE_IN)
        o_a = _gqa(p_a, cos_a, sin_a, a_q_gain[l], a_k_gain[l])
        o_b = _mla(p_b, cos_b, sin_b, b_q_gain[l], b_kv_gain[l], lw['wq'], lw['wkv'])
        r, v, kk, kd, lwd, bd, bonus, g = _rwkv_prep(p_c, lp)
        y_f, y_b = _rwkv_scan(r, v, kk, kd, lwd, bd)
        o_c = _rwkv_out(y_f, y_b, bonus, g, c_gn_w[l], c_gn_b[l])
        o_d = _nat(p_d, _nat_bias_table(d_rel_bias[l]))
        xs = _merge(xs, (o_a, o_b, o_c, o_d), p_g, lw['w_branch'], lw['w_out'], modl)
        xs = _mlp(xs, g_norm2[l], lw['w_mlp1'], lw['w_mlp2'], modl)
    return _final_norm(xs, g_final)
```

```python
import functools

import numpy as np
import jax
import jax.numpy as jnp
from jax import lax
from jax.experimental import pallas as pl
from jax.experimental.pallas import tpu as pltpu

F32 = jnp.float32
BF16 = jnp.bfloat16

D_MODEL = 1024
SEQ = 2048
DEPTH = 2
GRID_W = 64
N_ROWS = SEQ // GRID_W
CTX_LEN = 256
T_ALL = CTX_LEN + SEQ
HEAD_DIM = 64
ROPE_THETA = 10000.0
NORM_EPS = 1e-6
N_HEADS = 8
A_KV_HEADS = 2
B_Q_LORA = 384
B_KV_LORA = 256
B_NOPE = 64
B_ROPE = 32
C_W = 512
C_GN_EPS = 64e-5
NA_ROWS = 8
NA_COLS = 16
D_FF = 4 * D_MODEL
A_IN = 768
B_IN = 672
C_IN = 1920
D_IN = 1536
GATE_IN = 4096

V7X_LANES = 128
V7X_VMEM_LIMIT = 56 * 1024 * 1024

Q_TILE = 256
N_QT = T_ALL // Q_TILE
NAT_QROWS = Q_TILE // GRID_W
NAT_KROWS = 12
NAT_KWIN = NAT_KROWS * GRID_W
CHUNK = 64
N_CHUNK = T_ALL // CHUNK
N_CTX_CHUNK = CTX_LEN // CHUNK
HG = 4
HGW = HG * HEAD_DIM
NEG_BIG = -1e30


def _cparams(sem, vmem=V7X_VMEM_LIMIT):
    return pltpu.CompilerParams(dimension_semantics=sem, vmem_limit_bytes=vmem)


def _split_bf16(a):
    hi = a.astype(BF16)
    lo = (a - hi.astype(F32)).astype(BF16)
    return hi, lo


def _dot(a, b):
    return jnp.dot(a, b, preferred_element_type=F32)


def _dot_nt(a, b):
    return lax.dot_general(a, b, (((1,), (1,)), ((), ())), preferred_element_type=F32)


def _dot3(a, b):
    ah, al = _split_bf16(a)
    bh, bl = _split_bf16(b)
    return _dot(ah, bh) + _dot(ah, bl) + _dot(al, bh)


def _split3_bf16(a):
    a0 = a.astype(BF16)
    r1 = a - a0.astype(F32)
    a1 = r1.astype(BF16)
    return a0, a1, (r1 - a1.astype(F32)).astype(BF16)


def _dot3_exact_rhs(a, b3_bf16):
    return _dot(jnp.concatenate(_split3_bf16(a), axis=1), b3_bf16)


def _sigmoid(x):
    return 1.0 / (1.0 + jnp.exp(-x))


def _lane(shape):
    return lax.broadcasted_iota(jnp.int32, shape, len(shape) - 1)


def _rope(x, cos, sin, half):
    n = x.shape[-1]
    lo = (_lane(x.shape) % (2 * half)) < half
    partner = jnp.where(lo, pltpu.roll(x, n - half, 1), pltpu.roll(x, half, 1))
    return x * cos + partner * sin


def _mod_vec(mb_ref, mc_ref, k, is_ctx):
    lat = mb_ref[0, :, k * D_MODEL:(k + 1) * D_MODEL]
    ctx = mc_ref[0, :, k * D_MODEL:(k + 1) * D_MODEL]
    return jnp.where(is_ctx, ctx, lat)


def _is_ctx_rows(tm, tile_axis):
    row = pl.program_id(tile_axis) * tm + lax.broadcasted_iota(jnp.int32, (tm, 1), 0)
    return row < CTX_LEN


def _norm_mod(x, g, shift, scale):
    y = x * lax.rsqrt(jnp.mean(x * x, axis=-1, keepdims=True) + NORM_EPS) * g
    return y * (1.0 + scale) + shift


def _ada_kernel(c_ref, w_ref, b_ref, o_ref):
    c = c_ref[...]
    s = c * _sigmoid(c)
    o_ref[0] = _dot3(s, w_ref[0]) + b_ref[0]


def _ada(cc, w_ada, b_ada):
    n_l, _, n_out = w_ada.shape
    tn = 1536
    rows = cc.shape[0]
    return pl.pallas_call(
        _ada_kernel,
        grid=(n_l, n_out // tn),
        in_specs=[pl.BlockSpec((rows, D_MODEL), lambda l, j: (0, 0)),
                  pl.BlockSpec((1, D_MODEL, tn), lambda l, j: (l, 0, j)),
                  pl.BlockSpec((1, 1, tn), lambda l, j: (l, 0, j))],
        out_specs=pl.BlockSpec((1, rows, tn), lambda l, j: (l, 0, j)),
        out_shape=jax.ShapeDtypeStruct((n_l, rows, n_out), F32),
        compiler_params=_cparams(("arbitrary", "arbitrary")),
        name="ada",
    )(cc, w_ada, b_ada.reshape(n_l, 1, n_out))


def _mod_specs(n_b, grid_rank):
    if grid_rank == 2:
        return (pl.BlockSpec((1, 1, 6 * D_MODEL), lambda b, j: (b, 0, 0)),
                pl.BlockSpec((1, 1, 6 * D_MODEL), lambda b, j: (n_b, 0, 0)))
    return (pl.BlockSpec((1, 1, 6 * D_MODEL), lambda b, j, f: (b, 0, 0)),
            pl.BlockSpec((1, 1, 6 * D_MODEL), lambda b, j, f: (n_b, 0, 0)))


def _norm_kernel(x_ref, g_ref, mb_ref, mc_ref, h_ref, *, tm):
    is_ctx = _is_ctx_rows(tm, 1)
    h = _norm_mod(x_ref[0], g_ref[...], _mod_vec(mb_ref, mc_ref, 0, is_ctx), _mod_vec(mb_ref, mc_ref, 1, is_ctx))
    h_ref[0] = h.astype(h_ref.dtype)


def _norm1(x, g, modl):
    n_b = x.shape[0]
    tm = 768
    mb, mc = _mod_specs(n_b, 2)
    return pl.pallas_call(
        functools.partial(_norm_kernel, tm=tm),
        grid=(n_b, T_ALL // tm),
        in_specs=[pl.BlockSpec((1, tm, D_MODEL), lambda b, j: (b, j, 0)),
                  pl.BlockSpec((1, D_MODEL), lambda b, j: (0, 0)), mb, mc],
        out_specs=pl.BlockSpec((1, tm, D_MODEL), lambda b, j: (b, j, 0)),
        out_shape=jax.ShapeDtypeStruct(x.shape, BF16),
        compiler_params=_cparams(("parallel", "parallel")),
        name="norm1",
    )(x, g.reshape(1, D_MODEL), modl, modl)


def _mm_kernel(a_ref, w_ref, o_ref):
    o_ref[...] = _dot(a_ref[...], w_ref[...]).astype(o_ref.dtype)


def _mm(a, w, out_dtype, tm, tn, name):
    m, k = a.shape
    n = w.shape[1]
    return pl.pallas_call(
        _mm_kernel,
        grid=(m // tm, n // tn),
        in_specs=[pl.BlockSpec((tm, k), lambda i, j: (i, 0)),
                  pl.BlockSpec((k, tn), lambda i, j: (0, j))],
        out_specs=pl.BlockSpec((tm, tn), lambda i, j: (i, j)),
        out_shape=jax.ShapeDtypeStruct((m, n), out_dtype),
        compiler_params=_cparams(("parallel", "arbitrary")),
        name=name,
    )(a, w)


LOG2E = float(np.log2(np.e))
SUM_LANE = (HEAD_DIM, 0)


def _with_ones_lane(vblk, hh):
    lane = _lane(vblk.shape)
    keep = (lane < HEAD_DIM) if hh == 0 else (lane >= HEAD_DIM)
    return jnp.where(keep, vblk, jnp.where(lane == SUM_LANE[hh], 1.0, 0.0).astype(vblk.dtype))


def _attend_heads(o_ref, q_of, k_of, v_of, extra=None):
    def scores(h):
        q = q_of(h)
        s = _dot_nt(q, k_of(h))
        if extra is None:
            return (s,)
        return (s + extra[0](h), _dot_nt(q, extra[1](h)))

    def finish(h, sc):
        m = jnp.max(sc[0], axis=-1, keepdims=True)
        for s in sc[1:]:
            m = jnp.maximum(m, jnp.max(s, axis=-1, keepdims=True))
        o = _dot(jnp.exp2(sc[0] - m).astype(BF16), v_of(h))
        if extra is not None:
            o = o + _dot(jnp.exp2(sc[1] - m).astype(BF16), extra[2](h))
        lane = SUM_LANE[h % 2]
        return o / o[:, lane:lane + 1]

    nxt = scores(0)
    even = None
    for h in range(N_HEADS):
        cur = nxt
        if h + 1 < N_HEADS:
            nxt = scores(h + 1)
        o = finish(h, cur)
        if h % 2 == 0:
            even = o
        else:
            pair = jnp.where(_lane(o.shape) < HEAD_DIM, even, o)
            o_ref[0, :, 128 * (h // 2):128 * (h // 2 + 1)] = pair.astype(o_ref.dtype)


def _pad_head(blk, hh):
    return jnp.where(_lane(blk.shape) < HEAD_DIM, blk if hh == 0 else pltpu.roll(blk, HEAD_DIM, 1), 0.0)


def _pad_heads(blk):
    return _pad_head(blk, 0), _pad_head(blk, 1)


def _head_rms(x, gain):
    ms = jnp.sum(x * x, axis=-1, keepdims=True) * (1.0 / HEAD_DIM)
    return x * lax.rsqrt(ms + NORM_EPS) * gain


def _gqa_kernel(pq_ref, pall_ref, cq_ref, sq_ref, call_ref, sall_ref, qg_ref, kg_ref, o_ref, k_scr, v_scr):
    j = pl.program_id(1)

    @pl.when(j == 0)
    def _prep():
        kblk = pall_ref[0, :, 512:640].astype(F32)
        for g, kh in enumerate(_pad_heads(kblk)):
            kh = _rope(_head_rms(kh, kg_ref[...]), call_ref[...], sall_ref[...], 16)
            k_scr[:, 128 * g:128 * (g + 1)] = kh.astype(BF16)
        vblk = pall_ref[0, :, 640:768].astype(F32)
        vrot = pltpu.roll(vblk, HEAD_DIM, 1)
        for i, src in enumerate((vblk, vrot, vrot, vblk)):
            v_scr[:, 128 * i:128 * (i + 1)] = _with_ones_lane(src, i % 2).astype(BF16)

    def attend(n_keys):
        group = N_HEADS // A_KV_HEADS

        def q_of(h):
            qh = _pad_head(pq_ref[0, :, 128 * (h // 2):128 * (h // 2 + 1)].astype(F32), h % 2)
            qh = _rope(_head_rms(qh, qg_ref[...]), cq_ref[...], sq_ref[...], 16)
            return (qh * (HEAD_DIM ** -0.5 * LOG2E)).astype(BF16)

        def v_of(h):
            i = 2 * (h // group) + h % 2
            return v_scr[0:n_keys, 128 * i:128 * (i + 1)]

        _attend_heads(o_ref, q_of, lambda h: k_scr[0:n_keys, 128 * (h // group):128 * (h // group + 1)], v_of)

    @pl.when(j == 0)
    def _ctx():
        attend(CTX_LEN)

    @pl.when(j > 0)
    def _lat():
        attend(T_ALL)


def _gqa(p_a, cos, sin, q_gain, k_gain):
    n_b = p_a.shape[0]
    pad = lambda g: jnp.concatenate([g, jnp.zeros((HEAD_DIM,), F32)]).reshape(1, 128)
    tile = lambda b, j: (b, j, 0)
    whole = lambda b, j: (b, 0, 0)
    return pl.pallas_call(
        _gqa_kernel,
        grid=(n_b, N_QT),
        in_specs=[pl.BlockSpec((1, Q_TILE, A_IN), tile),
                  pl.BlockSpec((1, T_ALL, A_IN), whole),
                  pl.BlockSpec((Q_TILE, 128), lambda b, j: (j, 0)),
                  pl.BlockSpec((Q_TILE, 128), lambda b, j: (j, 0)),
                  pl.BlockSpec((T_ALL, 128), lambda b, j: (0, 0)),
                  pl.BlockSpec((T_ALL, 128), lambda b, j: (0, 0)),
                  pl.BlockSpec((1, 128), lambda b, j: (0, 0)),
                  pl.BlockSpec((1, 128), lambda b, j: (0, 0))],
        out_specs=pl.BlockSpec((1, Q_TILE, 512), tile),
        out_shape=jax.ShapeDtypeStruct((n_b, T_ALL, 512), BF16),
        scratch_shapes=[pltpu.VMEM((T_ALL, 256), BF16), pltpu.VMEM((T_ALL, 512), BF16)],
        compiler_params=_cparams(("parallel", "arbitrary")),
        name="gqa",
    )(p_a, p_a, cos, sin, cos, sin, pad(q_gain), pad(k_gain))


def _mla_kernel(pq_ref, pall_ref, cq_ref, sq_ref, call_ref, sall_ref, qg_ref, kvg_ref, wq_ref, wkv_ref,
                o_ref, k_scr, v_scr):
    j = pl.program_id(1)
    kw = N_HEADS * 128

    @pl.when(j == 0)
    def _prep():
        for i in range(N_QT):
            rows = slice(i * Q_TILE, (i + 1) * Q_TILE)
            ckv = pall_ref[0, rows, B_Q_LORA:B_Q_LORA + B_KV_LORA].astype(F32)
            n = ckv * lax.rsqrt(jnp.mean(ckv * ckv, axis=-1, keepdims=True) + NORM_EPS) * kvg_ref[...]
            kv = _dot(n.astype(BF16), wkv_ref[...])
            kr = _rope(pall_ref[0, rows, 640:768].astype(F32), call_ref[rows, :], sall_ref[rows, :], 8)
            for h in range(N_HEADS):
                k_scr[rows, 128 * h:128 * (h + 1)] = (kv[:, 128 * h:128 * (h + 1)] + kr).astype(BF16)
            for h in range(N_HEADS):
                v_scr[rows, 128 * h:128 * (h + 1)] = _with_ones_lane(kv[:, kw + 128 * h:kw + 128 * (h + 1)], h % 2).astype(BF16)

    def attend(n_keys):
        cq = pq_ref[0, :, 0:B_Q_LORA].astype(F32)
        n = cq * lax.rsqrt(jnp.mean(cq * cq, axis=-1, keepdims=True) + NORM_EPS) * qg_ref[...]
        q = _dot(n.astype(BF16), wq_ref[...])
        scale = (B_NOPE + B_ROPE) ** -0.5 * LOG2E
        _attend_heads(
            o_ref,
            lambda h: (_rope(q[:, 128 * h:128 * (h + 1)], cq_ref[...], sq_ref[...], 8) * scale).astype(BF16),
            lambda h: k_scr[0:n_keys, 128 * h:128 * (h + 1)],
            lambda h: v_scr[0:n_keys, 128 * h:128 * (h + 1)])

    @pl.when(j == 0)
    def _ctx():
        attend(CTX_LEN)

    @pl.when(j > 0)
    def _lat():
        attend(T_ALL)


def _mla(p_b, cos, sin, q_gain, kv_gain, wq, wkv):
    n_b = p_b.shape[0]
    tile = lambda b, j: (b, j, 0)
    whole = lambda b, j: (b, 0, 0)
    const = lambda b, j: (0, 0)
    w = p_b.shape[-1]
    return pl.pallas_call(
        _mla_kernel,
        grid=(n_b, N_QT),
        in_specs=[pl.BlockSpec((1, Q_TILE, w), tile),
                  pl.BlockSpec((1, T_ALL, w), whole),
                  pl.BlockSpec((Q_TILE, 128), lambda b, j: (j, 0)),
                  pl.BlockSpec((Q_TILE, 128), lambda b, j: (j, 0)),
                  pl.BlockSpec((T_ALL, 128), const),
                  pl.BlockSpec((T_ALL, 128), const),
                  pl.BlockSpec((1, B_Q_LORA), const),
                  pl.BlockSpec((1, B_KV_LORA), const),
                  pl.BlockSpec(wq.shape, const),
                  pl.BlockSpec(wkv.shape, const)],
        out_specs=pl.BlockSpec((1, Q_TILE, 512), tile),
        out_shape=jax.ShapeDtypeStruct((n_b, T_ALL, 512), BF16),
        scratch_shapes=[pltpu.VMEM((T_ALL, N_HEADS * 128), BF16), pltpu.VMEM((T_ALL, N_HEADS * 128), BF16)],
        compiler_params=_cparams(("parallel", "arbitrary")),
        name="mla",
    )(p_b, p_b, cos, sin, cos, sin, q_gain.reshape(1, -1), kv_gain.reshape(1, -1), wq, wkv)


def _nat_kernel(pq_ref, pall_ref, bias_ref, o_ref, k_scr, v_scr):
    j = pl.program_id(1)

    @pl.when(j == 0)
    def _prep():
        for jb in range(N_HEADS // 2):
            kblk = pall_ref[0, :, 512 + 128 * jb:512 + 128 * (jb + 1)].astype(F32)
            for hh, kh in enumerate(_pad_heads(kblk)):
                h = 2 * jb + hh
                k_scr[:, 128 * h:128 * (h + 1)] = kh.astype(BF16)
            vblk = pall_ref[0, :, 1024 + 128 * jb:1024 + 128 * (jb + 1)]
            for hh in range(2):
                v_scr[:, 128 * (2 * jb + hh):128 * (2 * jb + hh + 1)] = _with_ones_lane(vblk, hh)

    def q_of(h):
        qh = _pad_head(pq_ref[0, :, 128 * (h // 2):128 * (h // 2 + 1)].astype(F32), h % 2)
        return (qh * (HEAD_DIM ** -0.5 * LOG2E)).astype(BF16)

    hs = lambda h: slice(128 * h, 128 * (h + 1))

    @pl.when(j == 0)
    def _ctx():
        _attend_heads(o_ref, q_of, lambda h: k_scr[0:CTX_LEN, hs(h)], lambda h: v_scr[0:CTX_LEN, hs(h)])

    @pl.when(j > 0)
    def _lat():
        first_row = jnp.clip(NAT_QROWS * (j - 1) - NA_ROWS // 2, 0, N_ROWS - NAT_KROWS)
        start = pl.multiple_of(CTX_LEN + first_row * GRID_W, Q_TILE)
        win = pl.ds(start, NAT_KWIN)
        _attend_heads(o_ref, q_of, lambda h: k_scr[win, hs(h)], lambda h: v_scr[win, hs(h)],
                      extra=(lambda h: bias_ref[0, h],
                             lambda h: k_scr[0:CTX_LEN, hs(h)],
                             lambda h: v_scr[0:CTX_LEN, hs(h)]))


def _nat_bias_table(rel_bias):
    n_dr, n_dc = 2 * NA_ROWS - 1, 2 * NA_COLS - 1
    cols = np.arange(GRID_W)
    pick_col = (cols[None, None, :] - cols[None, :, None] + NA_COLS - 1 == np.arange(n_dc)[:, None, None])
    pick_row = np.zeros((3, NAT_QROWS, NAT_KROWS, n_dr), np.float32)
    valid = np.zeros((3, Q_TILE, NAT_KWIN), bool)
    for t, qb in enumerate((0, 1, N_ROWS // NAT_QROWS - 1)):
        first_row = int(np.clip(NAT_QROWS * qb - NA_ROWS // 2, 0, N_ROWS - NAT_KROWS))
        rq = NAT_QROWS * qb + np.arange(NAT_QROWS)
        rk = first_row + np.arange(NAT_KROWS)
        pick_row[t] = (rk[None, :, None] - rq[:, None, None] + NA_ROWS - 1 == np.arange(n_dr)[None, None, :])
        ql, kl = np.arange(Q_TILE), np.arange(NAT_KWIN)
        r, c = NAT_QROWS * qb + ql // GRID_W, ql % GRID_W
        kr, kc = first_row + kl // GRID_W, kl % GRID_W
        r0 = np.clip(r - NA_ROWS // 2, 0, N_ROWS - NA_ROWS)
        c0 = np.clip(c - NA_COLS // 2, 0, GRID_W - NA_COLS)
        valid[t] = ((kr[None, :] >= r0[:, None]) & (kr[None, :] < r0[:, None] + NA_ROWS)
                    & (kc[None, :] >= c0[:, None]) & (kc[None, :] < c0[:, None] + NA_COLS))
    by_col = jnp.einsum('hdc,cab->hdab', rel_bias.astype(F32), jnp.asarray(pick_col, F32), precision=lax.Precision.HIGHEST)
    tab = jnp.einsum('tqkd,hdab->thqakb', jnp.asarray(pick_row), by_col, precision=lax.Precision.HIGHEST)
    tab = tab.reshape(3, N_HEADS, Q_TILE, NAT_KWIN) * LOG2E
    return jnp.where(jnp.asarray(valid)[:, None], tab, NEG_BIG)


def _nat(p_d, bias_tab):
    n_b = p_d.shape[0]
    tile = lambda b, j: (b, j, 0)
    n_lat_tiles = N_QT - 1

    def bias_idx(b, j):
        qb = j - 1
        return (jnp.where(qb <= 0, 0, jnp.where(qb == n_lat_tiles - 1, 2, 1)), 0, 0, 0)

    return pl.pallas_call(
        _nat_kernel,
        grid=(n_b, N_QT),
        in_specs=[pl.BlockSpec((1, Q_TILE, D_IN), tile),
                  pl.BlockSpec((1, T_ALL, D_IN), lambda b, j: (b, 0, 0)),
                  pl.BlockSpec((1, N_HEADS, Q_TILE, NAT_KWIN), bias_idx)],
        out_specs=pl.BlockSpec((1, Q_TILE, 512), tile),
        out_shape=jax.ShapeDtypeStruct((n_b, T_ALL, 512), BF16),
        scratch_shapes=[pltpu.VMEM((T_ALL, N_HEADS * 128), BF16), pltpu.VMEM((T_ALL, N_HEADS * 128), BF16)],
        compiler_params=_cparams(("parallel", "arbitrary")),
        name="nat",
    )(p_d, p_d, bias_tab)


def _seg_sum(x, ones3):
    return _dot3_exact_rhs(x, ones3)


def _head_ones3():
    ones_bd = np.kron(np.eye(N_HEADS), np.ones((HEAD_DIM, HEAD_DIM)))
    return jnp.asarray(np.concatenate([ones_bd] * 3, axis=0), BF16)


def _rwkv_prep_kernel(z_ref, zp_ref, zn_ref, mu_ref, w0_ref, a0_ref, kk_ref, ka_ref, rk_ref,
                      wd3_ref, wa3_ref, wg3_ref, ones_ref,
                      r_out, v_out, kk_out, kd_out, lw_out, bd_out, bonus_out, g_out):
    j = pl.program_id(1)
    z = z_ref[0]
    tm = z.shape[0]
    row = lax.broadcasted_iota(jnp.int32, (tm, 1), 0)
    prev_row = jnp.where(j <= 1, 0.0, zp_ref[0, 7:8, :])
    next_row = jnp.where((j == 0) | (j == N_QT - 1), 0.0, zn_ref[0, 0:1, :])
    z_prev = jnp.where(row == 0, prev_row, pltpu.roll(z, 1, 0))
    z_next = jnp.where(row == tm - 1, next_row, pltpu.roll(z, tm - 1, 0))
    zs = z + (0.5 * (z_prev + z_next) - z) * mu_ref[...]

    r = zs[:, 0:512]
    k = zs[:, 512:1024]
    v = zs[:, 1024:1536]
    w_lo = zs[:, 1536:1664]
    a_lo = zs[:, 1664:1792]
    g_lo = zs[:, 1792:1920]
    ones_bd = ones_ref[...]

    def lora(x, w3_ref):
        xh, xl = _split_bf16(x)
        return _dot(jnp.concatenate([xh, xh, xl], axis=1), w3_ref[...])

    kkr = k * kk_ref[...]
    nrm = jnp.maximum(jnp.sqrt(_seg_sum(kkr * kkr, ones_bd)), 1e-12)
    kk = kkr / nrm
    dec = lora(jnp.tanh(w_lo), wd3_ref)
    aaa = lora(a_lo, wa3_ref)
    g = lora(_sigmoid(g_lo), wg3_ref)
    r_out[0] = r
    v_out[0] = v
    kk_out[0] = kk
    g_out[0] = g
    ksum = None
    for d in range(2):
        u = -(w0_ref[d:d + 1, :] + dec[:, 512 * d:512 * (d + 1)])
        softplus = jnp.maximum(u, 0.0) + jnp.log(1.0 + jnp.exp(-jnp.abs(u)))
        logw = -softplus - 0.5
        lw_out[d, 0] = -jnp.exp(logw)
        a = _sigmoid(a0_ref[d:d + 1, :] + aaa[:, 512 * d:512 * (d + 1)])
        bd_out[d, 0] = kk * a
        kd = k * (1.0 + (a - 1.0) * ka_ref[...])
        kd_out[d, 0] = kd
        ksum = kd if ksum is None else ksum + kd
    bonus_out[0] = _seg_sum(r * ksum * rk_ref[...], ones_bd) * v


def _rwkv_prep(p_c, lp):
    n_b = p_c.shape[0]
    tile = lambda b, j: (b, j, 0)
    const = lambda b, j: (0, 0)
    blocks8 = T_ALL // 8
    tpb = Q_TILE // 8
    prev = lambda b, j: (b, jnp.maximum(j * tpb - 1, 0), 0)
    nxt = lambda b, j: (b, jnp.minimum((j + 1) * tpb, blocks8 - 1), 0)
    o3 = jax.ShapeDtypeStruct((n_b, T_ALL, C_W), F32)
    o4 = jax.ShapeDtypeStruct((2, n_b, T_ALL, C_W), F32)
    s3 = pl.BlockSpec((1, Q_TILE, C_W), tile)
    s4 = pl.BlockSpec((2, 1, Q_TILE, C_W), lambda b, j: (0, b, j, 0))
    small = [lp['c_mu'].reshape(1, C_IN), lp['c_w0'], lp['c_a0'], lp['c_k_k'].reshape(1, C_W),
             lp['c_k_a'].reshape(1, C_W), lp['c_r_k'].reshape(1, C_W)]
    bd2 = lambda w: jnp.concatenate(
        [jnp.concatenate([w[0], jnp.zeros_like(w[0])], 1), jnp.concatenate([jnp.zeros_like(w[1]), w[1]], 1)], 0)
    mats = []
    for w in (bd2(lp['c_w_decay']), bd2(lp['c_w_aaa']), lp['c_w_gate']):
        hi, lo = _split_bf16(w)
        mats.append(jnp.concatenate([hi, lo, hi], axis=0))
    ins = small + mats + [_head_ones3()]
    return pl.pallas_call(
        _rwkv_prep_kernel,
        grid=(n_b, N_QT),
        in_specs=[pl.BlockSpec((1, Q_TILE, C_IN), tile),
                  pl.BlockSpec((1, 8, C_IN), prev),
                  pl.BlockSpec((1, 8, C_IN), nxt)] + [pl.BlockSpec(a.shape, const) for a in ins],
        out_specs=[s3, s3, s3, s4, s4, s4, s3, s3],
        out_shape=[o3, o3, o3, o4, o4, o4, o3, o3],
        compiler_params=_cparams(("parallel", "parallel")),
        name="rwkv_prep",
    )(p_c, p_c, p_c, *ins)


def _dot3_exact_rhs_lhs(m3_bf16, a):
    return _dot(m3_bf16, jnp.concatenate(_split3_bf16(a), axis=0))


N_INV_LEVELS = 6
SCAN_BATCH = 2
SCAN_TILE = 256
MK_STRICT, MK_INCL, MK_LEVEL0 = 0, 1, 2


def _scan_masks():
    masks, cums = [], []
    for fwd in (True, False):
        t = np.arange(CHUNK) if fwd else CHUNK - 1 - np.arange(CHUNK)
        tr, tc = t[:, None], t[None, :]
        levels = [((tr // (2 * h)) == (tc // (2 * h))) & (((tr // h) % 2) == 1) & (((tc // h) % 2) == 0)
                  for h in (2 ** k for k in range(N_INV_LEVELS))]
        masks.append(np.tile(np.stack([tr > tc, tr >= tc] + levels), (1, 1, HG)))
        cums.append(np.tile(tr >= tc, (1, 3)))
    return jnp.asarray(np.stack(masks), F32), jnp.asarray(np.stack(cums), BF16)


def _rwkv_scan_kernel(rf_ref, vf_ref, kkf_ref, kdf_ref, lwf_ref, bdf_ref,
                      rb_ref, vb_ref, kkb_ref, kdb_ref, lwb_ref, bdb_ref, mk_ref, cum_ref, yf_ref, yb_ref, s_scr):
    j = pl.program_id(1)
    lane_head = lax.broadcasted_iota(jnp.int32, (CHUNK, HGW), 1) // HEAD_DIM
    eye = (lax.broadcasted_iota(jnp.int32, (CHUNK, HGW), 0)
           == lax.broadcasted_iota(jnp.int32, (CHUNK, HGW), 1) % CHUNK).astype(F32)
    dir_refs = ((rf_ref, vf_ref, kkf_ref, kdf_ref, lwf_ref, bdf_ref, yf_ref),
                (rb_ref, vb_ref, kkb_ref, kdb_ref, lwb_ref, bdb_ref, yb_ref))
    chunks_per_tile = SCAN_TILE // CHUNK

    def masked(d, k, x):
        return jnp.where(mk_ref[d, k] > 0.5, x, 0.0)

    def stack(x):
        return jnp.concatenate([jnp.where(lane_head == p, x, jnp.zeros_like(x)) for p in range(HG)], axis=0)

    def head_transpose(x):
        t = stack(x).T
        return t[0:HEAD_DIM] + t[HEAD_DIM:2 * HEAD_DIM] + t[2 * HEAD_DIM:3 * HEAD_DIM] + t[3 * HEAD_DIM:4 * HEAD_DIM]

    @pl.when(j == 0)
    def _init():
        s_scr[...] = jnp.zeros_like(s_scr)

    def body(i, carry):
        ch = []
        for bi in range(SCAN_BATCH):
            for d in range(2):
                r_ref, v_ref, kk_ref, kd_ref, lw_ref, bd_ref, y_ref = dir_refs[d]
                c = i if d == 0 else chunks_per_tile - 1 - i
                rows = pl.ds(pl.multiple_of(c * CHUNK, CHUNK), CHUNK)
                lw = lw_ref[0, bi, rows, :]
                cum = _dot3_exact_rhs_lhs(cum_ref[d], lw)
                total = jnp.sum(lw, axis=0, keepdims=True)
                e_pos, e_neg, e_prev, e_rest = jnp.exp(cum), jnp.exp(-cum), jnp.exp(cum - lw), jnp.exp(total - cum)
                p_end = jnp.broadcast_to(jnp.exp(total), (CHUNK, C_W))
                for g in range(C_W // HGW):
                    ln = slice(g * HGW, (g + 1) * HGW)
                    kk, bd, kd = kk_ref[bi, rows, ln], bd_ref[0, bi, rows, ln], kd_ref[0, bi, rows, ln]
                    al = (kk * e_prev[:, ln]).astype(BF16)
                    rh = (r_ref[bi, rows, ln] * e_pos[:, ln]).astype(BF16)
                    ch.append(dict(
                        idx=(2 * bi + d) * (C_W // HGW) + g, d=d, bi=bi, rows=rows, ln=ln, y_ref=y_ref,
                        al=al, rh=rh, al_rh=jnp.concatenate([al, rh], axis=0), al4=stack(al),
                        be4=stack((bd * e_neg[:, ln]).astype(BF16)), ka4=stack((kd * e_neg[:, ln]).astype(BF16)),
                        bee_t=head_transpose(bd * e_rest[:, ln]).astype(BF16),
                        kae_t=head_transpose(kd * e_rest[:, ln]).astype(BF16),
                        p_end_t=head_transpose(p_end[:, ln]),
                        v4=stack(v_ref[bi, rows, ln].astype(BF16))))

        def stage(fn):
            for c_ in ch:
                c_.update(fn(c_, c_['d']))

        def nt_masks(c_, d, key, lo_name, hi_name, lo_f32=False):
            prod = _dot_nt(c_['al_rh'], c_[key])
            lo = masked(d, MK_STRICT, prod[0:CHUNK])
            return {lo_name: lo if lo_f32 else lo.astype(BF16),
                    hi_name: masked(d, MK_INCL, prod[CHUNK:2 * CHUNK]).astype(BF16)}

        stage(lambda c_, d: nt_masks(c_, d, 'be4', 'l_ab', 'm_rb', lo_f32=True))
        stage(lambda c_, d: nt_masks(c_, d, 'ka4', 'l_ak', 'm_rk'))

        def value_products(c_, d):
            prod = _dot(jnp.concatenate([c_['l_ak'], c_['kae_t'], c_['m_rk']], axis=0), c_['v4'])
            return dict(lv4=stack(prod[0:CHUNK].astype(BF16)), hv=prod[CHUNK:2 * CHUNK], yv=prod[2 * CHUNK:3 * CHUNK])

        stage(lambda c_, d: dict(tb=(eye - masked(d, MK_LEVEL0, c_['l_ab'])).astype(BF16)))
        for level in range(1, N_INV_LEVELS):
            stage(lambda c_, d: dict(
                t4=stack(_dot(masked(d, MK_LEVEL0 + level, c_['l_ab']).astype(BF16), stack(c_['tb'])).astype(BF16))))
            if level == 1:
                stage(value_products)
            stage(lambda c_, d: dict(tb=c_['tb'] - _dot(c_['tb'], c_['t4']).astype(BF16)))
        stage(lambda c_, d: dict(w=_dot(c_['tb'], c_['al4']).astype(BF16)))
        stage(lambda c_, d: dict(u0=_dot(c_['tb'], c_['lv4'])))
        stage(lambda c_, d: dict(s=s_scr[c_['idx']]))
        stage(lambda c_, d: dict(on_s=_dot(jnp.concatenate([c_['w'], c_['rh']], axis=0), stack(c_['s'].astype(BF16)))))
        stage(lambda c_, d: dict(u4=stack((c_['on_s'][0:CHUNK] + c_['u0']).astype(BF16))))
        stage(lambda c_, d: dict(on_u=_dot(jnp.concatenate([c_['m_rb'], c_['bee_t']], axis=0), c_['u4'])))
        for c_ in ch:
            s_scr[c_['idx']] = c_['s'] * c_['p_end_t'] - c_['on_u'][CHUNK:2 * CHUNK] + c_['hv']
        for c_ in ch:
            c_['y_ref'][c_['bi'], c_['rows'], c_['ln']] = c_['on_s'][CHUNK:2 * CHUNK] - c_['on_u'][0:CHUNK] + c_['yv']
        return carry

    lax.fori_loop(0, chunks_per_tile, body, 0)


def _rwkv_scan(r, v, kk, kd, lw, bd):
    n_b = r.shape[0]
    assert n_b % SCAN_BATCH == 0
    n_tiles, ctx_tiles = T_ALL // SCAN_TILE, CTX_LEN // SCAN_TILE
    bwd_tile = lambda j: jnp.where(j < ctx_tiles, ctx_tiles - 1 - j, n_tiles + ctx_tiles - 1 - j)
    f3 = pl.BlockSpec((SCAN_BATCH, SCAN_TILE, C_W), lambda b, j: (b, j, 0))
    f4 = pl.BlockSpec((1, SCAN_BATCH, SCAN_TILE, C_W), lambda b, j: (0, b, j, 0))
    b3 = pl.BlockSpec((SCAN_BATCH, SCAN_TILE, C_W), lambda b, j: (b, bwd_tile(j), 0))
    b4 = pl.BlockSpec((1, SCAN_BATCH, SCAN_TILE, C_W), lambda b, j: (1, b, bwd_tile(j), 0))
    y_shape = jax.ShapeDtypeStruct((n_b, T_ALL, C_W), F32)
    masks, cums = _scan_masks()
    return pl.pallas_call(
        _rwkv_scan_kernel,
        grid=(n_b // SCAN_BATCH, n_tiles),
        in_specs=[f3, f3, f3, f4, f4, f4, b3, b3, b3, b4, b4, b4,
                  pl.BlockSpec(masks.shape, lambda b, j: (0, 0, 0, 0)),
                  pl.BlockSpec(cums.shape, lambda b, j: (0, 0, 0))],
        out_specs=[f3, b3],
        out_shape=[y_shape, y_shape],
        scratch_shapes=[pltpu.VMEM((SCAN_BATCH * 2 * (C_W // HGW), HEAD_DIM, HGW), F32)],
        compiler_params=_cparams(("parallel", "arbitrary")),
        name="rwkv_scan",
    )(r, v, kk, kd, lw, bd, r, v, kk, kd, lw, bd, masks, cums)


def _rwkv_out_kernel(yf_ref, yb_ref, bonus_ref, g_ref, gw_ref, gb_ref, ones_ref, o_ref):
    y = yf_ref[0] + yb_ref[0]
    ones_bd = ones_ref[...]
    mean = _seg_sum(y, ones_bd) * (1.0 / HEAD_DIM)
    yc = y - mean
    var = _seg_sum(yc * yc, ones_bd) * (1.0 / HEAD_DIM)
    yn = yc * lax.rsqrt(var + C_GN_EPS) * gw_ref[...] + gb_ref[...]
    o_ref[0] = ((yn + bonus_ref[0]) * g_ref[0]).astype(o_ref.dtype)


def _rwkv_out(y_f, y_b, bonus, g, gn_w, gn_b):
    n_b = bonus.shape[0]
    tm = 768
    tile = lambda b, j: (b, j, 0)
    const = lambda b, j: (0, 0)
    ones_bd = _head_ones3()
    return pl.pallas_call(
        _rwkv_out_kernel,
        grid=(n_b, T_ALL // tm),
        in_specs=[pl.BlockSpec((1, tm, C_W), tile), pl.BlockSpec((1, tm, C_W), tile),
                  pl.BlockSpec((1, tm, C_W), tile), pl.BlockSpec((1, tm, C_W), tile),
                  pl.BlockSpec((1, C_W), const), pl.BlockSpec((1, C_W), const),
                  pl.BlockSpec(ones_bd.shape, const)],
        out_specs=pl.BlockSpec((1, tm, C_W), tile),
        out_shape=jax.ShapeDtypeStruct((n_b, T_ALL, C_W), BF16),
        compiler_params=_cparams(("parallel", "parallel")),
        name="rwkv_out",
    )(y_f, y_b, bonus, g, gn_w.reshape(1, C_W), gn_b.reshape(1, C_W), ones_bd)


def _merge_kernel(x_ref, oa_ref, ob_ref, oc_ref, od_ref, gate_ref, wb_ref, wo_ref, mb_ref, mc_ref, out_ref, *, tm):
    is_ctx = _is_ctx_rows(tm, 1)
    y = None
    for i, o_ref in enumerate((oa_ref, ob_ref, oc_ref, od_ref)):
        z = _dot(o_ref[0], wb_ref[i])
        sg = _sigmoid(gate_ref[0, :, i * D_MODEL:(i + 1) * D_MODEL].astype(F32))
        y = sg * z if y is None else y + sg * z
    z = _dot(y.astype(BF16), wo_ref[...])
    out_ref[0] = x_ref[0] + _mod_vec(mb_ref, mc_ref, 2, is_ctx) * z


def _merge(x, outs, gates, wb, wo, modl):
    n_b = x.shape[0]
    tm = 768
    tile = lambda b, j: (b, j, 0)
    mb, mc = _mod_specs(n_b, 2)
    o_spec = pl.BlockSpec((1, tm, 512), tile)
    return pl.pallas_call(
        functools.partial(_merge_kernel, tm=tm),
        grid=(n_b, T_ALL // tm),
        in_specs=[pl.BlockSpec((1, tm, D_MODEL), tile), o_spec, o_spec, o_spec, o_spec,
                  pl.BlockSpec((1, tm, GATE_IN), tile),
                  pl.BlockSpec(wb.shape, lambda b, j: (0, 0, 0)),
                  pl.BlockSpec(wo.shape, lambda b, j: (0, 0)), mb, mc],
        out_specs=pl.BlockSpec((1, tm, D_MODEL), tile),
        out_shape=jax.ShapeDtypeStruct(x.shape, F32),
        compiler_params=_cparams(("parallel", "parallel")),
        name="merge",
    )(x, *outs, gates, wb, wo, modl, modl)


def _mlp_kernel(x_ref, g_ref, mb_ref, mc_ref, w1_ref, w2_ref, out_ref, h_scr, acc_scr, *, tm, n_f):
    f = pl.program_id(2)
    is_ctx = _is_ctx_rows(tm, 1)

    @pl.when(f == 0)
    def _init():
        h = _norm_mod(x_ref[0], g_ref[...], _mod_vec(mb_ref, mc_ref, 3, is_ctx), _mod_vec(mb_ref, mc_ref, 4, is_ctx))
        h_scr[...] = h.astype(BF16)
        acc_scr[...] = jnp.zeros_like(acc_scr)

    a = jnp.square(jnp.maximum(_dot(h_scr[...], w1_ref[...]), 0.0))
    acc_scr[...] += _dot(a.astype(BF16), w2_ref[...])

    @pl.when(f == n_f - 1)
    def _fin():
        out_ref[0] = x_ref[0] + _mod_vec(mb_ref, mc_ref, 5, is_ctx) * acc_scr[...]


def _mlp(x, g, w1, w2, modl):
    n_b = x.shape[0]
    tm, tf = 1152, 2048
    n_f = D_FF // tf
    tile = lambda b, j, f: (b, j, 0)
    mb, mc = _mod_specs(n_b, 3)
    return pl.pallas_call(
        functools.partial(_mlp_kernel, tm=tm, n_f=n_f),
        grid=(n_b, T_ALL // tm, n_f),
        in_specs=[pl.BlockSpec((1, tm, D_MODEL), tile),
                  pl.BlockSpec((1, D_MODEL), lambda b, j, f: (0, 0)), mb, mc,
                  pl.BlockSpec((D_MODEL, tf), lambda b, j, f: (0, f)),
                  pl.BlockSpec((tf, D_MODEL), lambda b, j, f: (f, 0))],
        out_specs=pl.BlockSpec((1, tm, D_MODEL), tile),
        out_shape=jax.ShapeDtypeStruct(x.shape, F32),
        scratch_shapes=[pltpu.VMEM((tm, D_MODEL), BF16), pltpu.VMEM((tm, D_MODEL), F32)],
        compiler_params=_cparams(("parallel", "parallel", "arbitrary")),
        name="mlp",
    )(x, g.reshape(1, D_MODEL), modl, modl, w1, w2)


def _final_kernel(x_ref, g_ref, o_ref):
    x = x_ref[0]
    o_ref[0] = x * lax.rsqrt(jnp.mean(x * x, axis=-1, keepdims=True) + NORM_EPS) * g_ref[...]


def _final_norm(x, g):
    n_b = x.shape[0]
    tm = Q_TILE
    return pl.pallas_call(
        _final_kernel,
        grid=(n_b, SEQ // tm),
        in_specs=[pl.BlockSpec((1, tm, D_MODEL), lambda b, j: (b, j + CTX_LEN // tm, 0)),
                  pl.BlockSpec((1, D_MODEL), lambda b, j: (0, 0))],
        out_specs=pl.BlockSpec((1, tm, D_MODEL), lambda b, j: (b, j, 0)),
        out_shape=jax.ShapeDtypeStruct((n_b, SEQ, D_MODEL), F32),
        compiler_params=_cparams(("parallel", "parallel")),
        name="final_norm",
    )(x, g.reshape(1, D_MODEL))


def _rope_tables(half, lane0):
    t = np.arange(SEQ)
    inv = ROPE_THETA ** (-np.arange(half, dtype=np.float64) / half)
    cos = np.ones((T_ALL, V7X_LANES), np.float64)
    sin = np.zeros((T_ALL, V7X_LANES), np.float64)
    for part, pos in enumerate((t // GRID_W, t % GRID_W)):
        ang = pos[:, None].astype(np.float64) * inv[None, :]
        ang = ang.astype(np.float32).astype(np.float64)
        base = lane0 + 2 * half * part
        cos[CTX_LEN:, base:base + half] = np.cos(ang)
        cos[CTX_LEN:, base + half:base + 2 * half] = np.cos(ang)
        sin[CTX_LEN:, base:base + half] = -np.sin(ang)
        sin[CTX_LEN:, base + half:base + 2 * half] = np.sin(ang)
    return jnp.asarray(cos, F32), jnp.asarray(sin, F32)


def _layer_weights(l, w_in, b_w_q_up, b_w_kv_up, w_branch, w_out, w_mlp1, w_mlp2):
    wi = w_in[l]
    o_b = A_IN
    o_c = o_b + B_IN
    o_d = o_c + C_IN
    o_g = o_d + D_IN
    w_a = wi[:, :o_b]
    wb_raw = wi[:, o_b:o_c]
    z = lambda n: jnp.zeros((D_MODEL, n), F32)
    w_b = jnp.concatenate([wb_raw[:, :B_Q_LORA + B_KV_LORA], z(64), wb_raw[:, B_Q_LORA + B_KV_LORA:], z(32)], 1)
    wq = b_w_q_up[l].reshape(B_Q_LORA, N_HEADS, B_NOPE + B_ROPE)
    wq = jnp.concatenate([wq, jnp.zeros((B_Q_LORA, N_HEADS, 128 - B_NOPE - B_ROPE), F32)], -1)
    wkv = b_w_kv_up[l].reshape(B_KV_LORA, N_HEADS, 2 * HEAD_DIM)
    zk = jnp.zeros((B_KV_LORA, N_HEADS, HEAD_DIM), F32)
    wk = jnp.concatenate([wkv[:, :, :B_NOPE], zk], -1)
    even = (jnp.arange(N_HEADS) % 2 == 0)[None, :, None]
    wv = jnp.concatenate([jnp.where(even, wkv[:, :, B_NOPE:], 0.0), jnp.where(even, 0.0, wkv[:, :, B_NOPE:])], -1)
    return dict(
        w_a=w_a.astype(BF16), w_b=w_b.astype(BF16), w_c=wi[:, o_c:o_d].astype(BF16),
        w_d=wi[:, o_d:o_g].astype(BF16), w_g=wi[:, o_g:].astype(BF16),
        wq=wq.reshape(B_Q_LORA, N_HEADS * 128).astype(BF16),
        wkv=jnp.concatenate([wk.reshape(B_KV_LORA, -1), wv.reshape(B_KV_LORA, -1)], 1).astype(BF16),
        w_branch=w_branch[l].astype(BF16), w_out=w_out[l].astype(BF16),
        w_mlp1=w_mlp1[l].astype(BF16), w_mlp2=w_mlp2[l].astype(BF16))


def kernel(x, c, ctx, c_ctx, w_ada, b_ada, g_norm1, g_norm2, w_in, a_q_gain, a_k_gain, b_q_gain, b_kv_gain,
           b_w_q_up, b_w_kv_up, c_mu, c_w0, c_w_decay, c_a0, c_w_aaa, c_w_gate, c_k_k, c_k_a, c_r_k,
           c_gn_w, c_gn_b, d_rel_bias, w_branch, w_out, w_mlp1, w_mlp2, g_final):
    n_b = x.shape[0]
    assert x.shape[1:] == (SEQ, D_MODEL) and ctx.shape[1:] == (CTX_LEN, D_MODEL)
    m = n_b * T_ALL
    mod_rows = ((n_b + 1 + 7) // 8) * 8
    cc = jnp.concatenate([c, c_ctx[None, :], jnp.zeros((mod_rows - n_b - 1, D_MODEL), F32)], 0)
    mod_all = _ada(cc, w_ada, b_ada)
    cos_a, sin_a = _rope_tables(16, 0)
    cos_b, sin_b = _rope_tables(8, B_NOPE)
    xs = jnp.concatenate([ctx, x], axis=1)
    for l in range(DEPTH):
        lw = _layer_weights(l, w_in, b_w_q_up, b_w_kv_up, w_branch, w_out, w_mlp1, w_mlp2)
        lp = dict(c_mu=c_mu[l], c_w0=c_w0[l], c_w_decay=c_w_decay[l], c_a0=c_a0[l], c_w_aaa=c_w_aaa[l],
                  c_w_gate=c_w_gate[l], c_k_k=c_k_k[l], c_k_a=c_k_a[l], c_r_k=c_r_k[l])
        modl = mod_all[l].reshape(mod_rows, 1, 6 * D_MODEL)
        h = _norm1(xs, g_norm1[l], modl).reshape(m, D_MODEL)
        p_a = _mm(h, lw['w_a'], BF16, T_ALL, A_IN, "w_in_a").reshape(n_b, T_ALL, A_IN)
        p_b = _mm(h, lw['w_b'], BF16, T_ALL, 768, "w_in_b").reshape(n_b, T_ALL, 768)
        p_c = _mm(h, lw['w_c'], F32, T_ALL, 640, "w_in_c").reshape(n_b, T_ALL, C_IN)
        p_d = _mm(h, lw['w_d'], BF16, T_ALL, 768, "w_in_d").reshape(n_b, T_ALL, D_IN)
        p_g = _mm(h, lw['w_g'], BF16, T_ALL, 1024, "w_in_g").reshape(n_b, T_ALL, GATE_IN)
        o_a = _gqa(p_a, cos_a, sin_a, a_q_gain[l], a_k_gain[l])
        o_b = _mla(p_b, cos_b, sin_b, b_q_gain[l], b_kv_gain[l], lw['wq'], lw['wkv'])
        r, v, kk, kd, lwd, bd, bonus, g = _rwkv_prep(p_c, lp)
        y_f, y_b = _rwkv_scan(r, v, kk, kd, lwd, bd)
        o_c = _rwkv_out(y_f, y_b, bonus, g, c_gn_w[l], c_gn_b[l])
        o_d = _nat(p_d, _nat_bias_table(d_rel_bias[l]))
        xs = _merge(xs, (o_a, o_b, o_c, o_d), p_g, lw['w_branch'], lw['w_out'], modl)
        xs = _mlp(xs, g_norm2[l], lw['w_mlp1'], lw['w_mlp2'], modl)
    return _final_norm(xs, g_final)
```

```python
import functools

import numpy as np
import jax
import jax.numpy as jnp
from jax import lax
from jax.experimental import pallas as pl
from jax.experimental.pallas import tpu as pltpu

F32 = jnp.float32
BF16 = jnp.bfloat16

D_MODEL = 1024
SEQ = 2048
DEPTH = 2
GRID_W = 64
N_ROWS = SEQ // GRID_W
CTX_LEN = 256
T_ALL = CTX_LEN + SEQ
HEAD_DIM = 64
ROPE_THETA = 10000.0
NORM_EPS = 1e-6
N_HEADS = 8
A_KV_HEADS = 2
B_Q_LORA = 384
B_KV_LORA = 256
B_NOPE = 64
B_ROPE = 32
C_W = 512
C_GN_EPS = 64e-5
NA_ROWS = 8
NA_COLS = 16
D_FF = 4 * D_MODEL
A_IN = 768
B_IN = 672
C_IN = 1920
D_IN = 1536
GATE_IN = 4096
B_PAD = 768
PROJ_B_BLOCK = A_IN // B_PAD
PROJ_D_BLOCK = (A_IN + B_PAD) // D_IN

V7X_LANES = 128
V7X_VMEM_LIMIT = 56 * 1024 * 1024

Q_TILE = 256
N_QT = T_ALL // Q_TILE
NAT_QROWS = Q_TILE // GRID_W
NAT_KROWS = 12
NAT_KWIN = NAT_KROWS * GRID_W
CHUNK = 64
N_CHUNK = T_ALL // CHUNK
N_CTX_CHUNK = CTX_LEN // CHUNK
HG = 4
HGW = HG * HEAD_DIM
NEG_BIG = -1e30


def _cparams(sem, vmem=V7X_VMEM_LIMIT):
    return pltpu.CompilerParams(dimension_semantics=sem, vmem_limit_bytes=vmem)


def _split_bf16(a):
    hi = a.astype(BF16)
    lo = (a - hi.astype(F32)).astype(BF16)
    return hi, lo


def _dot(a, b):
    return jnp.dot(a, b, preferred_element_type=F32)


def _dot_nt(a, b):
    return lax.dot_general(a, b, (((1,), (1,)), ((), ())), preferred_element_type=F32)


def _dot3(a, b):
    ah, al = _split_bf16(a)
    bh, bl = _split_bf16(b)
    return _dot(ah, bh) + _dot(ah, bl) + _dot(al, bh)


def _split3_bf16(a):
    a0 = a.astype(BF16)
    r1 = a - a0.astype(F32)
    a1 = r1.astype(BF16)
    return a0, a1, (r1 - a1.astype(F32)).astype(BF16)


def _dot3_exact_rhs(a, b3_bf16):
    return _dot(jnp.concatenate(_split3_bf16(a), axis=1), b3_bf16)


def _sigmoid(x):
    return 1.0 / (1.0 + jnp.exp(-x))


def _lane(shape):
    return lax.broadcasted_iota(jnp.int32, shape, len(shape) - 1)


def _rope(x, cos, sin, half):
    n = x.shape[-1]
    lo = (_lane(x.shape) % (2 * half)) < half
    partner = jnp.where(lo, pltpu.roll(x, n - half, 1), pltpu.roll(x, half, 1))
    return x * cos + partner * sin


def _mod_vec(mb_ref, mc_ref, k, is_ctx):
    lat = mb_ref[0, :, k * D_MODEL:(k + 1) * D_MODEL]
    ctx = mc_ref[0, :, k * D_MODEL:(k + 1) * D_MODEL]
    return jnp.where(is_ctx, ctx, lat)


def _is_ctx_rows(tm, tile_axis):
    row = pl.program_id(tile_axis) * tm + lax.broadcasted_iota(jnp.int32, (tm, 1), 0)
    return row < CTX_LEN


def _norm_mod(x, g, shift, scale):
    y = x * lax.rsqrt(jnp.mean(x * x, axis=-1, keepdims=True) + NORM_EPS) * g
    return y * (1.0 + scale) + shift


def _ada_kernel(c_ref, w_ref, b_ref, o_ref):
    c = c_ref[...]
    s = c * _sigmoid(c)
    o_ref[0] = _dot3(s, w_ref[0]) + b_ref[0]


def _ada(cc, w_ada, b_ada):
    n_l, _, n_out = w_ada.shape
    tn = 1536
    rows = cc.shape[0]
    return pl.pallas_call(
        _ada_kernel,
        grid=(n_l, n_out // tn),
        in_specs=[pl.BlockSpec((rows, D_MODEL), lambda l, j: (0, 0)),
                  pl.BlockSpec((1, D_MODEL, tn), lambda l, j: (l, 0, j)),
                  pl.BlockSpec((1, 1, tn), lambda l, j: (l, 0, j))],
        out_specs=pl.BlockSpec((1, rows, tn), lambda l, j: (l, 0, j)),
        out_shape=jax.ShapeDtypeStruct((n_l, rows, n_out), F32),
        compiler_params=_cparams(("arbitrary", "arbitrary")),
        name="ada",
    )(cc, w_ada, b_ada.reshape(n_l, 1, n_out))


def _mod_specs(n_b, grid_rank):
    if grid_rank == 2:
        return (pl.BlockSpec((1, 1, 6 * D_MODEL), lambda b, j: (b, 0, 0)),
                pl.BlockSpec((1, 1, 6 * D_MODEL), lambda b, j: (n_b, 0, 0)))
    return (pl.BlockSpec((1, 1, 6 * D_MODEL), lambda b, j, f: (b, 0, 0)),
            pl.BlockSpec((1, 1, 6 * D_MODEL), lambda b, j, f: (n_b, 0, 0)))


def _norm_kernel(x_ref, g_ref, mb_ref, mc_ref, h_ref, *, tm):
    is_ctx = _is_ctx_rows(tm, 1)
    h = _norm_mod(x_ref[0], g_ref[...], _mod_vec(mb_ref, mc_ref, 0, is_ctx), _mod_vec(mb_ref, mc_ref, 1, is_ctx))
    h_ref[0] = h.astype(h_ref.dtype)


def _norm1(x, g, modl):
    n_b = x.shape[0]
    tm = 768
    mb, mc = _mod_specs(n_b, 2)
    return pl.pallas_call(
        functools.partial(_norm_kernel, tm=tm),
        grid=(n_b, T_ALL // tm),
        in_specs=[pl.BlockSpec((1, tm, D_MODEL), lambda b, j: (b, j, 0)),
                  pl.BlockSpec((1, D_MODEL), lambda b, j: (0, 0)), mb, mc],
        out_specs=pl.BlockSpec((1, tm, D_MODEL), lambda b, j: (b, j, 0)),
        out_shape=jax.ShapeDtypeStruct(x.shape, BF16),
        compiler_params=_cparams(("parallel", "parallel")),
        name="norm1",
    )(x, g.reshape(1, D_MODEL), modl, modl)


def _mm_kernel(a_ref, w_ref, o_ref):
    o_ref[0] = _dot(a_ref[0], w_ref[...]).astype(o_ref.dtype)


def _mm(a, w, out_dtype, tn, name):
    n_b, t, k = a.shape
    n = w.shape[1]
    return pl.pallas_call(
        _mm_kernel,
        grid=(n_b, n // tn),
        in_specs=[pl.BlockSpec((1, t, k), lambda i, j: (i, 0, 0)),
                  pl.BlockSpec((k, tn), lambda i, j: (0, j))],
        out_specs=pl.BlockSpec((1, t, tn), lambda i, j: (i, 0, j)),
        out_shape=jax.ShapeDtypeStruct((n_b, t, n), out_dtype),
        compiler_params=_cparams(("parallel", "arbitrary")),
        name=name,
    )(a, w)


LOG2E = float(np.log2(np.e))
SUM_LANE = (HEAD_DIM, 0)


def _with_ones_lane(vblk, hh):
    lane = _lane(vblk.shape)
    keep = (lane < HEAD_DIM) if hh == 0 else (lane >= HEAD_DIM)
    return jnp.where(keep, vblk, jnp.where(lane == SUM_LANE[hh], 1.0, 0.0).astype(vblk.dtype))


def _attend_heads(o_ref, q_of, k_of, v_of, extra=None):
    def scores(h):
        q = q_of(h)
        s = _dot_nt(q, k_of(h))
        if extra is None:
            return (s,)
        return (s + extra[0](h), _dot_nt(q, extra[1](h)))

    def finish(h, sc):
        m = jnp.max(sc[0], axis=-1, keepdims=True)
        for s in sc[1:]:
            m = jnp.maximum(m, jnp.max(s, axis=-1, keepdims=True))
        o = _dot(jnp.exp2(sc[0] - m).astype(BF16), v_of(h))
        if extra is not None:
            o = o + _dot(jnp.exp2(sc[1] - m).astype(BF16), extra[2](h))
        lane = SUM_LANE[h % 2]
        return o / o[:, lane:lane + 1]

    nxt = scores(0)
    even = None
    for h in range(N_HEADS):
        cur = nxt
        if h + 1 < N_HEADS:
            nxt = scores(h + 1)
        o = finish(h, cur)
        if h % 2 == 0:
            even = o
        else:
            pair = jnp.where(_lane(o.shape) < HEAD_DIM, even, o)
            o_ref[0, :, 128 * (h // 2):128 * (h // 2 + 1)] = pair.astype(o_ref.dtype)


def _pad_head(blk, hh):
    return jnp.where(_lane(blk.shape) < HEAD_DIM, blk if hh == 0 else pltpu.roll(blk, HEAD_DIM, 1), 0.0)


def _pad_heads(blk):
    return _pad_head(blk, 0), _pad_head(blk, 1)


def _head_rms(x, gain):
    ms = jnp.sum(x * x, axis=-1, keepdims=True) * (1.0 / HEAD_DIM)
    return x * lax.rsqrt(ms + NORM_EPS) * gain


def _gqa_kernel(pq_ref, pall_ref, cq_ref, sq_ref, call_ref, sall_ref, qg_ref, kg_ref, o_ref, k_scr, v_scr):
    j = pl.program_id(1)

    @pl.when(j == 0)
    def _prep():
        kblk = pall_ref[0, :, 512:640].astype(F32)
        for g, kh in enumerate(_pad_heads(kblk)):
            kh = _rope(_head_rms(kh, kg_ref[...]), call_ref[...], sall_ref[...], 16)
            k_scr[:, 128 * g:128 * (g + 1)] = kh.astype(BF16)
        vblk = pall_ref[0, :, 640:768].astype(F32)
        vrot = pltpu.roll(vblk, HEAD_DIM, 1)
        for i, src in enumerate((vblk, vrot, vrot, vblk)):
            v_scr[:, 128 * i:128 * (i + 1)] = _with_ones_lane(src, i % 2).astype(BF16)

    def attend(n_keys):
        group = N_HEADS // A_KV_HEADS

        def q_of(h):
            qh = _pad_head(pq_ref[0, :, 128 * (h // 2):128 * (h // 2 + 1)].astype(F32), h % 2)
            qh = _rope(_head_rms(qh, qg_ref[...]), cq_ref[...], sq_ref[...], 16)
            return (qh * (HEAD_DIM ** -0.5 * LOG2E)).astype(BF16)

        def v_of(h):
            i = 2 * (h // group) + h % 2
            return v_scr[0:n_keys, 128 * i:128 * (i + 1)]

        _attend_heads(o_ref, q_of, lambda h: k_scr[0:n_keys, 128 * (h // group):128 * (h // group + 1)], v_of)

    @pl.when(j == 0)
    def _ctx():
        attend(CTX_LEN)

    @pl.when(j > 0)
    def _lat():
        attend(T_ALL)


def _gqa(p_a, cos, sin, q_gain, k_gain):
    n_b = p_a.shape[0]
    pad = lambda g: jnp.concatenate([g, jnp.zeros((HEAD_DIM,), F32)]).reshape(1, 128)
    tile = lambda b, j: (b, j, 0)
    whole = lambda b, j: (b, 0, 0)
    return pl.pallas_call(
        _gqa_kernel,
        grid=(n_b, N_QT),
        in_specs=[pl.BlockSpec((1, Q_TILE, A_IN), tile),
                  pl.BlockSpec((1, T_ALL, A_IN), whole),
                  pl.BlockSpec((Q_TILE, 128), lambda b, j: (j, 0)),
                  pl.BlockSpec((Q_TILE, 128), lambda b, j: (j, 0)),
                  pl.BlockSpec((T_ALL, 128), lambda b, j: (0, 0)),
                  pl.BlockSpec((T_ALL, 128), lambda b, j: (0, 0)),
                  pl.BlockSpec((1, 128), lambda b, j: (0, 0)),
                  pl.BlockSpec((1, 128), lambda b, j: (0, 0))],
        out_specs=pl.BlockSpec((1, Q_TILE, 512), tile),
        out_shape=jax.ShapeDtypeStruct((n_b, T_ALL, 512), BF16),
        scratch_shapes=[pltpu.VMEM((T_ALL, 256), BF16), pltpu.VMEM((T_ALL, 512), BF16)],
        compiler_params=_cparams(("parallel", "arbitrary")),
        name="gqa",
    )(p_a, p_a, cos, sin, cos, sin, pad(q_gain), pad(k_gain))


def _mla_kernel(pq_ref, pall_ref, cq_ref, sq_ref, call_ref, sall_ref, qg_ref, kvg_ref, wq_ref, wkv_ref,
                o_ref, k_scr, v_scr):
    j = pl.program_id(1)
    kw = N_HEADS * 128

    @pl.when(j == 0)
    def _prep():
        for i in range(N_QT):
            rows = slice(i * Q_TILE, (i + 1) * Q_TILE)
            ckv = pall_ref[0, rows, B_Q_LORA:B_Q_LORA + B_KV_LORA].astype(F32)
            n = ckv * lax.rsqrt(jnp.mean(ckv * ckv, axis=-1, keepdims=True) + NORM_EPS) * kvg_ref[...]
            kv = _dot(n.astype(BF16), wkv_ref[...])
            kr = _rope(pall_ref[0, rows, 640:768].astype(F32), call_ref[rows, :], sall_ref[rows, :], 8)
            for h in range(N_HEADS):
                k_scr[rows, 128 * h:128 * (h + 1)] = (kv[:, 128 * h:128 * (h + 1)] + kr).astype(BF16)
            for h in range(N_HEADS):
                v_scr[rows, 128 * h:128 * (h + 1)] = _with_ones_lane(kv[:, kw + 128 * h:kw + 128 * (h + 1)], h % 2).astype(BF16)

    def attend(n_keys):
        cq = pq_ref[0, :, 0:B_Q_LORA].astype(F32)
        n = cq * lax.rsqrt(jnp.mean(cq * cq, axis=-1, keepdims=True) + NORM_EPS) * qg_ref[...]
        q = _dot(n.astype(BF16), wq_ref[...])
        scale = (B_NOPE + B_ROPE) ** -0.5 * LOG2E
        _attend_heads(
            o_ref,
            lambda h: (_rope(q[:, 128 * h:128 * (h + 1)], cq_ref[...], sq_ref[...], 8) * scale).astype(BF16),
            lambda h: k_scr[0:n_keys, 128 * h:128 * (h + 1)],
            lambda h: v_scr[0:n_keys, 128 * h:128 * (h + 1)])

    @pl.when(j == 0)
    def _ctx():
        attend(CTX_LEN)

    @pl.when(j > 0)
    def _lat():
        attend(T_ALL)


def _mla(p, cos, sin, q_gain, kv_gain, wq, wkv):
    n_b = p.shape[0]
    tile = lambda b, j: (b, j, 0)
    const = lambda b, j: (0, 0)
    return pl.pallas_call(
        _mla_kernel,
        grid=(n_b, N_QT),
        in_specs=[pl.BlockSpec((1, Q_TILE, B_PAD), lambda b, j: (b, j, PROJ_B_BLOCK)),
                  pl.BlockSpec((1, T_ALL, B_PAD), lambda b, j: (b, 0, PROJ_B_BLOCK)),
                  pl.BlockSpec((Q_TILE, 128), lambda b, j: (j, 0)),
                  pl.BlockSpec((Q_TILE, 128), lambda b, j: (j, 0)),
                  pl.BlockSpec((T_ALL, 128), const),
                  pl.BlockSpec((T_ALL, 128), const),
                  pl.BlockSpec((1, B_Q_LORA), const),
                  pl.BlockSpec((1, B_KV_LORA), const),
                  pl.BlockSpec(wq.shape, const),
                  pl.BlockSpec(wkv.shape, const)],
        out_specs=pl.BlockSpec((1, Q_TILE, 512), tile),
        out_shape=jax.ShapeDtypeStruct((n_b, T_ALL, 512), BF16),
        scratch_shapes=[pltpu.VMEM((T_ALL, N_HEADS * 128), BF16), pltpu.VMEM((T_ALL, N_HEADS * 128), BF16)],
        compiler_params=_cparams(("parallel", "arbitrary")),
        name="mla",
    )(p, p, cos, sin, cos, sin, q_gain.reshape(1, -1), kv_gain.reshape(1, -1), wq, wkv)


def _nat_kernel(pq_ref, pall_ref, bias_ref, o_ref, k_scr, v_scr):
    j = pl.program_id(1)

    @pl.when(j == 0)
    def _prep():
        for jb in range(N_HEADS // 2):
            kblk = pall_ref[0, :, 512 + 128 * jb:512 + 128 * (jb + 1)].astype(F32)
            for hh, kh in enumerate(_pad_heads(kblk)):
                h = 2 * jb + hh
                k_scr[:, 128 * h:128 * (h + 1)] = kh.astype(BF16)
            vblk = pall_ref[0, :, 1024 + 128 * jb:1024 + 128 * (jb + 1)]
            for hh in range(2):
                v_scr[:, 128 * (2 * jb + hh):128 * (2 * jb + hh + 1)] = _with_ones_lane(vblk, hh)

    def q_of(h):
        qh = _pad_head(pq_ref[0, :, 128 * (h // 2):128 * (h // 2 + 1)].astype(F32), h % 2)
        return (qh * (HEAD_DIM ** -0.5 * LOG2E)).astype(BF16)

    hs = lambda h: slice(128 * h, 128 * (h + 1))

    @pl.when(j == 0)
    def _ctx():
        _attend_heads(o_ref, q_of, lambda h: k_scr[0:CTX_LEN, hs(h)], lambda h: v_scr[0:CTX_LEN, hs(h)])

    @pl.when(j > 0)
    def _lat():
        first_row = jnp.clip(NAT_QROWS * (j - 1) - NA_ROWS // 2, 0, N_ROWS - NAT_KROWS)
        start = pl.multiple_of(CTX_LEN + first_row * GRID_W, Q_TILE)
        win = pl.ds(start, NAT_KWIN)
        _attend_heads(o_ref, q_of, lambda h: k_scr[win, hs(h)], lambda h: v_scr[win, hs(h)],
                      extra=(lambda h: bias_ref[0, h],
                             lambda h: k_scr[0:CTX_LEN, hs(h)],
                             lambda h: v_scr[0:CTX_LEN, hs(h)]))


def _nat_bias_table(rel_bias):
    n_dr, n_dc = 2 * NA_ROWS - 1, 2 * NA_COLS - 1
    cols = np.arange(GRID_W)
    pick_col = (cols[None, None, :] - cols[None, :, None] + NA_COLS - 1 == np.arange(n_dc)[:, None, None])
    pick_row = np.zeros((3, NAT_QROWS, NAT_KROWS, n_dr), np.float32)
    valid = np.zeros((3, Q_TILE, NAT_KWIN), bool)
    for t, qb in enumerate((0, 1, N_ROWS // NAT_QROWS - 1)):
        first_row = int(np.clip(NAT_QROWS * qb - NA_ROWS // 2, 0, N_ROWS - NAT_KROWS))
        rq = NAT_QROWS * qb + np.arange(NAT_QROWS)
        rk = first_row + np.arange(NAT_KROWS)
        pick_row[t] = (rk[None, :, None] - rq[:, None, None] + NA_ROWS - 1 == np.arange(n_dr)[None, None, :])
        ql, kl = np.arange(Q_TILE), np.arange(NAT_KWIN)
        r, c = NAT_QROWS * qb + ql // GRID_W, ql % GRID_W
        kr, kc = first_row + kl // GRID_W, kl % GRID_W
        r0 = np.clip(r - NA_ROWS // 2, 0, N_ROWS - NA_ROWS)
        c0 = np.clip(c - NA_COLS // 2, 0, GRID_W - NA_COLS)
        valid[t] = ((kr[None, :] >= r0[:, None]) & (kr[None, :] < r0[:, None] + NA_ROWS)
                    & (kc[None, :] >= c0[:, None]) & (kc[None, :] < c0[:, None] + NA_COLS))
    by_col = jnp.einsum('hdc,cab->hdab', rel_bias.astype(F32), jnp.asarray(pick_col, F32), precision=lax.Precision.HIGHEST)
    tab = jnp.einsum('tqkd,hdab->thqakb', jnp.asarray(pick_row), by_col, precision=lax.Precision.HIGHEST)
    tab = tab.reshape(3, N_HEADS, Q_TILE, NAT_KWIN) * LOG2E
    return jnp.where(jnp.asarray(valid)[:, None], tab, NEG_BIG)


def _nat(p, bias_tab):
    n_b = p.shape[0]
    tile = lambda b, j: (b, j, 0)
    n_lat_tiles = N_QT - 1

    def bias_idx(b, j):
        qb = j - 1
        return (jnp.where(qb <= 0, 0, jnp.where(qb == n_lat_tiles - 1, 2, 1)), 0, 0, 0)

    return pl.pallas_call(
        _nat_kernel,
        grid=(n_b, N_QT),
        in_specs=[pl.BlockSpec((1, Q_TILE, D_IN), lambda b, j: (b, j, PROJ_D_BLOCK)),
                  pl.BlockSpec((1, T_ALL, D_IN), lambda b, j: (b, 0, PROJ_D_BLOCK)),
                  pl.BlockSpec((1, N_HEADS, Q_TILE, NAT_KWIN), bias_idx)],
        out_specs=pl.BlockSpec((1, Q_TILE, 512), tile),
        out_shape=jax.ShapeDtypeStruct((n_b, T_ALL, 512), BF16),
        scratch_shapes=[pltpu.VMEM((T_ALL, N_HEADS * 128), BF16), pltpu.VMEM((T_ALL, N_HEADS * 128), BF16)],
        compiler_params=_cparams(("parallel", "arbitrary")),
        name="nat",
    )(p, p, bias_tab)


def _seg_sum(x, ones3):
    return _dot3_exact_rhs(x, ones3)


def _head_ones3():
    ones_bd = np.kron(np.eye(N_HEADS), np.ones((HEAD_DIM, HEAD_DIM)))
    return jnp.asarray(np.concatenate([ones_bd] * 3, axis=0), BF16)


def _rwkv_prep_kernel(z_ref, zp_ref, zn_ref, mu_ref, w0_ref, a0_ref, kk_ref, ka_ref, rk_ref,
                      wd3_ref, wa3_ref, wg3_ref, ones_ref,
                      r_out, v_out, kk_out, kd_out, lw_out, bd_out, bonus_out, g_out):
    j = pl.program_id(1)
    z = z_ref[0]
    tm = z.shape[0]
    row = lax.broadcasted_iota(jnp.int32, (tm, 1), 0)
    prev_row = jnp.where(j <= 1, 0.0, zp_ref[0, 7:8, :])
    next_row = jnp.where((j == 0) | (j == N_QT - 1), 0.0, zn_ref[0, 0:1, :])
    z_prev = jnp.where(row == 0, prev_row, pltpu.roll(z, 1, 0))
    z_next = jnp.where(row == tm - 1, next_row, pltpu.roll(z, tm - 1, 0))
    zs = z + (0.5 * (z_prev + z_next) - z) * mu_ref[...]

    r = zs[:, 0:512]
    k = zs[:, 512:1024]
    v = zs[:, 1024:1536]
    w_lo = zs[:, 1536:1664]
    a_lo = zs[:, 1664:1792]
    g_lo = zs[:, 1792:1920]
    ones_bd = ones_ref[...]

    def lora(x, w3_ref):
        xh, xl = _split_bf16(x)
        return _dot(jnp.concatenate([xh, xh, xl], axis=1), w3_ref[...])

    kkr = k * kk_ref[...]
    nrm = jnp.maximum(jnp.sqrt(_seg_sum(kkr * kkr, ones_bd)), 1e-12)
    kk = kkr / nrm
    dec = lora(jnp.tanh(w_lo), wd3_ref)
    aaa = lora(a_lo, wa3_ref)
    g = lora(_sigmoid(g_lo), wg3_ref)
    r_out[0] = r
    v_out[0] = v
    kk_out[0] = kk
    g_out[0] = g
    ksum = None
    for d in range(2):
        u = -(w0_ref[d:d + 1, :] + dec[:, 512 * d:512 * (d + 1)])
        softplus = jnp.maximum(u, 0.0) + jnp.log(1.0 + jnp.exp(-jnp.abs(u)))
        logw = -softplus - 0.5
        lw_out[d, 0] = -jnp.exp(logw)
        a = _sigmoid(a0_ref[d:d + 1, :] + aaa[:, 512 * d:512 * (d + 1)])
        bd_out[d, 0] = kk * a
        kd = k * (1.0 + (a - 1.0) * ka_ref[...])
        kd_out[d, 0] = kd
        ksum = kd if ksum is None else ksum + kd
    bonus_out[0] = _seg_sum(r * ksum * rk_ref[...], ones_bd) * v


def _rwkv_prep(p_c, lp):
    n_b = p_c.shape[0]
    tile = lambda b, j: (b, j, 0)
    const = lambda b, j: (0, 0)
    blocks8 = T_ALL // 8
    tpb = Q_TILE // 8
    prev = lambda b, j: (b, jnp.maximum(j * tpb - 1, 0), 0)
    nxt = lambda b, j: (b, jnp.minimum((j + 1) * tpb, blocks8 - 1), 0)
    o3 = jax.ShapeDtypeStruct((n_b, T_ALL, C_W), F32)
    o4 = jax.ShapeDtypeStruct((2, n_b, T_ALL, C_W), F32)
    s3 = pl.BlockSpec((1, Q_TILE, C_W), tile)
    s4 = pl.BlockSpec((2, 1, Q_TILE, C_W), lambda b, j: (0, b, j, 0))
    small = [lp['c_mu'].reshape(1, C_IN), lp['c_w0'], lp['c_a0'], lp['c_k_k'].reshape(1, C_W),
             lp['c_k_a'].reshape(1, C_W), lp['c_r_k'].reshape(1, C_W)]
    bd2 = lambda w: jnp.concatenate(
        [jnp.concatenate([w[0], jnp.zeros_like(w[0])], 1), jnp.concatenate([jnp.zeros_like(w[1]), w[1]], 1)], 0)
    mats = []
    for w in (bd2(lp['c_w_decay']), bd2(lp['c_w_aaa']), lp['c_w_gate']):
        hi, lo = _split_bf16(w)
        mats.append(jnp.concatenate([hi, lo, hi], axis=0))
    ins = small + mats + [_head_ones3()]
    return pl.pallas_call(
        _rwkv_prep_kernel,
        grid=(n_b, N_QT),
        in_specs=[pl.BlockSpec((1, Q_TILE, C_IN), tile),
                  pl.BlockSpec((1, 8, C_IN), prev),
                  pl.BlockSpec((1, 8, C_IN), nxt)] + [pl.BlockSpec(a.shape, const) for a in ins],
        out_specs=[s3, s3, s3, s4, s4, s4, s3, s3],
        out_shape=[o3, o3, o3, o4, o4, o4, o3, o3],
        compiler_params=_cparams(("parallel", "parallel")),
        name="rwkv_prep",
    )(p_c, p_c, p_c, *ins)


def _dot3_exact_rhs_lhs(m3_bf16, a):
    return _dot(m3_bf16, jnp.concatenate(_split3_bf16(a), axis=0))


N_INV_LEVELS = 6
SCAN_BATCH = 2
SCAN_TILE = 256
MK_STRICT, MK_INCL, MK_LEVEL0 = 0, 1, 2


def _scan_masks():
    masks, cums = [], []
    for fwd in (True, False):
        t = np.arange(CHUNK) if fwd else CHUNK - 1 - np.arange(CHUNK)
        tr, tc = t[:, None], t[None, :]
        levels = [((tr // (2 * h)) == (tc // (2 * h))) & (((tr // h) % 2) == 1) & (((tc // h) % 2) == 0)
                  for h in (2 ** k for k in range(N_INV_LEVELS))]
        masks.append(np.tile(np.stack([tr > tc, tr >= tc] + levels), (1, 1, HG)))
        cums.append(np.tile(tr >= tc, (1, 3)))
    return jnp.asarray(np.stack(masks), F32), jnp.asarray(np.stack(cums), BF16)


def _rwkv_scan_kernel(rf_ref, vf_ref, kkf_ref, kdf_ref, lwf_ref, bdf_ref,
                      rb_ref, vb_ref, kkb_ref, kdb_ref, lwb_ref, bdb_ref, mk_ref, cum_ref, yf_ref, yb_ref, s_scr):
    j = pl.program_id(1)
    lane_head = lax.broadcasted_iota(jnp.int32, (CHUNK, HGW), 1) // HEAD_DIM
    eye = (lax.broadcasted_iota(jnp.int32, (CHUNK, HGW), 0)
           == lax.broadcasted_iota(jnp.int32, (CHUNK, HGW), 1) % CHUNK).astype(F32)
    dir_refs = ((rf_ref, vf_ref, kkf_ref, kdf_ref, lwf_ref, bdf_ref, yf_ref),
                (rb_ref, vb_ref, kkb_ref, kdb_ref, lwb_ref, bdb_ref, yb_ref))
    chunks_per_tile = SCAN_TILE // CHUNK

    def masked(d, k, x):
        return jnp.where(mk_ref[d, k] > 0.5, x, 0.0)

    def stack(x):
        return jnp.concatenate([jnp.where(lane_head == p, x, jnp.zeros_like(x)) for p in range(HG)], axis=0)

    def head_transpose(x):
        t = stack(x).T
        return t[0:HEAD_DIM] + t[HEAD_DIM:2 * HEAD_DIM] + t[2 * HEAD_DIM:3 * HEAD_DIM] + t[3 * HEAD_DIM:4 * HEAD_DIM]

    @pl.when(j == 0)
    def _init():
        s_scr[...] = jnp.zeros_like(s_scr)

    def body(i, carry):
        ch = []
        for bi in range(SCAN_BATCH):
            for d in range(2):
                r_ref, v_ref, kk_ref, kd_ref, lw_ref, bd_ref, y_ref = dir_refs[d]
                c = i if d == 0 else chunks_per_tile - 1 - i
                rows = pl.ds(pl.multiple_of(c * CHUNK, CHUNK), CHUNK)
                lw = lw_ref[0, bi, rows, :]
                cum = _dot3_exact_rhs_lhs(cum_ref[d], lw)
                total = jnp.sum(lw, axis=0, keepdims=True)
                e_pos, e_neg, e_prev, e_rest = jnp.exp(cum), jnp.exp(-cum), jnp.exp(cum - lw), jnp.exp(total - cum)
                p_end = jnp.broadcast_to(jnp.exp(total), (CHUNK, C_W))
                for g in range(C_W // HGW):
                    ln = slice(g * HGW, (g + 1) * HGW)
                    kk, bd, kd = kk_ref[bi, rows, ln], bd_ref[0, bi, rows, ln], kd_ref[0, bi, rows, ln]
                    al = (kk * e_prev[:, ln]).astype(BF16)
                    rh = (r_ref[bi, rows, ln] * e_pos[:, ln]).astype(BF16)
                    ch.append(dict(
                        idx=(2 * bi + d) * (C_W // HGW) + g, d=d, bi=bi, rows=rows, ln=ln, y_ref=y_ref,
                        al=al, rh=rh, al_rh=jnp.concatenate([al, rh], axis=0), al4=stack(al),
                        be4=stack((bd * e_neg[:, ln]).astype(BF16)), ka4=stack((kd * e_neg[:, ln]).astype(BF16)),
                        bee_t=head_transpose(bd * e_rest[:, ln]).astype(BF16),
                        kae_t=head_transpose(kd * e_rest[:, ln]).astype(BF16),
                        p_end_t=head_transpose(p_end[:, ln]),
                        v4=stack(v_ref[bi, rows, ln].astype(BF16))))

        def stage(fn):
            for c_ in ch:
                c_.update(fn(c_, c_['d']))

        def nt_masks(c_, d, key, lo_name, hi_name, lo_f32=False):
            prod = _dot_nt(c_['al_rh'], c_[key])
            lo = masked(d, MK_STRICT, prod[0:CHUNK])
            return {lo_name: lo if lo_f32 else lo.astype(BF16),
                    hi_name: masked(d, MK_INCL, prod[CHUNK:2 * CHUNK]).astype(BF16)}

        stage(lambda c_, d: nt_masks(c_, d, 'be4', 'l_ab', 'm_rb', lo_f32=True))
        stage(lambda c_, d: nt_masks(c_, d, 'ka4', 'l_ak', 'm_rk'))

        def value_products(c_, d):
            prod = _dot(jnp.concatenate([c_['l_ak'], c_['kae_t'], c_['m_rk']], axis=0), c_['v4'])
            return dict(lv4=stack(prod[0:CHUNK].astype(BF16)), hv=prod[CHUNK:2 * CHUNK], yv=prod[2 * CHUNK:3 * CHUNK])

        stage(lambda c_, d: dict(tb=(eye - masked(d, MK_LEVEL0, c_['l_ab'])).astype(BF16)))
        for level in range(1, N_INV_LEVELS):
            stage(lambda c_, d: dict(
                t4=stack(_dot(masked(d, MK_LEVEL0 + level, c_['l_ab']).astype(BF16), stack(c_['tb'])).astype(BF16))))
            if level == 1:
                stage(value_products)
            stage(lambda c_, d: dict(tb=c_['tb'] - _dot(c_['tb'], c_['t4']).astype(BF16)))
        stage(lambda c_, d: dict(w=_dot(c_['tb'], c_['al4']).astype(BF16)))
        stage(lambda c_, d: dict(u0=_dot(c_['tb'], c_['lv4'])))
        stage(lambda c_, d: dict(s=s_scr[c_['idx']]))
        stage(lambda c_, d: dict(on_s=_dot(jnp.concatenate([c_['w'], c_['rh']], axis=0), stack(c_['s'].astype(BF16)))))
        stage(lambda c_, d: dict(u4=stack((c_['on_s'][0:CHUNK] + c_['u0']).astype(BF16))))
        stage(lambda c_, d: dict(on_u=_dot(jnp.concatenate([c_['m_rb'], c_['bee_t']], axis=0), c_['u4'])))
        for c_ in ch:
            s_scr[c_['idx']] = c_['s'] * c_['p_end_t'] - c_['on_u'][CHUNK:2 * CHUNK] + c_['hv']
        for c_ in ch:
            c_['y_ref'][c_['bi'], c_['rows'], c_['ln']] = c_['on_s'][CHUNK:2 * CHUNK] - c_['on_u'][0:CHUNK] + c_['yv']
        return carry

    lax.fori_loop(0, chunks_per_tile, body, 0)


def _rwkv_scan(r, v, kk, kd, lw, bd):
    n_b = r.shape[0]
    assert n_b % SCAN_BATCH == 0
    n_tiles, ctx_tiles = T_ALL // SCAN_TILE, CTX_LEN // SCAN_TILE
    bwd_tile = lambda j: jnp.where(j < ctx_tiles, ctx_tiles - 1 - j, n_tiles + ctx_tiles - 1 - j)
    f3 = pl.BlockSpec((SCAN_BATCH, SCAN_TILE, C_W), lambda b, j: (b, j, 0))
    f4 = pl.BlockSpec((1, SCAN_BATCH, SCAN_TILE, C_W), lambda b, j: (0, b, j, 0))
    b3 = pl.BlockSpec((SCAN_BATCH, SCAN_TILE, C_W), lambda b, j: (b, bwd_tile(j), 0))
    b4 = pl.BlockSpec((1, SCAN_BATCH, SCAN_TILE, C_W), lambda b, j: (1, b, bwd_tile(j), 0))
    y_shape = jax.ShapeDtypeStruct((n_b, T_ALL, C_W), F32)
    masks, cums = _scan_masks()
    return pl.pallas_call(
        _rwkv_scan_kernel,
        grid=(n_b // SCAN_BATCH, n_tiles),
        in_specs=[f3, f3, f3, f4, f4, f4, b3, b3, b3, b4, b4, b4,
                  pl.BlockSpec(masks.shape, lambda b, j: (0, 0, 0, 0)),
                  pl.BlockSpec(cums.shape, lambda b, j: (0, 0, 0))],
        out_specs=[f3, b3],
        out_shape=[y_shape, y_shape],
        scratch_shapes=[pltpu.VMEM((SCAN_BATCH * 2 * (C_W // HGW), HEAD_DIM, HGW), F32)],
        compiler_params=_cparams(("parallel", "arbitrary")),
        name="rwkv_scan",
    )(r, v, kk, kd, lw, bd, r, v, kk, kd, lw, bd, masks, cums)


def _rwkv_out_kernel(yf_ref, yb_ref, bonus_ref, g_ref, gw_ref, gb_ref, ones_ref, o_ref):
    y = yf_ref[0] + yb_ref[0]
    ones_bd = ones_ref[...]
    mean = _seg_sum(y, ones_bd) * (1.0 / HEAD_DIM)
    yc = y - mean
    var = _seg_sum(yc * yc, ones_bd) * (1.0 / HEAD_DIM)
    yn = yc * lax.rsqrt(var + C_GN_EPS) * gw_ref[...] + gb_ref[...]
    o_ref[0] = ((yn + bonus_ref[0]) * g_ref[0]).astype(o_ref.dtype)


def _rwkv_out(y_f, y_b, bonus, g, gn_w, gn_b):
    n_b = bonus.shape[0]
    tm = 768
    tile = lambda b, j: (b, j, 0)
    const = lambda b, j: (0, 0)
    ones_bd = _head_ones3()
    return pl.pallas_call(
        _rwkv_out_kernel,
        grid=(n_b, T_ALL // tm),
        in_specs=[pl.BlockSpec((1, tm, C_W), tile), pl.BlockSpec((1, tm, C_W), tile),
                  pl.BlockSpec((1, tm, C_W), tile), pl.BlockSpec((1, tm, C_W), tile),
                  pl.BlockSpec((1, C_W), const), pl.BlockSpec((1, C_W), const),
                  pl.BlockSpec(ones_bd.shape, const)],
        out_specs=pl.BlockSpec((1, tm, C_W), tile),
        out_shape=jax.ShapeDtypeStruct((n_b, T_ALL, C_W), BF16),
        compiler_params=_cparams(("parallel", "parallel")),
        name="rwkv_out",
    )(y_f, y_b, bonus, g, gn_w.reshape(1, C_W), gn_b.reshape(1, C_W), ones_bd)


def _merge_kernel(x_ref, oa_ref, ob_ref, oc_ref, od_ref, gate_ref, wb_ref, wo_ref, mb_ref, mc_ref, out_ref, *, tm):
    is_ctx = _is_ctx_rows(tm, 1)
    y = None
    for i, o_ref in enumerate((oa_ref, ob_ref, oc_ref, od_ref)):
        z = _dot(o_ref[0], wb_ref[i])
        sg = _sigmoid(gate_ref[0, :, i * D_MODEL:(i + 1) * D_MODEL].astype(F32))
        y = sg * z if y is None else y + sg * z
    z = _dot(y.astype(BF16), wo_ref[...])
    out_ref[0] = x_ref[0] + _mod_vec(mb_ref, mc_ref, 2, is_ctx) * z


def _merge(x, outs, gates, wb, wo, modl):
    n_b = x.shape[0]
    tm = 768
    tile = lambda b, j: (b, j, 0)
    mb, mc = _mod_specs(n_b, 2)
    o_spec = pl.BlockSpec((1, tm, 512), tile)
    return pl.pallas_call(
        functools.partial(_merge_kernel, tm=tm),
        grid=(n_b, T_ALL // tm),
        in_specs=[pl.BlockSpec((1, tm, D_MODEL), tile), o_spec, o_spec, o_spec, o_spec,
                  pl.BlockSpec((1, tm, GATE_IN), tile),
                  pl.BlockSpec(wb.shape, lambda b, j: (0, 0, 0)),
                  pl.BlockSpec(wo.shape, lambda b, j: (0, 0)), mb, mc],
        out_specs=pl.BlockSpec((1, tm, D_MODEL), tile),
        out_shape=jax.ShapeDtypeStruct(x.shape, F32),
        compiler_params=_cparams(("parallel", "parallel")),
        name="merge",
    )(x, *outs, gates, wb, wo, modl, modl)


def _mlp_kernel(x_ref, g_ref, mb_ref, mc_ref, w1_ref, w2_ref, out_ref, h_scr, acc_scr, *, tm, n_f):
    f = pl.program_id(2)
    is_ctx = _is_ctx_rows(tm, 1)

    @pl.when(f == 0)
    def _init():
        h = _norm_mod(x_ref[0], g_ref[...], _mod_vec(mb_ref, mc_ref, 3, is_ctx), _mod_vec(mb_ref, mc_ref, 4, is_ctx))
        h_scr[...] = h.astype(BF16)
        acc_scr[...] = jnp.zeros_like(acc_scr)

    a = jnp.square(jnp.maximum(_dot(h_scr[...], w1_ref[...]), 0.0))
    acc_scr[...] += _dot(a.astype(BF16), w2_ref[...])

    @pl.when(f == n_f - 1)
    def _fin():
        out_ref[0] = x_ref[0] + _mod_vec(mb_ref, mc_ref, 5, is_ctx) * acc_scr[...]


def _mlp(x, g, w1, w2, modl):
    n_b = x.shape[0]
    tm, tf = 1152, 2048
    n_f = D_FF // tf
    tile = lambda b, j, f: (b, j, 0)
    mb, mc = _mod_specs(n_b, 3)
    return pl.pallas_call(
        functools.partial(_mlp_kernel, tm=tm, n_f=n_f),
        grid=(n_b, T_ALL // tm, n_f),
        in_specs=[pl.BlockSpec((1, tm, D_MODEL), tile),
                  pl.BlockSpec((1, D_MODEL), lambda b, j, f: (0, 0)), mb, mc,
                  pl.BlockSpec((D_MODEL, tf), lambda b, j, f: (0, f)),
                  pl.BlockSpec((tf, D_MODEL), lambda b, j, f: (f, 0))],
        out_specs=pl.BlockSpec((1, tm, D_MODEL), tile),
        out_shape=jax.ShapeDtypeStruct(x.shape, F32),
        scratch_shapes=[pltpu.VMEM((tm, D_MODEL), BF16), pltpu.VMEM((tm, D_MODEL), F32)],
        compiler_params=_cparams(("parallel", "parallel", "arbitrary")),
        name="mlp",
    )(x, g.reshape(1, D_MODEL), modl, modl, w1, w2)


def _final_kernel(x_ref, g_ref, o_ref):
    x = x_ref[0]
    o_ref[0] = x * lax.rsqrt(jnp.mean(x * x, axis=-1, keepdims=True) + NORM_EPS) * g_ref[...]


def _final_norm(x, g):
    n_b = x.shape[0]
    tm = Q_TILE
    return pl.pallas_call(
        _final_kernel,
        grid=(n_b, SEQ // tm),
        in_specs=[pl.BlockSpec((1, tm, D_MODEL), lambda b, j: (b, j + CTX_LEN // tm, 0)),
                  pl.BlockSpec((1, D_MODEL), lambda b, j: (0, 0))],
        out_specs=pl.BlockSpec((1, tm, D_MODEL), lambda b, j: (b, j, 0)),
        out_shape=jax.ShapeDtypeStruct((n_b, SEQ, D_MODEL), F32),
        compiler_params=_cparams(("parallel", "parallel")),
        name="final_norm",
    )(x, g.reshape(1, D_MODEL))


def _rope_tables(half, lane0):
    t = np.arange(SEQ)
    inv = ROPE_THETA ** (-np.arange(half, dtype=np.float64) / half)
    cos = np.ones((T_ALL, V7X_LANES), np.float64)
    sin = np.zeros((T_ALL, V7X_LANES), np.float64)
    for part, pos in enumerate((t // GRID_W, t % GRID_W)):
        ang = pos[:, None].astype(np.float64) * inv[None, :]
        ang = ang.astype(np.float32).astype(np.float64)
        base = lane0 + 2 * half * part
        cos[CTX_LEN:, base:base + half] = np.cos(ang)
        cos[CTX_LEN:, base + half:base + 2 * half] = np.cos(ang)
        sin[CTX_LEN:, base:base + half] = -np.sin(ang)
        sin[CTX_LEN:, base + half:base + 2 * half] = np.sin(ang)
    return jnp.asarray(cos, F32), jnp.asarray(sin, F32)


def _layer_weights(l, w_in, b_w_q_up, b_w_kv_up, w_branch, w_out, w_mlp1, w_mlp2):
    wi = w_in[l]
    o_b = A_IN
    o_c = o_b + B_IN
    o_d = o_c + C_IN
    o_g = o_d + D_IN
    w_a = wi[:, :o_b]
    wb_raw = wi[:, o_b:o_c]
    z = lambda n: jnp.zeros((D_MODEL, n), F32)
    w_b = jnp.concatenate([wb_raw[:, :B_Q_LORA + B_KV_LORA], z(64), wb_raw[:, B_Q_LORA + B_KV_LORA:], z(32)], 1)
    wq = b_w_q_up[l].reshape(B_Q_LORA, N_HEADS, B_NOPE + B_ROPE)
    wq = jnp.concatenate([wq, jnp.zeros((B_Q_LORA, N_HEADS, 128 - B_NOPE - B_ROPE), F32)], -1)
    wkv = b_w_kv_up[l].reshape(B_KV_LORA, N_HEADS, 2 * HEAD_DIM)
    zk = jnp.zeros((B_KV_LORA, N_HEADS, HEAD_DIM), F32)
    wk = jnp.concatenate([wkv[:, :, :B_NOPE], zk], -1)
    even = (jnp.arange(N_HEADS) % 2 == 0)[None, :, None]
    wv = jnp.concatenate([jnp.where(even, wkv[:, :, B_NOPE:], 0.0), jnp.where(even, 0.0, wkv[:, :, B_NOPE:])], -1)
    return dict(
        w_abd=jnp.concatenate([w_a, w_b, wi[:, o_d:o_g]], 1).astype(BF16),
        w_c=wi[:, o_c:o_d].astype(BF16), w_g=wi[:, o_g:].astype(BF16),
        wq=wq.reshape(B_Q_LORA, N_HEADS * 128).astype(BF16),
        wkv=jnp.concatenate([wk.reshape(B_KV_LORA, -1), wv.reshape(B_KV_LORA, -1)], 1).astype(BF16),
        w_branch=w_branch[l].astype(BF16), w_out=w_out[l].astype(BF16),
        w_mlp1=w_mlp1[l].astype(BF16), w_mlp2=w_mlp2[l].astype(BF16))


def kernel(x, c, ctx, c_ctx, w_ada, b_ada, g_norm1, g_norm2, w_in, a_q_gain, a_k_gain, b_q_gain, b_kv_gain,
           b_w_q_up, b_w_kv_up, c_mu, c_w0, c_w_decay, c_a0, c_w_aaa, c_w_gate, c_k_k, c_k_a, c_r_k,
           c_gn_w, c_gn_b, d_rel_bias, w_branch, w_out, w_mlp1, w_mlp2, g_final):
    n_b = x.shape[0]
    assert x.shape[1:] == (SEQ, D_MODEL) and ctx.shape[1:] == (CTX_LEN, D_MODEL)
    mod_rows = ((n_b + 1 + 7) // 8) * 8
    cc = jnp.concatenate([c, c_ctx[None, :], jnp.zeros((mod_rows - n_b - 1, D_MODEL), F32)], 0)
    mod_all = _ada(cc, w_ada, b_ada)
    cos_a, sin_a = _rope_tables(16, 0)
    cos_b, sin_b = _rope_tables(8, B_NOPE)
    xs = jnp.concatenate([ctx, x], axis=1)
    for l in range(DEPTH):
        lw = _layer_weights(l, w_in, b_w_q_up, b_w_kv_up, w_branch, w_out, w_mlp1, w_mlp2)
        lp = dict(c_mu=c_mu[l], c_w0=c_w0[l], c_w_decay=c_w_decay[l], c_a0=c_a0[l], c_w_aaa=c_w_aaa[l],
                  c_w_gate=c_w_gate[l], c_k_k=c_k_k[l], c_k_a=c_k_a[l], c_r_k=c_r_k[l])
        modl = mod_all[l].reshape(mod_rows, 1, 6 * D_MODEL)
        h = _norm1(xs, g_norm1[l], modl)
        p_abd = _mm(h, lw['w_abd'], BF16, 1024, "w_in_abd")
        p_c = _mm(h, lw['w_c'], F32, 640, "w_in_c")
        p_g = _mm(h, lw['w_g'], BF16, 1024, "w_in_g")
        o_a = _gqa(p_abd, cos_a, sin_a, a_q_gain[l], a_k_gain[l])
        o_b = _mla(p_abd, cos_b, sin_b, b_q_gain[l], b_kv_gain[l], lw['wq'], lw['wkv'])
        r, v, kk, kd, lwd, bd, bonus, g = _rwkv_prep(p_c, lp)
        y_f, y_b = _rwkv_scan(r, v, kk, kd, lwd, bd)
        o_c = _rwkv_out(y_f, y_b, bonus, g, c_gn_w[l], c_gn_b[l])
        o_d = _nat(p_abd, _nat_bias_table(d_rel_bias[l]))
        xs = _merge(xs, (o_a, o_b, o_c, o_d), p_g, lw['w_branch'], lw['w_out'], modl)
        xs = _mlp(xs, g_norm2[l], lw['w_mlp1'], lw['w_mlp2'], modl)
    return _final_norm(xs, g_final)
```

```python
import functools

import numpy as np
import jax
import jax.numpy as jnp
from jax import lax
from jax.experimental import pallas as pl
from jax.experimental.pallas import tpu as pltpu

F32 = jnp.float32
BF16 = jnp.bfloat16

D_MODEL = 1024
SEQ = 2048
DEPTH = 2
GRID_W = 64
N_ROWS = SEQ // GRID_W
CTX_LEN = 256
T_ALL = CTX_LEN + SEQ
HEAD_DIM = 64
ROPE_THETA = 10000.0
NORM_EPS = 1e-6
N_HEADS = 8
A_KV_HEADS = 2
B_Q_LORA = 384
B_KV_LORA = 256
B_NOPE = 64
B_ROPE = 32
C_W = 512
C_GN_EPS = 64e-5
NA_ROWS = 8
NA_COLS = 16
D_FF = 4 * D_MODEL
A_IN = 768
B_IN = 672
C_IN = 1920
D_IN = 1536
GATE_IN = 4096
B_PAD = 768
PROJ_B_BLOCK = A_IN // B_PAD
PROJ_D_BLOCK = (A_IN + B_PAD) // D_IN

V7X_LANES = 128
V7X_VMEM_LIMIT = 56 * 1024 * 1024
LB = V7X_LANES

Q_TILE = 256
N_QT = T_ALL // Q_TILE
NAT_QROWS = Q_TILE // GRID_W
NAT_KROWS = 12
NAT_KWIN = NAT_KROWS * GRID_W
CHUNK = 64
HG = 4
HGW = HG * HEAD_DIM
NEG_BIG = -1e30


def _cparams(sem, vmem=V7X_VMEM_LIMIT):
    return pltpu.CompilerParams(dimension_semantics=sem, vmem_limit_bytes=vmem)


def _split_bf16(a):
    hi = a.astype(BF16)
    lo = (a - hi.astype(F32)).astype(BF16)
    return hi, lo


def _dot(a, b):
    return jnp.dot(a, b, preferred_element_type=F32)


def _dot_nt(a, b):
    return lax.dot_general(a, b, (((1,), (1,)), ((), ())), preferred_element_type=F32)


def _dot3(a, b):
    ah, al = _split_bf16(a)
    bh, bl = _split_bf16(b)
    return _dot(ah, bh) + _dot(ah, bl) + _dot(al, bh)


def _split3_bf16(a):
    a0 = a.astype(BF16)
    r1 = a - a0.astype(F32)
    a1 = r1.astype(BF16)
    return a0, a1, (r1 - a1.astype(F32)).astype(BF16)


def _dot3_exact_rhs(a, b3_bf16):
    return _dot(jnp.concatenate(_split3_bf16(a), axis=1), b3_bf16)


def _sigmoid(x):
    return 1.0 / (1.0 + jnp.exp(-x))


def _lane(shape):
    return lax.broadcasted_iota(jnp.int32, shape, len(shape) - 1)


def _rope(x, cos, sin, half):
    n = x.shape[-1]
    lo = (_lane(x.shape) % (2 * half)) < half
    partner = jnp.where(lo, pltpu.roll(x, n - half, 1), pltpu.roll(x, half, 1))
    return x * cos + partner * sin


def _mod_vec(mb_ref, mc_ref, k, is_ctx):
    lat = mb_ref[0, :, k * D_MODEL:(k + 1) * D_MODEL]
    ctx = mc_ref[0, :, k * D_MODEL:(k + 1) * D_MODEL]
    return jnp.where(is_ctx, ctx, lat)


def _is_ctx_rows(tm, tile_axis):
    row = pl.program_id(tile_axis) * tm + lax.broadcasted_iota(jnp.int32, (tm, 1), 0)
    return row < CTX_LEN


def _norm_mod(x, g, shift, scale):
    y = x * lax.rsqrt(jnp.mean(x * x, axis=-1, keepdims=True) + NORM_EPS) * g
    return y * (1.0 + scale) + shift


def _ada_kernel(c_ref, w_ref, b_ref, o_ref):
    c = c_ref[...]
    s = c * _sigmoid(c)
    o_ref[0] = _dot3(s, w_ref[0]) + b_ref[0]


def _ada(cc, w_ada, b_ada):
    n_l, _, n_out = w_ada.shape
    tn = 1536
    rows = cc.shape[0]
    return pl.pallas_call(
        _ada_kernel,
        grid=(n_l, n_out // tn),
        in_specs=[pl.BlockSpec((rows, D_MODEL), lambda l, j: (0, 0)),
                  pl.BlockSpec((1, D_MODEL, tn), lambda l, j: (l, 0, j)),
                  pl.BlockSpec((1, 1, tn), lambda l, j: (l, 0, j))],
        out_specs=pl.BlockSpec((1, rows, tn), lambda l, j: (l, 0, j)),
        out_shape=jax.ShapeDtypeStruct((n_l, rows, n_out), F32),
        compiler_params=_cparams(("arbitrary", "arbitrary")),
        name="ada",
    )(cc, w_ada, b_ada.reshape(n_l, 1, n_out))


def _mod_specs(n_b, grid_rank):
    if grid_rank == 2:
        return (pl.BlockSpec((1, 1, 6 * D_MODEL), lambda b, j: (b, 0, 0)),
                pl.BlockSpec((1, 1, 6 * D_MODEL), lambda b, j: (n_b, 0, 0)))
    return (pl.BlockSpec((1, 1, 6 * D_MODEL), lambda b, j, f: (b, 0, 0)),
            pl.BlockSpec((1, 1, 6 * D_MODEL), lambda b, j, f: (n_b, 0, 0)))


def _norm_kernel(x_ref, g_ref, mb_ref, mc_ref, h_ref, *, tm):
    is_ctx = _is_ctx_rows(tm, 1)
    h = _norm_mod(x_ref[0], g_ref[...], _mod_vec(mb_ref, mc_ref, 0, is_ctx), _mod_vec(mb_ref, mc_ref, 1, is_ctx))
    h_ref[0] = h.astype(h_ref.dtype)


def _norm1(x, g, modl):
    n_b = x.shape[0]
    tm = 768
    mb, mc = _mod_specs(n_b, 2)
    return pl.pallas_call(
        functools.partial(_norm_kernel, tm=tm),
        grid=(n_b, T_ALL // tm),
        in_specs=[pl.BlockSpec((1, tm, D_MODEL), lambda b, j: (b, j, 0)),
                  pl.BlockSpec((1, D_MODEL), lambda b, j: (0, 0)), mb, mc],
        out_specs=pl.BlockSpec((1, tm, D_MODEL), lambda b, j: (b, j, 0)),
        out_shape=jax.ShapeDtypeStruct(x.shape, BF16),
        compiler_params=_cparams(("parallel", "parallel")),
        name="norm1",
    )(x, g.reshape(1, D_MODEL), modl, modl)


def _mm_kernel(a_ref, w_ref, o_ref):
    o_ref[0] = _dot(a_ref[0], w_ref[...]).astype(o_ref.dtype)


def _mm(a, w, out_dtype, tn, name):
    n_b, t, k = a.shape
    n = w.shape[1]
    return pl.pallas_call(
        _mm_kernel,
        grid=(n_b, n // tn),
        in_specs=[pl.BlockSpec((1, t, k), lambda i, j: (i, 0, 0)),
                  pl.BlockSpec((k, tn), lambda i, j: (0, j))],
        out_specs=pl.BlockSpec((1, t, tn), lambda i, j: (i, 0, j)),
        out_shape=jax.ShapeDtypeStruct((n_b, t, n), out_dtype),
        compiler_params=_cparams(("parallel", "arbitrary")),
        name=name,
    )(a, w)


LOG2E = float(np.log2(np.e))
SUM_LANE = (HEAD_DIM, 0)


def _with_ones_lane(vblk, hh):
    lane = _lane(vblk.shape)
    keep = (lane < HEAD_DIM) if hh == 0 else (lane >= HEAD_DIM)
    return jnp.where(keep, vblk, jnp.where(lane == SUM_LANE[hh], 1.0, 0.0).astype(vblk.dtype))


def _attend_heads(o_ref, q_of, k_of, v_of, extra=None):
    def scores(h):
        q = q_of(h)
        s = _dot_nt(q, k_of(h))
        if extra is None:
            return (s,)
        return (s + extra[0](h), _dot_nt(q, extra[1](h)))

    def finish(h, sc):
        m = jnp.max(sc[0], axis=-1, keepdims=True)
        for s in sc[1:]:
            m = jnp.maximum(m, jnp.max(s, axis=-1, keepdims=True))
        o = _dot(jnp.exp2(sc[0] - m).astype(BF16), v_of(h))
        if extra is not None:
            o = o + _dot(jnp.exp2(sc[1] - m).astype(BF16), extra[2](h))
        lane = SUM_LANE[h % 2]
        return o / o[:, lane:lane + 1]

    nxt = scores(0)
    even = None
    for h in range(N_HEADS):
        cur = nxt
        if h + 1 < N_HEADS:
            nxt = scores(h + 1)
        o = finish(h, cur)
        if h % 2 == 0:
            even = o
        else:
            pair = jnp.where(_lane(o.shape) < HEAD_DIM, even, o)
            o_ref[0, :, LB * (h // 2):LB * (h // 2 + 1)] = pair.astype(o_ref.dtype)


def _pad_head(blk, hh):
    return jnp.where(_lane(blk.shape) < HEAD_DIM, blk if hh == 0 else pltpu.roll(blk, HEAD_DIM, 1), 0.0)


def _pad_heads(blk):
    return _pad_head(blk, 0), _pad_head(blk, 1)


def _head_rms(x, gain):
    ms = jnp.sum(x * x, axis=-1, keepdims=True) * (1.0 / HEAD_DIM)
    return x * lax.rsqrt(ms + NORM_EPS) * gain


def _gqa_kernel(pq_ref, pall_ref, cq_ref, sq_ref, call_ref, sall_ref, qg_ref, kg_ref, o_ref, k_scr, v_scr):
    j = pl.program_id(1)

    @pl.when(j == 0)
    def _prep():
        kblk = pall_ref[0, :, 512:640].astype(F32)
        for g, kh in enumerate(_pad_heads(kblk)):
            kh = _rope(_head_rms(kh, kg_ref[...]), call_ref[...], sall_ref[...], 16)
            k_scr[:, LB * g:LB * (g + 1)] = kh.astype(BF16)
        vblk = pall_ref[0, :, 640:768].astype(F32)
        vrot = pltpu.roll(vblk, HEAD_DIM, 1)
        for i, src in enumerate((vblk, vrot, vrot, vblk)):
            v_scr[:, LB * i:LB * (i + 1)] = _with_ones_lane(src, i % 2).astype(BF16)

    def attend(n_keys):
        group = N_HEADS // A_KV_HEADS

        def q_of(h):
            qh = _pad_head(pq_ref[0, :, LB * (h // 2):LB * (h // 2 + 1)].astype(F32), h % 2)
            qh = _rope(_head_rms(qh, qg_ref[...]), cq_ref[...], sq_ref[...], 16)
            return (qh * (HEAD_DIM ** -0.5 * LOG2E)).astype(BF16)

        def v_of(h):
            i = 2 * (h // group) + h % 2
            return v_scr[0:n_keys, LB * i:LB * (i + 1)]

        _attend_heads(o_ref, q_of, lambda h: k_scr[0:n_keys, LB * (h // group):LB * (h // group + 1)], v_of)

    @pl.when(j == 0)
    def _ctx():
        attend(CTX_LEN)

    @pl.when(j > 0)
    def _lat():
        attend(T_ALL)


def _gqa(p_a, cos, sin, q_gain, k_gain):
    n_b = p_a.shape[0]
    pad = lambda g: jnp.concatenate([g, jnp.zeros((HEAD_DIM,), F32)]).reshape(1, LB)
    tile = lambda b, j: (b, j, 0)
    whole = lambda b, j: (b, 0, 0)
    return pl.pallas_call(
        _gqa_kernel,
        grid=(n_b, N_QT),
        in_specs=[pl.BlockSpec((1, Q_TILE, A_IN), tile),
                  pl.BlockSpec((1, T_ALL, A_IN), whole),
                  pl.BlockSpec((Q_TILE, LB), lambda b, j: (j, 0)),
                  pl.BlockSpec((Q_TILE, LB), lambda b, j: (j, 0)),
                  pl.BlockSpec((T_ALL, LB), lambda b, j: (0, 0)),
                  pl.BlockSpec((T_ALL, LB), lambda b, j: (0, 0)),
                  pl.BlockSpec((1, LB), lambda b, j: (0, 0)),
                  pl.BlockSpec((1, LB), lambda b, j: (0, 0))],
        out_specs=pl.BlockSpec((1, Q_TILE, 512), tile),
        out_shape=jax.ShapeDtypeStruct((n_b, T_ALL, 512), BF16),
        scratch_shapes=[pltpu.VMEM((T_ALL, 256), BF16), pltpu.VMEM((T_ALL, 512), BF16)],
        compiler_params=_cparams(("parallel", "arbitrary")),
        name="gqa",
    )(p_a, p_a, cos, sin, cos, sin, pad(q_gain), pad(k_gain))


def _mla_kernel(pq_ref, pall_ref, cq_ref, sq_ref, call_ref, sall_ref, qg_ref, kvg_ref, wq_ref, wkv_ref,
                o_ref, k_scr, v_scr):
    j = pl.program_id(1)
    kw = N_HEADS * LB

    @pl.when(j == 0)
    def _prep():
        for i in range(N_QT):
            rows = slice(i * Q_TILE, (i + 1) * Q_TILE)
            ckv = pall_ref[0, rows, B_Q_LORA:B_Q_LORA + B_KV_LORA].astype(F32)
            n = ckv * lax.rsqrt(jnp.mean(ckv * ckv, axis=-1, keepdims=True) + NORM_EPS) * kvg_ref[...]
            kv = _dot(n.astype(BF16), wkv_ref[...])
            kr = _rope(pall_ref[0, rows, 640:768].astype(F32), call_ref[rows, :], sall_ref[rows, :], 8)
            for h in range(N_HEADS):
                k_scr[rows, LB * h:LB * (h + 1)] = (kv[:, LB * h:LB * (h + 1)] + kr).astype(BF16)
            for h in range(N_HEADS):
                v_scr[rows, LB * h:LB * (h + 1)] = _with_ones_lane(kv[:, kw + LB * h:kw + LB * (h + 1)], h % 2).astype(BF16)

    def attend(n_keys):
        cq = pq_ref[0, :, 0:B_Q_LORA].astype(F32)
        n = cq * lax.rsqrt(jnp.mean(cq * cq, axis=-1, keepdims=True) + NORM_EPS) * qg_ref[...]
        q = _dot(n.astype(BF16), wq_ref[...])
        scale = (B_NOPE + B_ROPE) ** -0.5 * LOG2E
        _attend_heads(
            o_ref,
            lambda h: (_rope(q[:, LB * h:LB * (h + 1)], cq_ref[...], sq_ref[...], 8) * scale).astype(BF16),
            lambda h: k_scr[0:n_keys, LB * h:LB * (h + 1)],
            lambda h: v_scr[0:n_keys, LB * h:LB * (h + 1)])

    @pl.when(j == 0)
    def _ctx():
        attend(CTX_LEN)

    @pl.when(j > 0)
    def _lat():
        attend(T_ALL)


def _mla(p, cos, sin, q_gain, kv_gain, wq, wkv):
    n_b = p.shape[0]
    tile = lambda b, j: (b, j, 0)
    const = lambda b, j: (0, 0)
    return pl.pallas_call(
        _mla_kernel,
        grid=(n_b, N_QT),
        in_specs=[pl.BlockSpec((1, Q_TILE, B_PAD), lambda b, j: (b, j, PROJ_B_BLOCK)),
                  pl.BlockSpec((1, T_ALL, B_PAD), lambda b, j: (b, 0, PROJ_B_BLOCK)),
                  pl.BlockSpec((Q_TILE, LB), lambda b, j: (j, 0)),
                  pl.BlockSpec((Q_TILE, LB), lambda b, j: (j, 0)),
                  pl.BlockSpec((T_ALL, LB), const),
                  pl.BlockSpec((T_ALL, LB), const),
                  pl.BlockSpec((1, B_Q_LORA), const),
                  pl.BlockSpec((1, B_KV_LORA), const),
                  pl.BlockSpec(wq.shape, const),
                  pl.BlockSpec(wkv.shape, const)],
        out_specs=pl.BlockSpec((1, Q_TILE, 512), tile),
        out_shape=jax.ShapeDtypeStruct((n_b, T_ALL, 512), BF16),
        scratch_shapes=[pltpu.VMEM((T_ALL, N_HEADS * LB), BF16), pltpu.VMEM((T_ALL, N_HEADS * LB), BF16)],
        compiler_params=_cparams(("parallel", "arbitrary")),
        name="mla",
    )(p, p, cos, sin, cos, sin, q_gain.reshape(1, -1), kv_gain.reshape(1, -1), wq, wkv)


def _nat_kernel(pq_ref, pall_ref, bias_ref, o_ref, k_scr, v_scr):
    j = pl.program_id(1)

    @pl.when(j == 0)
    def _prep():
        for jb in range(N_HEADS // 2):
            kblk = pall_ref[0, :, 512 + LB * jb:512 + LB * (jb + 1)].astype(F32)
            for hh, kh in enumerate(_pad_heads(kblk)):
                h = 2 * jb + hh
                k_scr[:, LB * h:LB * (h + 1)] = kh.astype(BF16)
            vblk = pall_ref[0, :, 1024 + LB * jb:1024 + LB * (jb + 1)]
            for hh in range(2):
                v_scr[:, LB * (2 * jb + hh):LB * (2 * jb + hh + 1)] = _with_ones_lane(vblk, hh)

    def q_of(h):
        qh = _pad_head(pq_ref[0, :, LB * (h // 2):LB * (h // 2 + 1)].astype(F32), h % 2)
        return (qh * (HEAD_DIM ** -0.5 * LOG2E)).astype(BF16)

    hs = lambda h: slice(LB * h, LB * (h + 1))

    @pl.when(j == 0)
    def _ctx():
        _attend_heads(o_ref, q_of, lambda h: k_scr[0:CTX_LEN, hs(h)], lambda h: v_scr[0:CTX_LEN, hs(h)])

    @pl.when(j > 0)
    def _lat():
        first_row = jnp.clip(NAT_QROWS * (j - 1) - NA_ROWS // 2, 0, N_ROWS - NAT_KROWS)
        start = pl.multiple_of(CTX_LEN + first_row * GRID_W, Q_TILE)
        win = pl.ds(start, NAT_KWIN)
        _attend_heads(o_ref, q_of, lambda h: k_scr[win, hs(h)], lambda h: v_scr[win, hs(h)],
                      extra=(lambda h: bias_ref[0, h],
                             lambda h: k_scr[0:CTX_LEN, hs(h)],
                             lambda h: v_scr[0:CTX_LEN, hs(h)]))


def _nat_bias_table(rel_bias):
    n_dr, n_dc = 2 * NA_ROWS - 1, 2 * NA_COLS - 1
    cols = np.arange(GRID_W)
    pick_col = (cols[None, None, :] - cols[None, :, None] + NA_COLS - 1 == np.arange(n_dc)[:, None, None])
    pick_row = np.zeros((3, NAT_QROWS, NAT_KROWS, n_dr), np.float32)
    valid = np.zeros((3, Q_TILE, NAT_KWIN), bool)
    for t, qb in enumerate((0, 1, N_ROWS // NAT_QROWS - 1)):
        first_row = int(np.clip(NAT_QROWS * qb - NA_ROWS // 2, 0, N_ROWS - NAT_KROWS))
        rq = NAT_QROWS * qb + np.arange(NAT_QROWS)
        rk = first_row + np.arange(NAT_KROWS)
        pick_row[t] = (rk[None, :, None] - rq[:, None, None] + NA_ROWS - 1 == np.arange(n_dr)[None, None, :])
        ql, kl = np.arange(Q_TILE), np.arange(NAT_KWIN)
        r, c = NAT_QROWS * qb + ql // GRID_W, ql % GRID_W
        kr, kc = first_row + kl // GRID_W, kl % GRID_W
        r0 = np.clip(r - NA_ROWS // 2, 0, N_ROWS - NA_ROWS)
        c0 = np.clip(c - NA_COLS // 2, 0, GRID_W - NA_COLS)
        valid[t] = ((kr[None, :] >= r0[:, None]) & (kr[None, :] < r0[:, None] + NA_ROWS)
                    & (kc[None, :] >= c0[:, None]) & (kc[None, :] < c0[:, None] + NA_COLS))
    bias_log2 = rel_bias.astype(F32) * LOG2E
    by_col = jnp.einsum('hdc,cab->hdab', bias_log2, jnp.asarray(pick_col, F32), precision=lax.Precision.HIGHEST)
    tab = jnp.einsum('tqkd,hdab->thqakb', jnp.asarray(pick_row), by_col, precision=lax.Precision.HIGHEST)
    return jnp.where(jnp.asarray(valid)[:, None], tab.reshape(3, N_HEADS, Q_TILE, NAT_KWIN), NEG_BIG)


def _nat(p, bias_tab):
    n_b = p.shape[0]
    tile = lambda b, j: (b, j, 0)
    n_lat_tiles = N_QT - 1

    def bias_idx(b, j):
        qb = j - 1
        return (jnp.where(qb <= 0, 0, jnp.where(qb == n_lat_tiles - 1, 2, 1)), 0, 0, 0)

    return pl.pallas_call(
        _nat_kernel,
        grid=(n_b, N_QT),
        in_specs=[pl.BlockSpec((1, Q_TILE, D_IN), lambda b, j: (b, j, PROJ_D_BLOCK)),
                  pl.BlockSpec((1, T_ALL, D_IN), lambda b, j: (b, 0, PROJ_D_BLOCK)),
                  pl.BlockSpec((1, N_HEADS, Q_TILE, NAT_KWIN), bias_idx)],
        out_specs=pl.BlockSpec((1, Q_TILE, 512), tile),
        out_shape=jax.ShapeDtypeStruct((n_b, T_ALL, 512), BF16),
        scratch_shapes=[pltpu.VMEM((T_ALL, N_HEADS * LB), BF16), pltpu.VMEM((T_ALL, N_HEADS * LB), BF16)],
        compiler_params=_cparams(("parallel", "arbitrary")),
        name="nat",
    )(p, p, bias_tab)


def _seg_sum(x, ones3):
    return _dot3_exact_rhs(x, ones3)


def _head_ones3():
    ones_bd = np.kron(np.eye(N_HEADS), np.ones((HEAD_DIM, HEAD_DIM)))
    return jnp.asarray(np.concatenate([ones_bd] * 3, axis=0), BF16)


def _rwkv_prep_kernel(z_ref, zp_ref, zn_ref, mu_ref, w0_ref, a0_ref, kk_ref, ka_ref, rk_ref,
                      wd3_ref, wa3_ref, wg3_ref, ones_ref,
                      r_out, v_out, kk_out, kd_out, lw_out, bd_out, bonus_out, g_out):
    j = pl.program_id(1)
    z = z_ref[0]
    tm = z.shape[0]
    row = lax.broadcasted_iota(jnp.int32, (tm, 1), 0)
    prev_row = jnp.where(j <= 1, 0.0, zp_ref[0, 7:8, :])
    next_row = jnp.where((j == 0) | (j == N_QT - 1), 0.0, zn_ref[0, 0:1, :])
    z_prev = jnp.where(row == 0, prev_row, pltpu.roll(z, 1, 0))
    z_next = jnp.where(row == tm - 1, next_row, pltpu.roll(z, tm - 1, 0))
    zs = z + (0.5 * (z_prev + z_next) - z) * mu_ref[...]

    r = zs[:, 0:512]
    k = zs[:, 512:1024]
    v = zs[:, 1024:1536]
    w_lo = zs[:, 1536:1664]
    a_lo = zs[:, 1664:1792]
    g_lo = zs[:, 1792:1920]
    ones_bd = ones_ref[...]

    def lora(x, w3_ref):
        xh, xl = _split_bf16(x)
        return _dot(jnp.concatenate([xh, xh, xl], axis=1), w3_ref[...])

    kkr = k * kk_ref[...]
    nrm = jnp.maximum(jnp.sqrt(_seg_sum(kkr * kkr, ones_bd)), 1e-12)
    kk = kkr / nrm
    dec = lora(jnp.tanh(w_lo), wd3_ref)
    aaa = lora(a_lo, wa3_ref)
    g = lora(_sigmoid(g_lo), wg3_ref)
    r_out[0] = r
    v_out[0] = v
    kk_out[0] = kk
    g_out[0] = g
    ksum = None
    for d in range(2):
        u = -(w0_ref[d:d + 1, :] + dec[:, 512 * d:512 * (d + 1)])
        softplus = jnp.maximum(u, 0.0) + jnp.log(1.0 + jnp.exp(-jnp.abs(u)))
        logw = -softplus - 0.5
        lw_out[d, 0] = -jnp.exp(logw)
        a = _sigmoid(a0_ref[d:d + 1, :] + aaa[:, 512 * d:512 * (d + 1)])
        bd_out[d, 0] = kk * a
        kd = k * (1.0 + (a - 1.0) * ka_ref[...])
        kd_out[d, 0] = kd
        ksum = kd if ksum is None else ksum + kd
    bonus_out[0] = _seg_sum(r * ksum * rk_ref[...], ones_bd) * v


def _rwkv_prep(p_c, lp):
    n_b = p_c.shape[0]
    tile = lambda b, j: (b, j, 0)
    const = lambda b, j: (0, 0)
    blocks8 = T_ALL // 8
    tpb = Q_TILE // 8
    prev = lambda b, j: (b, jnp.maximum(j * tpb - 1, 0), 0)
    nxt = lambda b, j: (b, jnp.minimum((j + 1) * tpb, blocks8 - 1), 0)
    o3 = jax.ShapeDtypeStruct((n_b, T_ALL, C_W), F32)
    o4 = jax.ShapeDtypeStruct((2, n_b, T_ALL, C_W), F32)
    s3 = pl.BlockSpec((1, Q_TILE, C_W), tile)
    s4 = pl.BlockSpec((2, 1, Q_TILE, C_W), lambda b, j: (0, b, j, 0))
    small = [lp['c_mu'].reshape(1, C_IN), lp['c_w0'], lp['c_a0'], lp['c_k_k'].reshape(1, C_W),
             lp['c_k_a'].reshape(1, C_W), lp['c_r_k'].reshape(1, C_W)]
    bd2 = lambda w: jnp.concatenate(
        [jnp.concatenate([w[0], jnp.zeros_like(w[0])], 1), jnp.concatenate([jnp.zeros_like(w[1]), w[1]], 1)], 0)
    mats = []
    for w in (bd2(lp['c_w_decay']), bd2(lp['c_w_aaa']), lp['c_w_gate']):
        hi, lo = _split_bf16(w)
        mats.append(jnp.concatenate([hi, lo, hi], axis=0))
    ins = small + mats + [_head_ones3()]
    return pl.pallas_call(
        _rwkv_prep_kernel,
        grid=(n_b, N_QT),
        in_specs=[pl.BlockSpec((1, Q_TILE, C_IN), tile),
                  pl.BlockSpec((1, 8, C_IN), prev),
                  pl.BlockSpec((1, 8, C_IN), nxt)] + [pl.BlockSpec(a.shape, const) for a in ins],
        out_specs=[s3, s3, s3, s4, s4, s4, s3, s3],
        out_shape=[o3, o3, o3, o4, o4, o4, o3, o3],
        compiler_params=_cparams(("parallel", "parallel")),
        name="rwkv_prep",
    )(p_c, p_c, p_c, *ins)


def _dot3_exact_rhs_lhs(m3_bf16, a):
    return _dot(m3_bf16, jnp.concatenate(_split3_bf16(a), axis=0))


N_INV_LEVELS = 6
SCAN_BATCH = 2
SCAN_TILE = 256
MK_STRICT, MK_INCL, MK_LEVEL0 = 0, 1, 2


def _scan_masks():
    masks, cums = [], []
    for fwd in (True, False):
        t = np.arange(CHUNK) if fwd else CHUNK - 1 - np.arange(CHUNK)
        tr, tc = t[:, None], t[None, :]
        levels = [((tr // (2 * h)) == (tc // (2 * h))) & (((tr // h) % 2) == 1) & (((tc // h) % 2) == 0)
                  for h in (2 ** k for k in range(N_INV_LEVELS))]
        masks.append(np.tile(np.stack([tr > tc, tr >= tc] + levels), (1, 1, HG)))
        cums.append(np.tile(tr >= tc, (1, 3)))
    return jnp.asarray(np.stack(masks), F32), jnp.asarray(np.stack(cums), BF16)


def _rwkv_scan_kernel(rf_ref, vf_ref, kkf_ref, kdf_ref, lwf_ref, bdf_ref,
                      rb_ref, vb_ref, kkb_ref, kdb_ref, lwb_ref, bdb_ref, mk_ref, cum_ref, yf_ref, yb_ref, s_scr):
    j = pl.program_id(1)
    lane_head = lax.broadcasted_iota(jnp.int32, (CHUNK, HGW), 1) // HEAD_DIM
    eye = (lax.broadcasted_iota(jnp.int32, (CHUNK, HGW), 0)
           == lax.broadcasted_iota(jnp.int32, (CHUNK, HGW), 1) % CHUNK).astype(F32)
    dir_refs = ((rf_ref, vf_ref, kkf_ref, kdf_ref, lwf_ref, bdf_ref, yf_ref),
                (rb_ref, vb_ref, kkb_ref, kdb_ref, lwb_ref, bdb_ref, yb_ref))
    chunks_per_tile = SCAN_TILE // CHUNK

    def masked(d, k, x):
        return jnp.where(mk_ref[d, k] > 0.5, x, 0.0)

    def stack(x):
        return jnp.concatenate([jnp.where(lane_head == p, x, jnp.zeros_like(x)) for p in range(HG)], axis=0)

    def head_transpose(x):
        t = stack(x).T
        return t[0:HEAD_DIM] + t[HEAD_DIM:2 * HEAD_DIM] + t[2 * HEAD_DIM:3 * HEAD_DIM] + t[3 * HEAD_DIM:4 * HEAD_DIM]

    @pl.when(j == 0)
    def _init():
        s_scr[...] = jnp.zeros_like(s_scr)

    def body(i, carry):
        ch = []
        for bi in range(SCAN_BATCH):
            for d in range(2):
                r_ref, v_ref, kk_ref, kd_ref, lw_ref, bd_ref, y_ref = dir_refs[d]
                c = i if d == 0 else chunks_per_tile - 1 - i
                rows = pl.ds(pl.multiple_of(c * CHUNK, CHUNK), CHUNK)
                lw = lw_ref[0, bi, rows, :]
                cum = _dot3_exact_rhs_lhs(cum_ref[d], lw)
                total = jnp.sum(lw, axis=0, keepdims=True)
                e_pos, e_neg, e_prev, e_rest = jnp.exp(cum), jnp.exp(-cum), jnp.exp(cum - lw), jnp.exp(total - cum)
                p_end = jnp.broadcast_to(jnp.exp(total), (CHUNK, C_W))
                for g in range(C_W // HGW):
                    ln = slice(g * HGW, (g + 1) * HGW)
                    kk, bd, kd = kk_ref[bi, rows, ln], bd_ref[0, bi, rows, ln], kd_ref[0, bi, rows, ln]
                    al = (kk * e_prev[:, ln]).astype(BF16)
                    rh = (r_ref[bi, rows, ln] * e_pos[:, ln]).astype(BF16)
                    ch.append(dict(
                        idx=(2 * bi + d) * (C_W // HGW) + g, d=d, bi=bi, rows=rows, ln=ln, y_ref=y_ref,
                        al=al, rh=rh, al_rh=jnp.concatenate([al, rh], axis=0), al4=stack(al),
                        be4=stack((bd * e_neg[:, ln]).astype(BF16)), ka4=stack((kd * e_neg[:, ln]).astype(BF16)),
                        bee_t=head_transpose(bd * e_rest[:, ln]).astype(BF16),
                        kae_t=head_transpose(kd * e_rest[:, ln]).astype(BF16),
                        p_end_t=head_transpose(p_end[:, ln]),
                        v4=stack(v_ref[bi, rows, ln].astype(BF16))))

        def stage(fn):
            for c_ in ch:
                c_.update(fn(c_, c_['d']))

        def nt_masks(c_, d, key, lo_name, hi_name, lo_f32=False):
            prod = _dot_nt(c_['al_rh'], c_[key])
            lo = masked(d, MK_STRICT, prod[0:CHUNK])
            return {lo_name: lo if lo_f32 else lo.astype(BF16),
                    hi_name: masked(d, MK_INCL, prod[CHUNK:2 * CHUNK]).astype(BF16)}

        stage(lambda c_, d: nt_masks(c_, d, 'be4', 'l_ab', 'm_rb', lo_f32=True))
        stage(lambda c_, d: nt_masks(c_, d, 'ka4', 'l_ak', 'm_rk'))

        def value_products(c_, d):
            prod = _dot(jnp.concatenate([c_['l_ak'], c_['kae_t'], c_['m_rk']], axis=0), c_['v4'])
            return dict(lv4=stack(prod[0:CHUNK].astype(BF16)), hv=prod[CHUNK:2 * CHUNK], yv=prod[2 * CHUNK:3 * CHUNK])

        stage(lambda c_, d: dict(tb=(eye - masked(d, MK_LEVEL0, c_['l_ab'])).astype(BF16)))
        for level in range(1, N_INV_LEVELS):
            stage(lambda c_, d: dict(
                t4=stack(_dot(masked(d, MK_LEVEL0 + level, c_['l_ab']).astype(BF16), stack(c_['tb'])).astype(BF16))))
            if level == 1:
                stage(value_products)
            stage(lambda c_, d: dict(tb=c_['tb'] - _dot(c_['tb'], c_['t4']).astype(BF16)))
        stage(lambda c_, d: dict(w=_dot(c_['tb'], c_['al4']).astype(BF16)))
        stage(lambda c_, d: dict(u0=_dot(c_['tb'], c_['lv4'])))
        stage(lambda c_, d: dict(s=s_scr[c_['idx']]))
        stage(lambda c_, d: dict(on_s=_dot(jnp.concatenate([c_['w'], c_['rh']], axis=0), stack(c_['s'].astype(BF16)))))
        stage(lambda c_, d: dict(u4=stack((c_['on_s'][0:CHUNK] + c_['u0']).astype(BF16))))
        stage(lambda c_, d: dict(on_u=_dot(jnp.concatenate([c_['m_rb'], c_['bee_t']], axis=0), c_['u4'])))
        for c_ in ch:
            s_scr[c_['idx']] = c_['s'] * c_['p_end_t'] - c_['on_u'][CHUNK:2 * CHUNK] + c_['hv']
        for c_ in ch:
            c_['y_ref'][c_['bi'], c_['rows'], c_['ln']] = c_['on_s'][CHUNK:2 * CHUNK] - c_['on_u'][0:CHUNK] + c_['yv']
        return carry

    lax.fori_loop(0, chunks_per_tile, body, 0)


def _rwkv_scan(r, v, kk, kd, lw, bd):
    n_b = r.shape[0]
    assert n_b % SCAN_BATCH == 0
    n_tiles, ctx_tiles = T_ALL // SCAN_TILE, CTX_LEN // SCAN_TILE
    bwd_tile = lambda j: jnp.where(j < ctx_tiles, ctx_tiles - 1 - j, n_tiles + ctx_tiles - 1 - j)
    f3 = pl.BlockSpec((SCAN_BATCH, SCAN_TILE, C_W), lambda b, j: (b, j, 0))
    f4 = pl.BlockSpec((1, SCAN_BATCH, SCAN_TILE, C_W), lambda b, j: (0, b, j, 0))
    b3 = pl.BlockSpec((SCAN_BATCH, SCAN_TILE, C_W), lambda b, j: (b, bwd_tile(j), 0))
    b4 = pl.BlockSpec((1, SCAN_BATCH, SCAN_TILE, C_W), lambda b, j: (1, b, bwd_tile(j), 0))
    y_shape = jax.ShapeDtypeStruct((n_b, T_ALL, C_W), F32)
    masks, cums = _scan_masks()
    return pl.pallas_call(
        _rwkv_scan_kernel,
        grid=(n_b // SCAN_BATCH, n_tiles),
        in_specs=[f3, f3, f3, f4, f4, f4, b3, b3, b3, b4, b4, b4,
                  pl.BlockSpec(masks.shape, lambda b, j: (0, 0, 0, 0)),
                  pl.BlockSpec(cums.shape, lambda b, j: (0, 0, 0))],
        out_specs=[f3, b3],
        out_shape=[y_shape, y_shape],
        scratch_shapes=[pltpu.VMEM((SCAN_BATCH * 2 * (C_W // HGW), HEAD_DIM, HGW), F32)],
        compiler_params=_cparams(("parallel", "arbitrary")),
        name="rwkv_scan",
    )(r, v, kk, kd, lw, bd, r, v, kk, kd, lw, bd, masks, cums)


def _rwkv_out_kernel(yf_ref, yb_ref, bonus_ref, g_ref, gw_ref, gb_ref, ones_ref, o_ref):
    y = yf_ref[0] + yb_ref[0]
    ones_bd = ones_ref[...]
    mean = _seg_sum(y, ones_bd) * (1.0 / HEAD_DIM)
    yc = y - mean
    var = _seg_sum(yc * yc, ones_bd) * (1.0 / HEAD_DIM)
    yn = yc * lax.rsqrt(var + C_GN_EPS) * gw_ref[...] + gb_ref[...]
    o_ref[0] = ((yn + bonus_ref[0]) * g_ref[0]).astype(o_ref.dtype)


def _rwkv_out(y_f, y_b, bonus, g, gn_w, gn_b):
    n_b = bonus.shape[0]
    tm = 768
    tile = lambda b, j: (b, j, 0)
    const = lambda b, j: (0, 0)
    ones_bd = _head_ones3()
    return pl.pallas_call(
        _rwkv_out_kernel,
        grid=(n_b, T_ALL // tm),
        in_specs=[pl.BlockSpec((1, tm, C_W), tile), pl.BlockSpec((1, tm, C_W), tile),
                  pl.BlockSpec((1, tm, C_W), tile), pl.BlockSpec((1, tm, C_W), tile),
                  pl.BlockSpec((1, C_W), const), pl.BlockSpec((1, C_W), const),
                  pl.BlockSpec(ones_bd.shape, const)],
        out_specs=pl.BlockSpec((1, tm, C_W), tile),
        out_shape=jax.ShapeDtypeStruct((n_b, T_ALL, C_W), BF16),
        compiler_params=_cparams(("parallel", "parallel")),
        name="rwkv_out",
    )(y_f, y_b, bonus, g, gn_w.reshape(1, C_W), gn_b.reshape(1, C_W), ones_bd)


def _merge_kernel(x_ref, oa_ref, ob_ref, oc_ref, od_ref, gate_ref, wb_ref, wo_ref, mb_ref, mc_ref, out_ref, *, tm):
    is_ctx = _is_ctx_rows(tm, 1)
    y = None
    for i, o_ref in enumerate((oa_ref, ob_ref, oc_ref, od_ref)):
        z = _dot(o_ref[0], wb_ref[i])
        sg = _sigmoid(gate_ref[0, :, i * D_MODEL:(i + 1) * D_MODEL].astype(F32))
        y = sg * z if y is None else y + sg * z
    z = _dot(y.astype(BF16), wo_ref[...])
    out_ref[0] = x_ref[0] + _mod_vec(mb_ref, mc_ref, 2, is_ctx) * z


def _merge(x, outs, gates, wb, wo, modl):
    n_b = x.shape[0]
    tm = 768
    tile = lambda b, j: (b, j, 0)
    mb, mc = _mod_specs(n_b, 2)
    o_spec = pl.BlockSpec((1, tm, 512), tile)
    return pl.pallas_call(
        functools.partial(_merge_kernel, tm=tm),
        grid=(n_b, T_ALL // tm),
        in_specs=[pl.BlockSpec((1, tm, D_MODEL), tile), o_spec, o_spec, o_spec, o_spec,
                  pl.BlockSpec((1, tm, GATE_IN), tile),
                  pl.BlockSpec(wb.shape, lambda b, j: (0, 0, 0)),
                  pl.BlockSpec(wo.shape, lambda b, j: (0, 0)), mb, mc],
        out_specs=pl.BlockSpec((1, tm, D_MODEL), tile),
        out_shape=jax.ShapeDtypeStruct(x.shape, F32),
        compiler_params=_cparams(("parallel", "parallel")),
        name="merge",
    )(x, *outs, gates, wb, wo, modl, modl)


def _mlp_kernel(x_ref, g_ref, mb_ref, mc_ref, w1_ref, w2_ref, out_ref, h_scr, acc_scr, *, tm, n_f):
    f = pl.program_id(2)
    is_ctx = _is_ctx_rows(tm, 1)

    @pl.when(f == 0)
    def _init():
        h = _norm_mod(x_ref[0], g_ref[...], _mod_vec(mb_ref, mc_ref, 3, is_ctx), _mod_vec(mb_ref, mc_ref, 4, is_ctx))
        h_scr[...] = h.astype(BF16)
        acc_scr[...] = jnp.zeros_like(acc_scr)

    a = jnp.square(jnp.maximum(_dot(h_scr[...], w1_ref[...]), 0.0))
    acc_scr[...] += _dot(a.astype(BF16), w2_ref[...])

    @pl.when(f == n_f - 1)
    def _fin():
        out_ref[0] = x_ref[0] + _mod_vec(mb_ref, mc_ref, 5, is_ctx) * acc_scr[...]


def _mlp(x, g, w1, w2, modl):
    n_b = x.shape[0]
    tm, tf = 1152, 2048
    n_f = D_FF // tf
    tile = lambda b, j, f: (b, j, 0)
    mb, mc = _mod_specs(n_b, 3)
    return pl.pallas_call(
        functools.partial(_mlp_kernel, tm=tm, n_f=n_f),
        grid=(n_b, T_ALL // tm, n_f),
        in_specs=[pl.BlockSpec((1, tm, D_MODEL), tile),
                  pl.BlockSpec((1, D_MODEL), lambda b, j, f: (0, 0)), mb, mc,
                  pl.BlockSpec((D_MODEL, tf), lambda b, j, f: (0, f)),
                  pl.BlockSpec((tf, D_MODEL), lambda b, j, f: (f, 0))],
        out_specs=pl.BlockSpec((1, tm, D_MODEL), tile),
        out_shape=jax.ShapeDtypeStruct(x.shape, F32),
        scratch_shapes=[pltpu.VMEM((tm, D_MODEL), BF16), pltpu.VMEM((tm, D_MODEL), F32)],
        compiler_params=_cparams(("parallel", "parallel", "arbitrary")),
        name="mlp",
    )(x, g.reshape(1, D_MODEL), modl, modl, w1, w2)


def _final_kernel(x_ref, g_ref, o_ref):
    x = x_ref[0]
    o_ref[0] = x * lax.rsqrt(jnp.mean(x * x, axis=-1, keepdims=True) + NORM_EPS) * g_ref[...]


def _final_norm(x, g):
    n_b = x.shape[0]
    tm = Q_TILE
    return pl.pallas_call(
        _final_kernel,
        grid=(n_b, SEQ // tm),
        in_specs=[pl.BlockSpec((1, tm, D_MODEL), lambda b, j: (b, j + CTX_LEN // tm, 0)),
                  pl.BlockSpec((1, D_MODEL), lambda b, j: (0, 0))],
        out_specs=pl.BlockSpec((1, tm, D_MODEL), lambda b, j: (b, j, 0)),
        out_shape=jax.ShapeDtypeStruct((n_b, SEQ, D_MODEL), F32),
        compiler_params=_cparams(("parallel", "parallel")),
        name="final_norm",
    )(x, g.reshape(1, D_MODEL))


def _rope_tables(half, lane0):
    t = np.arange(SEQ)
    inv = ROPE_THETA ** (-np.arange(half, dtype=np.float64) / half)
    cos = np.ones((T_ALL, V7X_LANES), np.float64)
    sin = np.zeros((T_ALL, V7X_LANES), np.float64)
    for part, pos in enumerate((t // GRID_W, t % GRID_W)):
        ang = pos[:, None].astype(np.float64) * inv[None, :]
        ang = ang.astype(np.float32).astype(np.float64)
        base = lane0 + 2 * half * part
        cos[CTX_LEN:, base:base + half] = np.cos(ang)
        cos[CTX_LEN:, base + half:base + 2 * half] = np.cos(ang)
        sin[CTX_LEN:, base:base + half] = -np.sin(ang)
        sin[CTX_LEN:, base + half:base + 2 * half] = np.sin(ang)
    return jnp.asarray(cos, F32), jnp.asarray(sin, F32)


def _layer_weights(l, w_in, b_w_q_up, b_w_kv_up, w_branch, w_out, w_mlp1, w_mlp2):
    wi = w_in[l]
    o_b = A_IN
    o_c = o_b + B_IN
    o_d = o_c + C_IN
    o_g = o_d + D_IN
    w_a = wi[:, :o_b]
    wb_raw = wi[:, o_b:o_c]
    z = lambda n: jnp.zeros((D_MODEL, n), F32)
    w_b = jnp.concatenate([wb_raw[:, :B_Q_LORA + B_KV_LORA], z(64), wb_raw[:, B_Q_LORA + B_KV_LORA:], z(32)], 1)
    wq = b_w_q_up[l].reshape(B_Q_LORA, N_HEADS, B_NOPE + B_ROPE)
    wq = jnp.concatenate([wq, jnp.zeros((B_Q_LORA, N_HEADS, LB - B_NOPE - B_ROPE), F32)], -1)
    wkv = b_w_kv_up[l].reshape(B_KV_LORA, N_HEADS, 2 * HEAD_DIM)
    zk = jnp.zeros((B_KV_LORA, N_HEADS, HEAD_DIM), F32)
    wk = jnp.concatenate([wkv[:, :, :B_NOPE], zk], -1)
    even = (jnp.arange(N_HEADS) % 2 == 0)[None, :, None]
    wv = jnp.concatenate([jnp.where(even, wkv[:, :, B_NOPE:], 0.0), jnp.where(even, 0.0, wkv[:, :, B_NOPE:])], -1)
    return dict(
        w_abd=jnp.concatenate([w_a, w_b, wi[:, o_d:o_g]], 1).astype(BF16),
        w_c=wi[:, o_c:o_d].astype(BF16), w_g=wi[:, o_g:].astype(BF16),
        wq=wq.reshape(B_Q_LORA, N_HEADS * LB).astype(BF16),
        wkv=jnp.concatenate([wk.reshape(B_KV_LORA, -1), wv.reshape(B_KV_LORA, -1)], 1).astype(BF16),
        w_branch=w_branch[l].astype(BF16), w_out=w_out[l].astype(BF16),
        w_mlp1=w_mlp1[l].astype(BF16), w_mlp2=w_mlp2[l].astype(BF16))


def kernel(x, c, ctx, c_ctx, w_ada, b_ada, g_norm1, g_norm2, w_in, a_q_gain, a_k_gain, b_q_gain, b_kv_gain,
           b_w_q_up, b_w_kv_up, c_mu, c_w0, c_w_decay, c_a0, c_w_aaa, c_w_gate, c_k_k, c_k_a, c_r_k,
           c_gn_w, c_gn_b, d_rel_bias, w_branch, w_out, w_mlp1, w_mlp2, g_final):
    n_b = x.shape[0]
    assert x.shape[1:] == (SEQ, D_MODEL) and ctx.shape[1:] == (CTX_LEN, D_MODEL)
    mod_rows = ((n_b + 1 + 7) // 8) * 8
    cc = jnp.concatenate([c, c_ctx[None, :], jnp.zeros((mod_rows - n_b - 1, D_MODEL), F32)], 0)
    mod_all = _ada(cc, w_ada, b_ada)
    cos_a, sin_a = _rope_tables(16, 0)
    cos_b, sin_b = _rope_tables(8, B_NOPE)
    xs = jnp.concatenate([ctx, x], axis=1)
    for l in range(DEPTH):
        lw = _layer_weights(l, w_in, b_w_q_up, b_w_kv_up, w_branch, w_out, w_mlp1, w_mlp2)
        lp = dict(c_mu=c_mu[l], c_w0=c_w0[l], c_w_decay=c_w_decay[l], c_a0=c_a0[l], c_w_aaa=c_w_aaa[l],
                  c_w_gate=c_w_gate[l], c_k_k=c_k_k[l], c_k_a=c_k_a[l], c_r_k=c_r_k[l])
        modl = mod_all[l].reshape(mod_rows, 1, 6 * D_MODEL)
        h = _norm1(xs, g_norm1[l], modl)
        p_abd = _mm(h, lw['w_abd'], BF16, 1024, "w_in_abd")
        p_c = _mm(h, lw['w_c'], F32, 640, "w_in_c")
        p_g = _mm(h, lw['w_g'], BF16, 1024, "w_in_g")
        o_a = _gqa(p_abd, cos_a, sin_a, a_q_gain[l], a_k_gain[l])
        o_b = _mla(p_abd, cos_b, sin_b, b_q_gain[l], b_kv_gain[l], lw['wq'], lw['wkv'])
        r, v, kk, kd, lwd, bd, bonus, g = _rwkv_prep(p_c, lp)
        y_f, y_b = _rwkv_scan(r, v, kk, kd, lwd, bd)
        o_c = _rwkv_out(y_f, y_b, bonus, g, c_gn_w[l], c_gn_b[l])
        o_d = _nat(p_abd, _nat_bias_table(d_rel_bias[l]))
        xs = _merge(xs, (o_a, o_b, o_c, o_d), p_g, lw['w_branch'], lw['w_out'], modl)
        xs = _mlp(xs, g_norm2[l], lw['w_mlp1'], lw['w_mlp2'], modl)
    return _final_norm(xs, g_final)
```

```python
import functools

import numpy as np
import jax
import jax.numpy as jnp
from jax import lax
from jax.experimental import pallas as pl
from jax.experimental.pallas import tpu as pltpu

F32 = jnp.float32
BF16 = jnp.bfloat16

D_MODEL = 1024
SEQ = 2048
DEPTH = 2
GRID_W = 64
N_ROWS = SEQ // GRID_W
CTX_LEN = 256
T_ALL = CTX_LEN + SEQ
HEAD_DIM = 64
ROPE_THETA = 10000.0
NORM_EPS = 1e-6
N_HEADS = 8
A_KV_HEADS = 2
B_Q_LORA = 384
B_KV_LORA = 256
B_NOPE = 64
B_ROPE = 32
C_W = 512
C_GN_EPS = 64e-5
NA_ROWS = 8
NA_COLS = 16
D_FF = 4 * D_MODEL
A_IN = 768
B_IN = 672
C_IN = 1920
D_IN = 1536
GATE_IN = 4096
B_PAD = 768
PROJ_B_BLOCK = A_IN // B_PAD
PROJ_D_BLOCK = (A_IN + B_PAD) // D_IN

V7X_LANES = 128
V7X_VMEM_LIMIT = 56 * 1024 * 1024
LB = V7X_LANES

Q_TILE = 256
N_QT = T_ALL // Q_TILE
NAT_QROWS = Q_TILE // GRID_W
NAT_KROWS = 12
NAT_KWIN = NAT_KROWS * GRID_W
CHUNK = 64
HG = 4
HGW = HG * HEAD_DIM
NEG_BIG = -1e30


def _cparams(sem, vmem=V7X_VMEM_LIMIT):
    return pltpu.CompilerParams(dimension_semantics=sem, vmem_limit_bytes=vmem)


def _split_bf16(a):
    hi = a.astype(BF16)
    lo = (a - hi.astype(F32)).astype(BF16)
    return hi, lo


def _dot(a, b):
    return jnp.dot(a, b, preferred_element_type=F32)


def _dot_nt(a, b):
    return lax.dot_general(a, b, (((1,), (1,)), ((), ())), preferred_element_type=F32)


def _dot3(a, b):
    ah, al = _split_bf16(a)
    bh, bl = _split_bf16(b)
    return _dot(ah, bh) + _dot(ah, bl) + _dot(al, bh)


def _split3_bf16(a):
    a0 = a.astype(BF16)
    r1 = a - a0.astype(F32)
    a1 = r1.astype(BF16)
    return a0, a1, (r1 - a1.astype(F32)).astype(BF16)


def _dot3_exact_rhs(a, b3_bf16):
    return _dot(jnp.concatenate(_split3_bf16(a), axis=1), b3_bf16)


def _sigmoid(x):
    return 1.0 / (1.0 + jnp.exp(-x))


def _lane(shape):
    return lax.broadcasted_iota(jnp.int32, shape, len(shape) - 1)


def _rope(x, cos, sin, half):
    n = x.shape[-1]
    lo = (_lane(x.shape) % (2 * half)) < half
    partner = jnp.where(lo, pltpu.roll(x, n - half, 1), pltpu.roll(x, half, 1))
    return x * cos + partner * sin


def _mod_vec(mb_ref, mc_ref, k, is_ctx):
    lat = mb_ref[0, :, k * D_MODEL:(k + 1) * D_MODEL]
    ctx = mc_ref[0, :, k * D_MODEL:(k + 1) * D_MODEL]
    return jnp.where(is_ctx, ctx, lat)


def _is_ctx_rows(tm, tile_axis):
    row = pl.program_id(tile_axis) * tm + lax.broadcasted_iota(jnp.int32, (tm, 1), 0)
    return row < CTX_LEN


def _norm_mod(x, g, shift, scale):
    y = x * lax.rsqrt(jnp.mean(x * x, axis=-1, keepdims=True) + NORM_EPS) * g
    return y * (1.0 + scale) + shift


def _ada_kernel(c_ref, w_ref, b_ref, o_ref):
    c = c_ref[...]
    s = c * _sigmoid(c)
    o_ref[0] = _dot3(s, w_ref[0]) + b_ref[0]


def _ada(cc, w_ada, b_ada):
    n_l, _, n_out = w_ada.shape
    tn = 1536
    rows = cc.shape[0]
    return pl.pallas_call(
        _ada_kernel,
        grid=(n_l, n_out // tn),
        in_specs=[pl.BlockSpec((rows, D_MODEL), lambda l, j: (0, 0)),
                  pl.BlockSpec((1, D_MODEL, tn), lambda l, j: (l, 0, j)),
                  pl.BlockSpec((1, 1, tn), lambda l, j: (l, 0, j))],
        out_specs=pl.BlockSpec((1, rows, tn), lambda l, j: (l, 0, j)),
        out_shape=jax.ShapeDtypeStruct((n_l, rows, n_out), F32),
        compiler_params=_cparams(("arbitrary", "arbitrary")),
        name="ada",
    )(cc, w_ada, b_ada.reshape(n_l, 1, n_out))


def _mod_specs(n_b, grid_rank):
    if grid_rank == 2:
        return (pl.BlockSpec((1, 1, 6 * D_MODEL), lambda b, j: (b, 0, 0)),
                pl.BlockSpec((1, 1, 6 * D_MODEL), lambda b, j: (n_b, 0, 0)))
    return (pl.BlockSpec((1, 1, 6 * D_MODEL), lambda b, j, f: (b, 0, 0)),
            pl.BlockSpec((1, 1, 6 * D_MODEL), lambda b, j, f: (n_b, 0, 0)))


def _norm_kernel(x_ref, g_ref, mb_ref, mc_ref, h_ref, *, tm):
    is_ctx = _is_ctx_rows(tm, 1)
    h = _norm_mod(x_ref[0], g_ref[...], _mod_vec(mb_ref, mc_ref, 0, is_ctx), _mod_vec(mb_ref, mc_ref, 1, is_ctx))
    h_ref[0] = h.astype(h_ref.dtype)


def _norm1(x, g, modl):
    n_b = x.shape[0]
    tm = 768
    mb, mc = _mod_specs(n_b, 2)
    return pl.pallas_call(
        functools.partial(_norm_kernel, tm=tm),
        grid=(n_b, T_ALL // tm),
        in_specs=[pl.BlockSpec((1, tm, D_MODEL), lambda b, j: (b, j, 0)),
                  pl.BlockSpec((1, D_MODEL), lambda b, j: (0, 0)), mb, mc],
        out_specs=pl.BlockSpec((1, tm, D_MODEL), lambda b, j: (b, j, 0)),
        out_shape=jax.ShapeDtypeStruct(x.shape, BF16),
        compiler_params=_cparams(("parallel", "parallel")),
        name="norm1",
    )(x, g.reshape(1, D_MODEL), modl, modl)


def _mm_kernel(a_ref, w_ref, o_ref):
    o_ref[0] = _dot(a_ref[0], w_ref[...]).astype(o_ref.dtype)


def _mm(a, w, out_dtype, tn, name):
    n_b, t, k = a.shape
    n = w.shape[1]
    return pl.pallas_call(
        _mm_kernel,
        grid=(n_b, n // tn),
        in_specs=[pl.BlockSpec((1, t, k), lambda i, j: (i, 0, 0)),
                  pl.BlockSpec((k, tn), lambda i, j: (0, j))],
        out_specs=pl.BlockSpec((1, t, tn), lambda i, j: (i, 0, j)),
        out_shape=jax.ShapeDtypeStruct((n_b, t, n), out_dtype),
        compiler_params=_cparams(("parallel", "arbitrary")),
        name=name,
    )(a, w)


LOG2E = float(np.log2(np.e))
SUM_LANE = (HEAD_DIM, 0)


def _with_ones_lane(vblk, hh):
    lane = _lane(vblk.shape)
    keep = (lane < HEAD_DIM) if hh == 0 else (lane >= HEAD_DIM)
    return jnp.where(keep, vblk, jnp.where(lane == SUM_LANE[hh], 1.0, 0.0).astype(vblk.dtype))


def _attend_heads(o_ref, q_of, k_of, v_of, extra=None):
    def scores(h):
        q = q_of(h)
        s = _dot_nt(q, k_of(h))
        if extra is None:
            return (s,)
        return (s + extra[0](h), _dot_nt(q, extra[1](h)))

    def finish(h, sc):
        m = jnp.max(sc[0], axis=-1, keepdims=True)
        for s in sc[1:]:
            m = jnp.maximum(m, jnp.max(s, axis=-1, keepdims=True))
        o = _dot(jnp.exp2(sc[0] - m).astype(BF16), v_of(h))
        if extra is not None:
            o = o + _dot(jnp.exp2(sc[1] - m).astype(BF16), extra[2](h))
        lane = SUM_LANE[h % 2]
        return o / o[:, lane:lane + 1]

    nxt = scores(0)
    even = None
    for h in range(N_HEADS):
        cur = nxt
        if h + 1 < N_HEADS:
            nxt = scores(h + 1)
        o = finish(h, cur)
        if h % 2 == 0:
            even = o
        else:
            pair = jnp.where(_lane(o.shape) < HEAD_DIM, even, o)
            o_ref[0, :, LB * (h // 2):LB * (h // 2 + 1)] = pair.astype(o_ref.dtype)


def _pad_head(blk, hh):
    return jnp.where(_lane(blk.shape) < HEAD_DIM, blk if hh == 0 else pltpu.roll(blk, HEAD_DIM, 1), 0.0)


def _pad_heads(blk):
    return _pad_head(blk, 0), _pad_head(blk, 1)


def _head_rms(x, gain):
    ms = jnp.sum(x * x, axis=-1, keepdims=True) * (1.0 / HEAD_DIM)
    return x * lax.rsqrt(ms + NORM_EPS) * gain


def _gqa_kernel(pq_ref, pall_ref, cq_ref, sq_ref, call_ref, sall_ref, qg_ref, kg_ref, o_ref, k_scr, v_scr):
    j = pl.program_id(1)

    @pl.when(j == 0)
    def _prep():
        kblk = pall_ref[0, :, 512:640].astype(F32)
        for g, kh in enumerate(_pad_heads(kblk)):
            kh = _rope(_head_rms(kh, kg_ref[...]), call_ref[...], sall_ref[...], 16)
            k_scr[:, LB * g:LB * (g + 1)] = kh.astype(BF16)
        vblk = pall_ref[0, :, 640:768].astype(F32)
        vrot = pltpu.roll(vblk, HEAD_DIM, 1)
        for i, src in enumerate((vblk, vrot, vrot, vblk)):
            v_scr[:, LB * i:LB * (i + 1)] = _with_ones_lane(src, i % 2).astype(BF16)

    def attend(n_keys):
        group = N_HEADS // A_KV_HEADS

        def q_of(h):
            qh = _pad_head(pq_ref[0, :, LB * (h // 2):LB * (h // 2 + 1)].astype(F32), h % 2)
            qh = _rope(_head_rms(qh, qg_ref[...]), cq_ref[...], sq_ref[...], 16)
            return (qh * (HEAD_DIM ** -0.5 * LOG2E)).astype(BF16)

        def v_of(h):
            i = 2 * (h // group) + h % 2
            return v_scr[0:n_keys, LB * i:LB * (i + 1)]

        _attend_heads(o_ref, q_of, lambda h: k_scr[0:n_keys, LB * (h // group):LB * (h // group + 1)], v_of)

    @pl.when(j == 0)
    def _ctx():
        attend(CTX_LEN)

    @pl.when(j > 0)
    def _lat():
        attend(T_ALL)


def _gqa(p_a, cos, sin, q_gain, k_gain):
    n_b = p_a.shape[0]
    pad = lambda g: jnp.concatenate([g, jnp.zeros((HEAD_DIM,), F32)]).reshape(1, LB)
    tile = lambda b, j: (b, j, 0)
    whole = lambda b, j: (b, 0, 0)
    return pl.pallas_call(
        _gqa_kernel,
        grid=(n_b, N_QT),
        in_specs=[pl.BlockSpec((1, Q_TILE, A_IN), tile),
                  pl.BlockSpec((1, T_ALL, A_IN), whole),
                  pl.BlockSpec((Q_TILE, LB), lambda b, j: (j, 0)),
                  pl.BlockSpec((Q_TILE, LB), lambda b, j: (j, 0)),
                  pl.BlockSpec((T_ALL, LB), lambda b, j: (0, 0)),
                  pl.BlockSpec((T_ALL, LB), lambda b, j: (0, 0)),
                  pl.BlockSpec((1, LB), lambda b, j: (0, 0)),
                  pl.BlockSpec((1, LB), lambda b, j: (0, 0))],
        out_specs=pl.BlockSpec((1, Q_TILE, 512), tile),
        out_shape=jax.ShapeDtypeStruct((n_b, T_ALL, 512), BF16),
        scratch_shapes=[pltpu.VMEM((T_ALL, 256), BF16), pltpu.VMEM((T_ALL, 512), BF16)],
        compiler_params=_cparams(("parallel", "arbitrary")),
        name="gqa",
    )(p_a, p_a, cos, sin, cos, sin, pad(q_gain), pad(k_gain))


def _mla_kernel(pq_ref, pall_ref, cq_ref, sq_ref, call_ref, sall_ref, qg_ref, kvg_ref, wq_ref, wkv_ref,
                o_ref, k_scr, v_scr):
    j = pl.program_id(1)
    kw = N_HEADS * LB

    @pl.when(j == 0)
    def _prep():
        for i in range(N_QT):
            rows = slice(i * Q_TILE, (i + 1) * Q_TILE)
            ckv = pall_ref[0, rows, B_Q_LORA:B_Q_LORA + B_KV_LORA].astype(F32)
            n = ckv * lax.rsqrt(jnp.mean(ckv * ckv, axis=-1, keepdims=True) + NORM_EPS) * kvg_ref[...]
            kv = _dot(n.astype(BF16), wkv_ref[...])
            kr = _rope(pall_ref[0, rows, 640:768].astype(F32), call_ref[rows, :], sall_ref[rows, :], 8)
            for h in range(N_HEADS):
                k_scr[rows, LB * h:LB * (h + 1)] = (kv[:, LB * h:LB * (h + 1)] + kr).astype(BF16)
            for h in range(N_HEADS):
                v_scr[rows, LB * h:LB * (h + 1)] = _with_ones_lane(kv[:, kw + LB * h:kw + LB * (h + 1)], h % 2).astype(BF16)

    def attend(n_keys):
        cq = pq_ref[0, :, 0:B_Q_LORA].astype(F32)
        n = cq * lax.rsqrt(jnp.mean(cq * cq, axis=-1, keepdims=True) + NORM_EPS) * qg_ref[...]
        q = _dot(n.astype(BF16), wq_ref[...])
        scale = (B_NOPE + B_ROPE) ** -0.5 * LOG2E
        _attend_heads(
            o_ref,
            lambda h: (_rope(q[:, LB * h:LB * (h + 1)], cq_ref[...], sq_ref[...], 8) * scale).astype(BF16),
            lambda h: k_scr[0:n_keys, LB * h:LB * (h + 1)],
            lambda h: v_scr[0:n_keys, LB * h:LB * (h + 1)])

    @pl.when(j == 0)
    def _ctx():
        attend(CTX_LEN)

    @pl.when(j > 0)
    def _lat():
        attend(T_ALL)


def _mla(p, cos, sin, q_gain, kv_gain, wq, wkv):
    n_b = p.shape[0]
    tile = lambda b, j: (b, j, 0)
    const = lambda b, j: (0, 0)
    return pl.pallas_call(
        _mla_kernel,
        grid=(n_b, N_QT),
        in_specs=[pl.BlockSpec((1, Q_TILE, B_PAD), lambda b, j: (b, j, PROJ_B_BLOCK)),
                  pl.BlockSpec((1, T_ALL, B_PAD), lambda b, j: (b, 0, PROJ_B_BLOCK)),
                  pl.BlockSpec((Q_TILE, LB), lambda b, j: (j, 0)),
                  pl.BlockSpec((Q_TILE, LB), lambda b, j: (j, 0)),
                  pl.BlockSpec((T_ALL, LB), const),
                  pl.BlockSpec((T_ALL, LB), const),
                  pl.BlockSpec((1, B_Q_LORA), const),
                  pl.BlockSpec((1, B_KV_LORA), const),
                  pl.BlockSpec(wq.shape, const),
                  pl.BlockSpec(wkv.shape, const)],
        out_specs=pl.BlockSpec((1, Q_TILE, 512), tile),
        out_shape=jax.ShapeDtypeStruct((n_b, T_ALL, 512), BF16),
        scratch_shapes=[pltpu.VMEM((T_ALL, N_HEADS * LB), BF16), pltpu.VMEM((T_ALL, N_HEADS * LB), BF16)],
        compiler_params=_cparams(("parallel", "arbitrary")),
        name="mla",
    )(p, p, cos, sin, cos, sin, q_gain.reshape(1, -1), kv_gain.reshape(1, -1), wq, wkv)


def _nat_kernel(pq_ref, pall_ref, bias_ref, o_ref, k_scr, v_scr):
    j = pl.program_id(1)

    @pl.when(j == 0)
    def _prep():
        for jb in range(N_HEADS // 2):
            kblk = pall_ref[0, :, 512 + LB * jb:512 + LB * (jb + 1)].astype(F32)
            for hh, kh in enumerate(_pad_heads(kblk)):
                h = 2 * jb + hh
                k_scr[:, LB * h:LB * (h + 1)] = kh.astype(BF16)
            vblk = pall_ref[0, :, 1024 + LB * jb:1024 + LB * (jb + 1)]
            for hh in range(2):
                v_scr[:, LB * (2 * jb + hh):LB * (2 * jb + hh + 1)] = _with_ones_lane(vblk, hh)

    def q_of(h):
        qh = _pad_head(pq_ref[0, :, LB * (h // 2):LB * (h // 2 + 1)].astype(F32), h % 2)
        return (qh * (HEAD_DIM ** -0.5 * LOG2E)).astype(BF16)

    hs = lambda h: slice(LB * h, LB * (h + 1))

    @pl.when(j == 0)
    def _ctx():
        _attend_heads(o_ref, q_of, lambda h: k_scr[0:CTX_LEN, hs(h)], lambda h: v_scr[0:CTX_LEN, hs(h)])

    @pl.when(j > 0)
    def _lat():
        first_row = jnp.clip(NAT_QROWS * (j - 1) - NA_ROWS // 2, 0, N_ROWS - NAT_KROWS)
        start = pl.multiple_of(CTX_LEN + first_row * GRID_W, Q_TILE)
        win = pl.ds(start, NAT_KWIN)
        _attend_heads(o_ref, q_of, lambda h: k_scr[win, hs(h)], lambda h: v_scr[win, hs(h)],
                      extra=(lambda h: bias_ref[0, h],
                             lambda h: k_scr[0:CTX_LEN, hs(h)],
                             lambda h: v_scr[0:CTX_LEN, hs(h)]))


def _nat_bias_table(rel_bias):
    n_dr, n_dc = 2 * NA_ROWS - 1, 2 * NA_COLS - 1
    cols = np.arange(GRID_W)
    pick_col = (cols[None, None, :] - cols[None, :, None] + NA_COLS - 1 == np.arange(n_dc)[:, None, None])
    bias_log2 = rel_bias.astype(F32) * LOG2E
    by_col = jnp.einsum('hdc,cab->hdab', bias_log2, jnp.asarray(pick_col, F32), precision=lax.Precision.HIGHEST)
    zero_block = jnp.zeros((N_HEADS, GRID_W, GRID_W), F32)
    valid = np.zeros((3, Q_TILE, NAT_KWIN), bool)
    tabs = []
    for t, qb in enumerate((0, 1, N_ROWS // NAT_QROWS - 1)):
        first_row = int(np.clip(NAT_QROWS * qb - NA_ROWS // 2, 0, N_ROWS - NAT_KROWS))
        row_blocks = []
        for rq in range(NAT_QROWS * qb, NAT_QROWS * (qb + 1)):
            offs = [first_row + k - rq + NA_ROWS - 1 for k in range(NAT_KROWS)]
            row_blocks.append(jnp.concatenate([by_col[:, d] if 0 <= d < n_dr else zero_block for d in offs], axis=-1))
        tabs.append(jnp.concatenate(row_blocks, axis=-2))
        ql, kl = np.arange(Q_TILE), np.arange(NAT_KWIN)
        r, c = NAT_QROWS * qb + ql // GRID_W, ql % GRID_W
        kr, kc = first_row + kl // GRID_W, kl % GRID_W
        r0 = np.clip(r - NA_ROWS // 2, 0, N_ROWS - NA_ROWS)
        c0 = np.clip(c - NA_COLS // 2, 0, GRID_W - NA_COLS)
        valid[t] = ((kr[None, :] >= r0[:, None]) & (kr[None, :] < r0[:, None] + NA_ROWS)
                    & (kc[None, :] >= c0[:, None]) & (kc[None, :] < c0[:, None] + NA_COLS))
    return jnp.where(jnp.asarray(valid)[:, None], jnp.stack(tabs), NEG_BIG)


def _nat(p, bias_tab):
    n_b = p.shape[0]
    tile = lambda b, j: (b, j, 0)
    n_lat_tiles = N_QT - 1

    def bias_idx(b, j):
        qb = j - 1
        return (jnp.where(qb <= 0, 0, jnp.where(qb == n_lat_tiles - 1, 2, 1)), 0, 0, 0)

    return pl.pallas_call(
        _nat_kernel,
        grid=(n_b, N_QT),
        in_specs=[pl.BlockSpec((1, Q_TILE, D_IN), lambda b, j: (b, j, PROJ_D_BLOCK)),
                  pl.BlockSpec((1, T_ALL, D_IN), lambda b, j: (b, 0, PROJ_D_BLOCK)),
                  pl.BlockSpec((1, N_HEADS, Q_TILE, NAT_KWIN), bias_idx)],
        out_specs=pl.BlockSpec((1, Q_TILE, 512), tile),
        out_shape=jax.ShapeDtypeStruct((n_b, T_ALL, 512), BF16),
        scratch_shapes=[pltpu.VMEM((T_ALL, N_HEADS * LB), BF16), pltpu.VMEM((T_ALL, N_HEADS * LB), BF16)],
        compiler_params=_cparams(("parallel", "arbitrary")),
        name="nat",
    )(p, p, bias_tab)


def _seg_sum(x, ones3):
    return _dot3_exact_rhs(x, ones3)


def _head_ones3():
    ones_bd = np.kron(np.eye(N_HEADS), np.ones((HEAD_DIM, HEAD_DIM)))
    return jnp.asarray(np.concatenate([ones_bd] * 3, axis=0), BF16)


def _rwkv_prep_kernel(z_ref, zp_ref, zn_ref, mu_ref, w0_ref, a0_ref, kk_ref, ka_ref, rk_ref,
                      wd3_ref, wa3_ref, wg3_ref, ones_ref,
                      r_out, v_out, kk_out, kd_out, lw_out, bd_out, bonus_out, g_out):
    j = pl.program_id(1)
    z = z_ref[0]
    tm = z.shape[0]
    row = lax.broadcasted_iota(jnp.int32, (tm, 1), 0)
    prev_row = jnp.where(j <= 1, 0.0, zp_ref[0, 7:8, :])
    next_row = jnp.where((j == 0) | (j == N_QT - 1), 0.0, zn_ref[0, 0:1, :])
    z_prev = jnp.where(row == 0, prev_row, pltpu.roll(z, 1, 0))
    z_next = jnp.where(row == tm - 1, next_row, pltpu.roll(z, tm - 1, 0))
    zs = z + (0.5 * (z_prev + z_next) - z) * mu_ref[...]

    r = zs[:, 0:512]
    k = zs[:, 512:1024]
    v = zs[:, 1024:1536]
    w_lo = zs[:, 1536:1664]
    a_lo = zs[:, 1664:1792]
    g_lo = zs[:, 1792:1920]
    ones_bd = ones_ref[...]

    def lora(x, w3_ref):
        xh, xl = _split_bf16(x)
        return _dot(jnp.concatenate([xh, xh, xl], axis=1), w3_ref[...])

    kkr = k * kk_ref[...]
    nrm = jnp.maximum(jnp.sqrt(_seg_sum(kkr * kkr, ones_bd)), 1e-12)
    kk = kkr / nrm
    dec = lora(jnp.tanh(w_lo), wd3_ref)
    aaa = lora(a_lo, wa3_ref)
    g = lora(_sigmoid(g_lo), wg3_ref)
    r_out[0] = r
    v_out[0] = v
    kk_out[0] = kk
    g_out[0] = g.astype(g_out.dtype)
    ksum = None
    for d in range(2):
        u = -(w0_ref[d:d + 1, :] + dec[:, 512 * d:512 * (d + 1)])
        softplus = jnp.maximum(u, 0.0) + jnp.log(1.0 + jnp.exp(-jnp.abs(u)))
        logw = -softplus - 0.5
        lw_out[d, 0] = -jnp.exp(logw)
        a = _sigmoid(a0_ref[d:d + 1, :] + aaa[:, 512 * d:512 * (d + 1)])
        bd_out[d, 0] = kk * a
        kd = k * (1.0 + (a - 1.0) * ka_ref[...])
        kd_out[d, 0] = kd
        ksum = kd if ksum is None else ksum + kd
    bonus_out[0] = (_seg_sum(r * ksum * rk_ref[...], ones_bd) * v).astype(bonus_out.dtype)


def _rwkv_prep(p_c, lp):
    n_b = p_c.shape[0]
    tile = lambda b, j: (b, j, 0)
    const = lambda b, j: (0, 0)
    blocks8 = T_ALL // 8
    tpb = Q_TILE // 8
    prev = lambda b, j: (b, jnp.maximum(j * tpb - 1, 0), 0)
    nxt = lambda b, j: (b, jnp.minimum((j + 1) * tpb, blocks8 - 1), 0)
    o3 = jax.ShapeDtypeStruct((n_b, T_ALL, C_W), F32)
    o3_bf16 = jax.ShapeDtypeStruct((n_b, T_ALL, C_W), BF16)
    o4 = jax.ShapeDtypeStruct((2, n_b, T_ALL, C_W), F32)
    s3 = pl.BlockSpec((1, Q_TILE, C_W), tile)
    s4 = pl.BlockSpec((2, 1, Q_TILE, C_W), lambda b, j: (0, b, j, 0))
    small = [lp['c_mu'].reshape(1, C_IN), lp['c_w0'], lp['c_a0'], lp['c_k_k'].reshape(1, C_W),
             lp['c_k_a'].reshape(1, C_W), lp['c_r_k'].reshape(1, C_W)]
    bd2 = lambda w: jnp.concatenate(
        [jnp.concatenate([w[0], jnp.zeros_like(w[0])], 1), jnp.concatenate([jnp.zeros_like(w[1]), w[1]], 1)], 0)
    mats = []
    for w in (bd2(lp['c_w_decay']), bd2(lp['c_w_aaa']), lp['c_w_gate']):
        hi, lo = _split_bf16(w)
        mats.append(jnp.concatenate([hi, lo, hi], axis=0))
    ins = small + mats + [_head_ones3()]
    return pl.pallas_call(
        _rwkv_prep_kernel,
        grid=(n_b, N_QT),
        in_specs=[pl.BlockSpec((1, Q_TILE, C_IN), tile),
                  pl.BlockSpec((1, 8, C_IN), prev),
                  pl.BlockSpec((1, 8, C_IN), nxt)] + [pl.BlockSpec(a.shape, const) for a in ins],
        out_specs=[s3, s3, s3, s4, s4, s4, s3, s3],
        out_shape=[o3, o3, o3, o4, o4, o4, o3_bf16, o3_bf16],
        compiler_params=_cparams(("parallel", "parallel")),
        name="rwkv_prep",
    )(p_c, p_c, p_c, *ins)


def _dot3_exact_rhs_lhs(m3_bf16, a):
    return _dot(m3_bf16, jnp.concatenate(_split3_bf16(a), axis=0))


N_INV_LEVELS = 6
SCAN_BATCH = 2
SCAN_TILE = 256
MK_STRICT, MK_INCL, MK_LEVEL0 = 0, 1, 2


def _scan_masks():
    masks, cums = [], []
    for fwd in (True, False):
        t = np.arange(CHUNK) if fwd else CHUNK - 1 - np.arange(CHUNK)
        tr, tc = t[:, None], t[None, :]
        levels = [((tr // (2 * h)) == (tc // (2 * h))) & (((tr // h) % 2) == 1) & (((tc // h) % 2) == 0)
                  for h in (2 ** k for k in range(N_INV_LEVELS))]
        masks.append(np.tile(np.stack([tr > tc, tr >= tc] + levels), (1, 1, HG)))
        cums.append(np.tile(tr >= tc, (1, 3)))
    return jnp.asarray(np.stack(masks), F32), jnp.asarray(np.stack(cums), BF16)


def _rwkv_scan_kernel(rf_ref, vf_ref, kkf_ref, kdf_ref, lwf_ref, bdf_ref,
                      rb_ref, vb_ref, kkb_ref, kdb_ref, lwb_ref, bdb_ref, mk_ref, cum_ref, yf_ref, yb_ref, s_scr):
    j = pl.program_id(1)
    lane_head = lax.broadcasted_iota(jnp.int32, (CHUNK, HGW), 1) // HEAD_DIM
    eye = (lax.broadcasted_iota(jnp.int32, (CHUNK, HGW), 0)
           == lax.broadcasted_iota(jnp.int32, (CHUNK, HGW), 1) % CHUNK).astype(F32)
    dir_refs = ((rf_ref, vf_ref, kkf_ref, kdf_ref, lwf_ref, bdf_ref, yf_ref),
                (rb_ref, vb_ref, kkb_ref, kdb_ref, lwb_ref, bdb_ref, yb_ref))
    chunks_per_tile = SCAN_TILE // CHUNK

    def masked(d, k, x):
        return jnp.where(mk_ref[d, k] > 0.5, x, 0.0)

    def stack(x):
        return jnp.concatenate([jnp.where(lane_head == p, x, jnp.zeros_like(x)) for p in range(HG)], axis=0)

    def head_transpose(x):
        t = stack(x).T
        return t[0:HEAD_DIM] + t[HEAD_DIM:2 * HEAD_DIM] + t[2 * HEAD_DIM:3 * HEAD_DIM] + t[3 * HEAD_DIM:4 * HEAD_DIM]

    @pl.when(j == 0)
    def _init():
        s_scr[...] = jnp.zeros_like(s_scr)

    def body(i, carry):
        ch = []
        for bi in range(SCAN_BATCH):
            for d in range(2):
                r_ref, v_ref, kk_ref, kd_ref, lw_ref, bd_ref, y_ref = dir_refs[d]
                c = i if d == 0 else chunks_per_tile - 1 - i
                rows = pl.ds(pl.multiple_of(c * CHUNK, CHUNK), CHUNK)
                lw = lw_ref[0, bi, rows, :]
                cum = _dot3_exact_rhs_lhs(cum_ref[d], lw)
                total = jnp.sum(lw, axis=0, keepdims=True)
                e_pos, e_neg, e_prev, e_rest = jnp.exp(cum), jnp.exp(-cum), jnp.exp(cum - lw), jnp.exp(total - cum)
                p_end = jnp.broadcast_to(jnp.exp(total), (CHUNK, C_W))
                for g in range(C_W // HGW):
                    ln = slice(g * HGW, (g + 1) * HGW)
                    kk, bd, kd = kk_ref[bi, rows, ln], bd_ref[0, bi, rows, ln], kd_ref[0, bi, rows, ln]
                    al = (kk * e_prev[:, ln]).astype(BF16)
                    rh = (r_ref[bi, rows, ln] * e_pos[:, ln]).astype(BF16)
                    ch.append(dict(
                        idx=(2 * bi + d) * (C_W // HGW) + g, d=d, bi=bi, rows=rows, ln=ln, y_ref=y_ref,
                        al=al, rh=rh, al_rh=jnp.concatenate([al, rh], axis=0), al4=stack(al),
                        be4=stack((bd * e_neg[:, ln]).astype(BF16)), ka4=stack((kd * e_neg[:, ln]).astype(BF16)),
                        bee_t=head_transpose(bd * e_rest[:, ln]).astype(BF16),
                        kae_t=head_transpose(kd * e_rest[:, ln]).astype(BF16),
                        p_end_t=head_transpose(p_end[:, ln]),
                        v4=stack(v_ref[bi, rows, ln].astype(BF16))))

        def stage(fn):
            for c_ in ch:
                c_.update(fn(c_, c_['d']))

        def nt_masks(c_, d, key, lo_name, hi_name, lo_f32=False):
            prod = _dot_nt(c_['al_rh'], c_[key])
            lo = masked(d, MK_STRICT, prod[0:CHUNK])
            return {lo_name: lo if lo_f32 else lo.astype(BF16),
                    hi_name: masked(d, MK_INCL, prod[CHUNK:2 * CHUNK]).astype(BF16)}

        stage(lambda c_, d: nt_masks(c_, d, 'be4', 'l_ab', 'm_rb', lo_f32=True))
        stage(lambda c_, d: nt_masks(c_, d, 'ka4', 'l_ak', 'm_rk'))

        def value_products(c_, d):
            prod = _dot(jnp.concatenate([c_['l_ak'], c_['kae_t'], c_['m_rk']], axis=0), c_['v4'])
            return dict(lv4=stack(prod[0:CHUNK].astype(BF16)), hv=prod[CHUNK:2 * CHUNK], yv=prod[2 * CHUNK:3 * CHUNK])

        stage(lambda c_, d: dict(tb=(eye - masked(d, MK_LEVEL0, c_['l_ab'])).astype(BF16)))
        for level in range(1, N_INV_LEVELS):
            stage(lambda c_, d: dict(
                t4=stack(_dot(masked(d, MK_LEVEL0 + level, c_['l_ab']).astype(BF16), stack(c_['tb'])).astype(BF16))))
            if level == 1:
                stage(value_products)
            stage(lambda c_, d: dict(tb=c_['tb'] - _dot(c_['tb'], c_['t4']).astype(BF16)))
        stage(lambda c_, d: dict(w=_dot(c_['tb'], c_['al4']).astype(BF16)))
        stage(lambda c_, d: dict(u0=_dot(c_['tb'], c_['lv4'])))
        stage(lambda c_, d: dict(s=s_scr[c_['idx']]))
        stage(lambda c_, d: dict(on_s=_dot(jnp.concatenate([c_['w'], c_['rh']], axis=0), stack(c_['s'].astype(BF16)))))
        stage(lambda c_, d: dict(u4=stack((c_['on_s'][0:CHUNK] + c_['u0']).astype(BF16))))
        stage(lambda c_, d: dict(on_u=_dot(jnp.concatenate([c_['m_rb'], c_['bee_t']], axis=0), c_['u4'])))
        for c_ in ch:
            s_scr[c_['idx']] = c_['s'] * c_['p_end_t'] - c_['on_u'][CHUNK:2 * CHUNK] + c_['hv']
        for c_ in ch:
            y = c_['on_s'][CHUNK:2 * CHUNK] - c_['on_u'][0:CHUNK] + c_['yv']
            c_['y_ref'][c_['bi'], c_['rows'], c_['ln']] = y.astype(c_['y_ref'].dtype)
        return carry

    lax.fori_loop(0, chunks_per_tile, body, 0)


def _rwkv_scan(r, v, kk, kd, lw, bd):
    n_b = r.shape[0]
    assert n_b % SCAN_BATCH == 0
    n_tiles, ctx_tiles = T_ALL // SCAN_TILE, CTX_LEN // SCAN_TILE
    bwd_tile = lambda j: jnp.where(j < ctx_tiles, ctx_tiles - 1 - j, n_tiles + ctx_tiles - 1 - j)
    f3 = pl.BlockSpec((SCAN_BATCH, SCAN_TILE, C_W), lambda b, j: (b, j, 0))
    f4 = pl.BlockSpec((1, SCAN_BATCH, SCAN_TILE, C_W), lambda b, j: (0, b, j, 0))
    b3 = pl.BlockSpec((SCAN_BATCH, SCAN_TILE, C_W), lambda b, j: (b, bwd_tile(j), 0))
    b4 = pl.BlockSpec((1, SCAN_BATCH, SCAN_TILE, C_W), lambda b, j: (1, b, bwd_tile(j), 0))
    y_shape = jax.ShapeDtypeStruct((n_b, T_ALL, C_W), BF16)
    masks, cums = _scan_masks()
    return pl.pallas_call(
        _rwkv_scan_kernel,
        grid=(n_b // SCAN_BATCH, n_tiles),
        in_specs=[f3, f3, f3, f4, f4, f4, b3, b3, b3, b4, b4, b4,
                  pl.BlockSpec(masks.shape, lambda b, j: (0, 0, 0, 0)),
                  pl.BlockSpec(cums.shape, lambda b, j: (0, 0, 0))],
        out_specs=[f3, b3],
        out_shape=[y_shape, y_shape],
        scratch_shapes=[pltpu.VMEM((SCAN_BATCH * 2 * (C_W // HGW), HEAD_DIM, HGW), F32)],
        compiler_params=_cparams(("parallel", "arbitrary")),
        name="rwkv_scan",
    )(r, v, kk, kd, lw, bd, r, v, kk, kd, lw, bd, masks, cums)


def _rwkv_out_kernel(yf_ref, yb_ref, bonus_ref, g_ref, gw_ref, gb_ref, ones_ref, o_ref):
    y = yf_ref[0].astype(F32) + yb_ref[0].astype(F32)
    ones_bd = ones_ref[...]
    mean = _seg_sum(y, ones_bd) * (1.0 / HEAD_DIM)
    yc = y - mean
    var = _seg_sum(yc * yc, ones_bd) * (1.0 / HEAD_DIM)
    yn = yc * lax.rsqrt(var + C_GN_EPS) * gw_ref[...] + gb_ref[...]
    o_ref[0] = ((yn + bonus_ref[0].astype(F32)) * g_ref[0].astype(F32)).astype(o_ref.dtype)


def _rwkv_out(y_f, y_b, bonus, g, gn_w, gn_b):
    n_b = bonus.shape[0]
    tm = 768
    tile = lambda b, j: (b, j, 0)
    const = lambda b, j: (0, 0)
    ones_bd = _head_ones3()
    return pl.pallas_call(
        _rwkv_out_kernel,
        grid=(n_b, T_ALL // tm),
        in_specs=[pl.BlockSpec((1, tm, C_W), tile), pl.BlockSpec((1, tm, C_W), tile),
                  pl.BlockSpec((1, tm, C_W), tile), pl.BlockSpec((1, tm, C_W), tile),
                  pl.BlockSpec((1, C_W), const), pl.BlockSpec((1, C_W), const),
                  pl.BlockSpec(ones_bd.shape, const)],
        out_specs=pl.BlockSpec((1, tm, C_W), tile),
        out_shape=jax.ShapeDtypeStruct((n_b, T_ALL, C_W), BF16),
        compiler_params=_cparams(("parallel", "parallel")),
        name="rwkv_out",
    )(y_f, y_b, bonus, g, gn_w.reshape(1, C_W), gn_b.reshape(1, C_W), ones_bd)


def _merge_kernel(x_ref, oa_ref, ob_ref, oc_ref, od_ref, gate_ref, wb_ref, wo_ref, mb_ref, mc_ref, out_ref, *, tm):
    is_ctx = _is_ctx_rows(tm, 1)
    y = None
    for i, o_ref in enumerate((oa_ref, ob_ref, oc_ref, od_ref)):
        z = _dot(o_ref[0], wb_ref[i])
        sg = _sigmoid(gate_ref[0, :, i * D_MODEL:(i + 1) * D_MODEL].astype(F32))
        y = sg * z if y is None else y + sg * z
    z = _dot(y.astype(BF16), wo_ref[...])
    out_ref[0] = x_ref[0] + _mod_vec(mb_ref, mc_ref, 2, is_ctx) * z


def _merge(x, outs, gates, wb, wo, modl):
    n_b = x.shape[0]
    tm = 768
    tile = lambda b, j: (b, j, 0)
    mb, mc = _mod_specs(n_b, 2)
    o_spec = pl.BlockSpec((1, tm, 512), tile)
    return pl.pallas_call(
        functools.partial(_merge_kernel, tm=tm),
        grid=(n_b, T_ALL // tm),
        in_specs=[pl.BlockSpec((1, tm, D_MODEL), tile), o_spec, o_spec, o_spec, o_spec,
                  pl.BlockSpec((1, tm, GATE_IN), tile),
                  pl.BlockSpec(wb.shape, lambda b, j: (0, 0, 0)),
                  pl.BlockSpec(wo.shape, lambda b, j: (0, 0)), mb, mc],
        out_specs=pl.BlockSpec((1, tm, D_MODEL), tile),
        out_shape=jax.ShapeDtypeStruct(x.shape, F32),
        compiler_params=_cparams(("parallel", "parallel")),
        name="merge",
    )(x, *outs, gates, wb, wo, modl, modl)


def _mlp_kernel(x_ref, g_ref, mb_ref, mc_ref, w1_ref, w2_ref, out_ref, h_scr, acc_scr, *, tm, n_f):
    f = pl.program_id(2)
    is_ctx = _is_ctx_rows(tm, 1)

    @pl.when(f == 0)
    def _init():
        h = _norm_mod(x_ref[0], g_ref[...], _mod_vec(mb_ref, mc_ref, 3, is_ctx), _mod_vec(mb_ref, mc_ref, 4, is_ctx))
        h_scr[...] = h.astype(BF16)
        acc_scr[...] = jnp.zeros_like(acc_scr)

    a = jnp.square(jnp.maximum(_dot(h_scr[...], w1_ref[...]), 0.0))
    acc_scr[...] += _dot(a.astype(BF16), w2_ref[...])

    @pl.when(f == n_f - 1)
    def _fin():
        out_ref[0] = x_ref[0] + _mod_vec(mb_ref, mc_ref, 5, is_ctx) * acc_scr[...]


def _mlp(x, g, w1, w2, modl):
    n_b = x.shape[0]
    tm, tf = 1152, 2048
    n_f = D_FF // tf
    tile = lambda b, j, f: (b, j, 0)
    mb, mc = _mod_specs(n_b, 3)
    return pl.pallas_call(
        functools.partial(_mlp_kernel, tm=tm, n_f=n_f),
        grid=(n_b, T_ALL // tm, n_f),
        in_specs=[pl.BlockSpec((1, tm, D_MODEL), tile),
                  pl.BlockSpec((1, D_MODEL), lambda b, j, f: (0, 0)), mb, mc,
                  pl.BlockSpec((D_MODEL, tf), lambda b, j, f: (0, f)),
                  pl.BlockSpec((tf, D_MODEL), lambda b, j, f: (f, 0))],
        out_specs=pl.BlockSpec((1, tm, D_MODEL), tile),
        out_shape=jax.ShapeDtypeStruct(x.shape, F32),
        scratch_shapes=[pltpu.VMEM((tm, D_MODEL), BF16), pltpu.VMEM((tm, D_MODEL), F32)],
        compiler_params=_cparams(("parallel", "parallel", "arbitrary")),
        name="mlp",
    )(x, g.reshape(1, D_MODEL), modl, modl, w1, w2)


def _final_kernel(x_ref, g_ref, o_ref):
    x = x_ref[0]
    o_ref[0] = x * lax.rsqrt(jnp.mean(x * x, axis=-1, keepdims=True) + NORM_EPS) * g_ref[...]


def _final_norm(x, g):
    n_b = x.shape[0]
    tm = Q_TILE
    return pl.pallas_call(
        _final_kernel,
        grid=(n_b, SEQ // tm),
        in_specs=[pl.BlockSpec((1, tm, D_MODEL), lambda b, j: (b, j + CTX_LEN // tm, 0)),
                  pl.BlockSpec((1, D_MODEL), lambda b, j: (0, 0))],
        out_specs=pl.BlockSpec((1, tm, D_MODEL), lambda b, j: (b, j, 0)),
        out_shape=jax.ShapeDtypeStruct((n_b, SEQ, D_MODEL), F32),
        compiler_params=_cparams(("parallel", "parallel")),
        name="final_norm",
    )(x, g.reshape(1, D_MODEL))


def _rope_tables(half, lane0):
    t = np.arange(SEQ)
    inv = ROPE_THETA ** (-np.arange(half, dtype=np.float64) / half)
    cos = np.ones((T_ALL, V7X_LANES), np.float64)
    sin = np.zeros((T_ALL, V7X_LANES), np.float64)
    for part, pos in enumerate((t // GRID_W, t % GRID_W)):
        ang = pos[:, None].astype(np.float64) * inv[None, :]
        ang = ang.astype(np.float32).astype(np.float64)
        base = lane0 + 2 * half * part
        cos[CTX_LEN:, base:base + half] = np.cos(ang)
        cos[CTX_LEN:, base + half:base + 2 * half] = np.cos(ang)
        sin[CTX_LEN:, base:base + half] = -np.sin(ang)
        sin[CTX_LEN:, base + half:base + 2 * half] = np.sin(ang)
    return jnp.asarray(cos, F32), jnp.asarray(sin, F32)


def _layer_weights(l, w_in, b_w_q_up, b_w_kv_up, w_branch, w_out, w_mlp1, w_mlp2):
    wi = w_in[l]
    o_b = A_IN
    o_c = o_b + B_IN
    o_d = o_c + C_IN
    o_g = o_d + D_IN
    w_a = wi[:, :o_b]
    wb_raw = wi[:, o_b:o_c]
    z = lambda n: jnp.zeros((D_MODEL, n), F32)
    w_b = jnp.concatenate([wb_raw[:, :B_Q_LORA + B_KV_LORA], z(64), wb_raw[:, B_Q_LORA + B_KV_LORA:], z(32)], 1)
    wq = b_w_q_up[l].reshape(B_Q_LORA, N_HEADS, B_NOPE + B_ROPE)
    wq = jnp.concatenate([wq, jnp.zeros((B_Q_LORA, N_HEADS, LB - B_NOPE - B_ROPE), F32)], -1)
    wkv = b_w_kv_up[l].reshape(B_KV_LORA, N_HEADS, 2 * HEAD_DIM)
    zk = jnp.zeros((B_KV_LORA, N_HEADS, HEAD_DIM), F32)
    wk = jnp.concatenate([wkv[:, :, :B_NOPE], zk], -1)
    even = (jnp.arange(N_HEADS) % 2 == 0)[None, :, None]
    wv = jnp.concatenate([jnp.where(even, wkv[:, :, B_NOPE:], 0.0), jnp.where(even, 0.0, wkv[:, :, B_NOPE:])], -1)
    return dict(
        w_abd=jnp.concatenate([w_a, w_b, wi[:, o_d:o_g]], 1).astype(BF16),
        w_c=wi[:, o_c:o_d].astype(BF16), w_g=wi[:, o_g:].astype(BF16),
        wq=wq.reshape(B_Q_LORA, N_HEADS * LB).astype(BF16),
        wkv=jnp.concatenate([wk.reshape(B_KV_LORA, -1), wv.reshape(B_KV_LORA, -1)], 1).astype(BF16),
        w_branch=w_branch[l].astype(BF16), w_out=w_out[l].astype(BF16),
        w_mlp1=w_mlp1[l].astype(BF16), w_mlp2=w_mlp2[l].astype(BF16))


def kernel(x, c, ctx, c_ctx, w_ada, b_ada, g_norm1, g_norm2, w_in, a_q_gain, a_k_gain, b_q_gain, b_kv_gain,
           b_w_q_up, b_w_kv_up, c_mu, c_w0, c_w_decay, c_a0, c_w_aaa, c_w_gate, c_k_k, c_k_a, c_r_k,
           c_gn_w, c_gn_b, d_rel_bias, w_branch, w_out, w_mlp1, w_mlp2, g_final):
    n_b = x.shape[0]
    assert x.shape[1:] == (SEQ, D_MODEL) and ctx.shape[1:] == (CTX_LEN, D_MODEL)
    mod_rows = ((n_b + 1 + 7) // 8) * 8
    cc = jnp.concatenate([c, c_ctx[None, :], jnp.zeros((mod_rows - n_b - 1, D_MODEL), F32)], 0)
    mod_all = _ada(cc, w_ada, b_ada)
    cos_a, sin_a = _rope_tables(16, 0)
    cos_b, sin_b = _rope_tables(8, B_NOPE)
    xs = jnp.concatenate([ctx, x], axis=1)
    for l in range(DEPTH):
        lw = _layer_weights(l, w_in, b_w_q_up, b_w_kv_up, w_branch, w_out, w_mlp1, w_mlp2)
        lp = dict(c_mu=c_mu[l], c_w0=c_w0[l], c_w_decay=c_w_decay[l], c_a0=c_a0[l], c_w_aaa=c_w_aaa[l],
                  c_w_gate=c_w_gate[l], c_k_k=c_k_k[l], c_k_a=c_k_a[l], c_r_k=c_r_k[l])
        modl = mod_all[l].reshape(mod_rows, 1, 6 * D_MODEL)
        h = _norm1(xs, g_norm1[l], modl)
        p_abd = _mm(h, lw['w_abd'], BF16, 1024, "w_in_abd")
        p_c = _mm(h, lw['w_c'], F32, 640, "w_in_c")
        p_g = _mm(h, lw['w_g'], BF16, 1024, "w_in_g")
        o_a = _gqa(p_abd, cos_a, sin_a, a_q_gain[l], a_k_gain[l])
        o_b = _mla(p_abd, cos_b, sin_b, b_q_gain[l], b_kv_gain[l], lw['wq'], lw['wkv'])
        r, v, kk, kd, lwd, bd, bonus, g = _rwkv_prep(p_c, lp)
        y_f, y_b = _rwkv_scan(r, v, kk, kd, lwd, bd)
        o_c = _rwkv_out(y_f, y_b, bonus, g, c_gn_w[l], c_gn_b[l])
        o_d = _nat(p_abd, _nat_bias_table(d_rel_bias[l]))
        xs = _merge(xs, (o_a, o_b, o_c, o_d), p_g, lw['w_branch'], lw['w_out'], modl)
        xs = _mlp(xs, g_norm2[l], lw['w_mlp1'], lw['w_mlp2'], modl)
    return _final_norm(xs, g_final)
```
